```python
import jax, jax.numpy as jnp
from jax import lax
import numpy as np

D_MODEL = 4096
BATCH = 4
SEQ = 2048
DEPTH = 2
DEC_BATCH = 128
DEC_SEQ = 4
PAST_LEN = 16384
PAGE_SIZE = 128

HS_A = 64
W_A = 3 * D_MODEL // 8
H_A = W_A // HS_A
R_W = 128
R_A = 128
R_G = 256
A_PROJ = 3 * W_A + R_W + R_A + R_G
RWKV_LN_EPS = 64e-5
W_B = 5 * D_MODEL // 16
DV_B = 256
H_B = W_B // DV_B
DK_B = DV_B // 2
K_B = H_B * DK_B
R_GLA = 16
GLA_TAU = 16.0
B_PROJ = 2 * K_B + W_B + R_GLA + W_B
W_C = 5 * D_MODEL // 16
HD_C = 128
H_C = W_C // HD_C
C_PROJ = 4 * W_C
N_BRANCH = 3
GATE_PROJ = N_BRANCH * D_MODEL
OFF_A = 0
OFF_B = OFF_A + A_PROJ
OFF_C = OFF_B + B_PROJ
OFF_G = OFF_C + C_PROJ
IN_COLS = OFF_G + GATE_PROJ
CHUNK = 32
D_FF = 2 * D_MODEL
CONV_W = 3
RMS_EPS = 1e-6
HEAD_NORM_EPS = 1e-5

kernel_name = 'hybrid_rwkv7_gla_hgrn2_convffn_step'


def rms_norm(x, g):
    x32 = x.astype(jnp.float32)
    y = x32 * lax.rsqrt(jnp.mean(x32 * x32, axis=-1, keepdims=True) + RMS_EPS)
    return (y * g.astype(jnp.float32)).astype(x.dtype)


def head_rms_norm(o, g):
    return o * lax.rsqrt(jnp.mean(o * o, axis=-1, keepdims=True) + HEAD_NORM_EPS) * g.astype(jnp.float32)


def chunked_gated_linear(q, k, v, log_a, S0):
    B, T, H, K = q.shape
    V = v.shape[-1]
    C = min(CHUNK, T)
    n = -(-T // C)
    pad = n * C - T

    def prep(z):
        z = jnp.pad(z, ((0, 0), (0, pad), (0, 0), (0, 0)))
        return z.reshape(B, n, C, H, z.shape[-1]).transpose(1, 0, 3, 2, 4)

    qc, kc, vc, ac = prep(q), prep(k), prep(v), prep(log_a)
    causal = jnp.tril(jnp.ones((C, C), dtype=bool))[:, :, None]

    def step(S, inp):
        q_c, k_c, v_c, a_c = inp
        b = jnp.cumsum(a_c, axis=2)
        o_inter = jnp.einsum('bhck,bhkv->bhcv', q_c * jnp.exp(b), S)
        diff = b[:, :, :, None, :] - b[:, :, None, :, :]
        decay = jnp.where(causal, jnp.exp(jnp.where(causal, diff, 0.0)), 0.0)
        att = jnp.einsum('bhik,bhjk,bhijk->bhij', q_c, k_c, decay)
        o = o_inter + jnp.einsum('bhij,bhjv->bhiv', att, v_c)
        b_last = b[:, :, -1:, :]
        S = (jnp.exp(b_last[:, :, 0, :])[..., None] * S
             + jnp.einsum('bhck,bhcv->bhkv', k_c * jnp.exp(b_last - b), v_c))
        return S, o

    S_T, o = lax.scan(step, S0, (qc, kc, vc, ac))
    o = o.transpose(1, 0, 3, 2, 4).reshape(B, n * C, H, V)[:, :T]
    return o, S_T


def rwkv7_branch(p, shift_buf, S0, mu, w0, w2, a0, a2, g2, k_k, k_a, r_k, ln_g, ln_b):
    f32 = jnp.float32
    B, T, _ = p.shape
    prev = jnp.concatenate([shift_buf[:, None, :].astype(p.dtype), p[:, :-1]], axis=1)
    ps = p + mu * (prev - p)
    r = ps[..., :W_A]
    k = ps[..., W_A:2 * W_A]
    v = ps[..., 2 * W_A:3 * W_A]
    o1 = 3 * W_A
    wc = ps[..., o1:o1 + R_W]
    ac = ps[..., o1 + R_W:o1 + R_W + R_A]
    gc = ps[..., o1 + R_W + R_A:]
    w = -jax.nn.softplus(-(w0 + jnp.tanh(wc) @ w2).astype(f32)) - 0.5
    decay = jnp.exp(-jnp.exp(w))
    a = jax.nn.sigmoid((a0 + ac @ a2).astype(f32))
    g = (jax.nn.sigmoid(gc) @ g2).astype(f32)

    def heads(z):
        return z.reshape(B, T, H_A, HS_A)

    k32 = k.astype(f32)
    kk = heads(k32 * k_k.astype(f32))
    kk = kk * lax.rsqrt(jnp.maximum(jnp.sum(kk * kk, axis=-1, keepdims=True), 1e-24))
    k32 = heads(k32 * (1.0 + (a - 1.0) * k_a.astype(f32)))
    r32 = heads(r.astype(f32))
    v32 = heads(v.astype(f32))
    a_h = heads(a)
    d_h = heads(decay)

    def step(S, inp):
        r_t, d_t, k_t, v_t, kk_t, a_t = inp
        s_kk = jnp.einsum('bhvk,bhk->bhv', S, kk_t)
        S = (S * d_t[:, :, None, :]
             - s_kk[..., None] * (kk_t * a_t)[:, :, None, :]
             + v_t[..., None] * k_t[:, :, None, :])
        return S, jnp.einsum('bhvk,bhk->bhv', S, r_t)

    xs = tuple(jnp.swapaxes(z, 0, 1) for z in (r32, d_h, k32, v32, kk, a_h))
    S_T, o = lax.scan(step, S0.astype(f32), xs)
    o = jnp.swapaxes(o, 0, 1)
    m = jnp.mean(o, axis=-1, keepdims=True)
    var = jnp.mean(jnp.square(o - m), axis=-1, keepdims=True)
    o = ((o - m) * lax.rsqrt(var + RWKV_LN_EPS)).reshape(B, T, W_A) * ln_g.astype(f32) + ln_b.astype(f32)
    bonus = jnp.sum(r32 * k32 * r_k.astype(f32).reshape(H_A, HS_A), axis=-1, keepdims=True) * v32
    o = (o + bonus.reshape(B, T, W_A)) * g
    return o.astype(p.dtype), p[:, -1], S_T


def gla_branch(seg, S0, a2, a_bias, norm_g):
    f32 = jnp.float32
    B, T, _ = seg.shape
    q = seg[..., :K_B]
    k = seg[..., K_B:2 * K_B]
    v = seg[..., 2 * K_B:2 * K_B + W_B]
    ac = seg[..., 2 * K_B + W_B:2 * K_B + W_B + R_GLA]
    rg = seg[..., 2 * K_B + W_B + R_GLA:]
    log_a = jax.nn.log_sigmoid((ac @ a2 + a_bias).astype(f32)) / GLA_TAU
    q = q.astype(f32).reshape(B, T, H_B, DK_B) * (DK_B ** -0.5)
    k = k.astype(f32).reshape(B, T, H_B, DK_B)
    v = v.astype(f32).reshape(B, T, H_B, DV_B)
    o, S_T = chunked_gated_linear(q, k, v, log_a.reshape(B, T, H_B, DK_B), S0.astype(f32))
    o = head_rms_norm(o, norm_g) * jax.nn.silu(rg.astype(f32)).reshape(B, T, H_B, DV_B)
    return o.reshape(B, T, W_B).astype(seg.dtype), S_T


def hgrn2_branch(seg, S0, lb, norm_g):
    f32 = jnp.float32
    B, T, _ = seg.shape
    q = jax.nn.silu(seg[..., :W_C].astype(f32))
    z = seg[..., W_C:2 * W_C].astype(f32)
    i = seg[..., 2 * W_C:3 * W_C].astype(f32)
    g = seg[..., 3 * W_C:].astype(f32)
    lb = lb.astype(f32)
    sig = jax.nn.sigmoid(z)
    f = lb + (1.0 - lb) * sig
    log_f = jnp.log(jnp.maximum(f, 1e-30))
    k = 1.0 - f
    hd = lambda t: t.reshape(B, T, H_C, HD_C)
    o, S_T = chunked_gated_linear(hd(q), hd(k), hd(i), hd(log_f), S0.astype(f32))
    o = head_rms_norm(o, norm_g) * jax.nn.sigmoid(hd(g))
    return o.reshape(B, T, W_C).astype(seg.dtype), S_T


def conv_ffn(h, buf, w_up, w_gate, conv_w, conv_b, w_down):
    T = h.shape[1]
    u = h @ w_up
    ext = jnp.concatenate([buf.astype(u.dtype), u], axis=1)
    c = conv_b
    for j in range(CONV_W):
        c = c + ext[:, j:j + T] * conv_w[j]
    y = (jax.nn.gelu(c, approximate=True) * (h @ w_gate)) @ w_down
    return y, ext[:, T:]


def trunk(x, shift_s, rwkv_s, gla_s, hgrn_s, conv_s, prm, lb_all, state_dtype):
    n_shift, n_rwkv, n_gla, n_hgrn, n_conv = [], [], [], [], []
    for l in range(DEPTH):
        lp = {name: arr[l] for name, arr in prm.items()}
        h = rms_norm(x, lp['norm_mix_pre'])
        proj = h @ lp['w_in']
        o_a, sh, s_a = rwkv7_branch(proj[..., OFF_A:OFF_B], shift_s[l], rwkv_s[l], lp['rwkv_mu'],
                                    lp['rwkv_w0'], lp['rwkv_w2'], lp['rwkv_a0'], lp['rwkv_a2'],
                                    lp['rwkv_g2'], lp['rwkv_k_k'], lp['rwkv_k_a'], lp['rwkv_r_k'],
                                    lp['rwkv_ln_g'], lp['rwkv_ln_b'])
        o_b, s_b = gla_branch(proj[..., OFF_B:OFF_C], gla_s[l], lp['gla_a2'], lp['gla_a_bias'],
                              lp['gla_norm_g'])
        o_c, s_c = hgrn2_branch(proj[..., OFF_C:OFF_G], hgrn_s[l], lb_all[l], lp['hgrn_norm_g'])
        gate = jax.nn.sigmoid(proj[..., OFF_G:])
        merged = (gate[..., :D_MODEL] * (o_a @ lp['p_rwkv'])
                  + gate[..., D_MODEL:2 * D_MODEL] * (o_b @ lp['p_gla'])
                  + gate[..., 2 * D_MODEL:] * (o_c @ lp['p_hgrn']))
        x = x + rms_norm(merged @ lp['w_out'], lp['norm_mix_post'])
        h = rms_norm(x, lp['norm_ffn_pre'])
        f, cb = conv_ffn(h, conv_s[l], lp['ffn_up'], lp['ffn_gate'], lp['ffn_conv_w'],
                         lp['ffn_conv_b'], lp['ffn_down'])
        x = x + rms_norm(f, lp['norm_ffn_post'])
        n_shift.append(sh)
        n_rwkv.append(s_a)
        n_gla.append(s_b)
        n_hgrn.append(s_c)
        n_conv.append(cb)
    st = lambda xs: jnp.stack(xs).astype(state_dtype)
    return x, st(n_shift), st(n_rwkv), st(n_gla), st(n_hgrn), st(n_conv)


def setup_inputs(seed: int = 0) -> dict:
    key = jax.random.key(seed)
    ks = iter(jax.random.split(key, 48))
    L = DEPTH

    def nrm(shape, scale):
        return scale * jax.random.normal(next(ks), shape, jnp.float32)

    def gain(shape):
        return 1.0 + nrm(shape, 0.02)

    return {
        'x_prompt': nrm((BATCH, SEQ, D_MODEL), 1.0),
        'x_sample': nrm((DEC_BATCH, DEC_SEQ, D_MODEL), 1.0),
        'state_rwkv_shift': nrm((L, DEC_BATCH, A_PROJ), 1.0),
        'state_rwkv': nrm((L, DEC_BATCH, H_A, HS_A, HS_A), 0.3),
        'state_gla': nrm((L, DEC_BATCH, H_B, DK_B, DV_B), 1.0),
        'state_hgrn': nrm((L, DEC_BATCH, H_C, HD_C, HD_C), 0.5),
        'state_ffn_conv': nrm((L, DEC_BATCH, CONV_W - 1, D_FF), 1.0),
        'norm_mix_pre': gain((L, D_MODEL)),
        'norm_mix_post': gain((L, D_MODEL)),
        'norm_ffn_pre': gain((L, D_MODEL)),
        'norm_ffn_post': gain((L, D_MODEL)),
        'w_in': nrm((L, D_MODEL, IN_COLS), D_MODEL ** -0.5),
        'rwkv_mu': jax.random.uniform(next(ks), (L, A_PROJ), jnp.float32),
        'rwkv_w0': nrm((L, W_A), 0.5),
        'rwkv_w2': nrm((L, R_W, W_A), R_W ** -0.5),
        'rwkv_a0': nrm((L, W_A), 0.1),
        'rwkv_a2': nrm((L, R_A, W_A), R_A ** -0.5),
        'rwkv_g2': nrm((L, R_G, W_A), R_G ** -0.5),
        'rwkv_k_k': 0.85 + nrm((L, W_A), 0.05),
        'rwkv_k_a': 1.0 + nrm((L, W_A), 0.05),
        'rwkv_r_k': nrm((L, W_A), 0.1),
        'rwkv_ln_g': gain((L, W_A)),
        'rwkv_ln_b': nrm((L, W_A), 0.02),
        'gla_a2': nrm((L, R_GLA, K_B), R_GLA ** -0.5),
        'gla_a_bias': nrm((L, K_B), 0.1),
        'gla_norm_g': gain((L, DV_B)),
        'hgrn_lb_logits': nrm((L, W_C), 0.1),
        'hgrn_norm_g': gain((L, HD_C)),
        'p_rwkv': nrm((L, W_A, D_MODEL), W_A ** -0.5),
        'p_gla': nrm((L, W_B, D_MODEL), W_B ** -0.5),
        'p_hgrn': nrm((L, W_C, D_MODEL), W_C ** -0.5),
        'w_out': nrm((L, D_MODEL, D_MODEL), D_MODEL ** -0.5),
        'ffn_up': nrm((L, D_MODEL, D_FF), D_MODEL ** -0.5),
        'ffn_gate': nrm((L, D_MODEL, D_FF), D_MODEL ** -0.5),
        'ffn_conv_w': nrm((L, CONV_W, D_FF), CONV_W ** -0.5),
        'ffn_conv_b': nrm((L, D_FF), 0.02),
        'ffn_down': nrm((L, D_FF, D_MODEL), D_FF ** -0.5),
    }


def reference(x_prompt, x_sample, state_rwkv_shift, state_rwkv, state_gla, state_hgrn, state_ffn_conv,
              norm_mix_pre, norm_mix_post, norm_ffn_pre, norm_ffn_post, w_in,
              rwkv_mu, rwkv_w0, rwkv_w2, rwkv_a0, rwkv_a2, rwkv_g2, rwkv_k_k, rwkv_k_a, rwkv_r_k,
              rwkv_ln_g, rwkv_ln_b, gla_a2, gla_a_bias, gla_norm_g, hgrn_lb_logits, hgrn_norm_g,
              p_rwkv, p_gla, p_hgrn, w_out, ffn_up, ffn_gate, ffn_conv_w, ffn_conv_b, ffn_down):
    prm = dict(norm_mix_pre=norm_mix_pre, norm_mix_post=norm_mix_post, norm_ffn_pre=norm_ffn_pre,
               norm_ffn_post=norm_ffn_post, w_in=w_in, rwkv_mu=rwkv_mu, rwkv_w0=rwkv_w0,
               rwkv_w2=rwkv_w2, rwkv_a0=rwkv_a0, rwkv_a2=rwkv_a2, rwkv_g2=rwkv_g2,
               rwkv_k_k=rwkv_k_k, rwkv_k_a=rwkv_k_a, rwkv_r_k=rwkv_r_k, rwkv_ln_g=rwkv_ln_g,
               rwkv_ln_b=rwkv_ln_b, gla_a2=gla_a2, gla_a_bias=gla_a_bias, gla_norm_g=gla_norm_g,
               hgrn_norm_g=hgrn_norm_g, p_rwkv=p_rwkv, p_gla=p_gla, p_hgrn=p_hgrn, w_out=w_out,
               ffn_up=ffn_up, ffn_gate=ffn_gate, ffn_conv_w=ffn_conv_w, ffn_conv_b=ffn_conv_b,
               ffn_down=ffn_down)
    probs = jax.nn.softmax(hgrn_lb_logits.astype(jnp.float32), axis=0)
    lb_all = jnp.cumsum(probs, axis=0) - probs[0]
    sdt = state_rwkv.dtype
    bp = x_prompt.shape[0]
    f32 = jnp.float32
    z_shift = jnp.zeros((DEPTH, bp, A_PROJ), x_prompt.dtype)
    z_rwkv = jnp.zeros((DEPTH, bp, H_A, HS_A, HS_A), f32)
    z_gla = jnp.zeros((DEPTH, bp, H_B, DK_B, DV_B), f32)
    z_hgrn = jnp.zeros((DEPTH, bp, H_C, HD_C, HD_C), f32)
    z_conv = jnp.zeros((DEPTH, bp, CONV_W - 1, D_FF), x_prompt.dtype)
    y_prompt, pr_shift, pr_rwkv, pr_gla, pr_hgrn, pr_conv = trunk(
        x_prompt, z_shift, z_rwkv, z_gla, z_hgrn, z_conv, prm, lb_all, sdt)
    y_sample, sa_shift, sa_rwkv, sa_gla, sa_hgrn, sa_conv = trunk(
        x_sample, state_rwkv_shift, state_rwkv, state_gla, state_hgrn, state_ffn_conv, prm, lb_all, sdt)
    return (y_prompt, y_sample, pr_shift, pr_rwkv, pr_gla, pr_hgrn, pr_conv,
            sa_shift, sa_rwkv, sa_gla, sa_hgrn, sa_conv)
```

```python
import functools

import jax
import jax.numpy as jnp
from jax import lax
from jax.experimental import pallas as pl
from jax.experimental.pallas import tpu as pltpu

F32 = jnp.float32
BF16 = jnp.bfloat16

D_MODEL = 4096
DEPTH = 2
HS_A = 64
W_A = 3 * D_MODEL // 8
H_A = W_A // HS_A
R_W, R_A, R_G = 128, 128, 256
A_PROJ = 3 * W_A + R_W + R_A + R_G
RWKV_LN_EPS = 64e-5
W_B = 5 * D_MODEL // 16
DV_B = 256
H_B = W_B // DV_B
DK_B = DV_B // 2
K_B = H_B * DK_B
R_GLA = 16
GLA_TAU = 16.0
W_C = 5 * D_MODEL // 16
HD_C = 128
H_C = W_C // HD_C
D_FF = 2 * D_MODEL
CONV_W = 3
RMS_EPS = 1e-6
HEAD_NORM_EPS = 1e-5
CHUNK = 32

LANES = 128
SUBLANES = 8
VMEM_LIMIT_BYTES = 56 * 1024 * 1024

COL_A = 0
COL_BQ = A_PROJ
COL_BK = COL_BQ + K_B
COL_BV = COL_BK + K_B
COL_BG = COL_BV + W_B
COL_C = COL_BG + W_B
COL_G = COL_C + 4 * W_C
COL_BA = COL_G + 3 * D_MODEL
PROJ_TN = 1024
PROJ_COLS = -(-(COL_BA + R_GLA) // PROJ_TN) * PROJ_TN


def _cparams(semantics):
    return pltpu.CompilerParams(dimension_semantics=semantics, vmem_limit_bytes=VMEM_LIMIT_BYTES)


def _rmsnorm_kernel(x_ref, g_ref, o_ref):
    x = x_ref[...]
    ms = jnp.mean(x * x, axis=-1, keepdims=True)
    o_ref[...] = (x * lax.rsqrt(ms + RMS_EPS) * g_ref[...]).astype(o_ref.dtype)


def rmsnorm(x, g, *, tm, out_dtype):
    m, d = x.shape
    return pl.pallas_call(
        _rmsnorm_kernel,
        grid=(m // tm,),
        in_specs=[pl.BlockSpec((tm, d), lambda i: (i, 0)), pl.BlockSpec((1, d), lambda i: (0, 0))],
        out_specs=pl.BlockSpec((tm, d), lambda i: (i, 0)),
        out_shape=jax.ShapeDtypeStruct((m, d), out_dtype),
        compiler_params=_cparams(("parallel",)),
        name="rmsnorm",
    )(x, g.reshape(1, d))


def _add_rmsnorm_kernel(x_ref, y_ref, g_ref, o_ref):
    y = y_ref[...]
    ms = jnp.mean(y * y, axis=-1, keepdims=True)
    o_ref[...] = x_ref[...] + y * lax.rsqrt(ms + RMS_EPS) * g_ref[...]


def add_rmsnorm(x, y, g, *, tm):
    m, d = x.shape
    return pl.pallas_call(
        _add_rmsnorm_kernel,
        grid=(m // tm,),
        in_specs=[pl.BlockSpec((tm, d), lambda i: (i, 0)), pl.BlockSpec((tm, d), lambda i: (i, 0)),
                  pl.BlockSpec((1, d), lambda i: (0, 0))],
        out_specs=pl.BlockSpec((tm, d), lambda i: (i, 0)),
        out_shape=jax.ShapeDtypeStruct((m, d), F32),
        compiler_params=_cparams(("parallel",)),
        name="add_rmsnorm",
    )(x, y, g.reshape(1, d))


def _matmul_kernel(x_ref, w_ref, o_ref):
    o_ref[...] = jnp.dot(x_ref[...], w_ref[...], preferred_element_type=F32).astype(o_ref.dtype)


def matmul(x, w, *, tm, tn, out_dtype=F32, name="matmul"):
    m, k = x.shape
    n = w.shape[1]
    return pl.pallas_call(
        _matmul_kernel,
        grid=(m // tm, n // tn),
        in_specs=[pl.BlockSpec((tm, k), lambda i, j: (i, 0)), pl.BlockSpec((k, tn), lambda i, j: (0, j))],
        out_specs=pl.BlockSpec((tm, tn), lambda i, j: (i, j)),
        out_shape=jax.ShapeDtypeStruct((m, n), out_dtype),
        compiler_params=_cparams(("parallel", "parallel")),
        name=name,
    )(x, w)


def _merge_kernel(y_ref, g_ref, ob_ref, oc_ref, pa_ref, pb_ref, pc_ref, ga_ref, gb_ref, gc_ref, o_ref):
    oa = (y_ref[...] * g_ref[...]).astype(BF16)
    ma = jnp.dot(oa, pa_ref[...], preferred_element_type=F32)
    mb = jnp.dot(ob_ref[...], pb_ref[...], preferred_element_type=F32)
    mc = jnp.dot(oc_ref[...], pc_ref[...], preferred_element_type=F32)
    merged = (jax.nn.sigmoid(ga_ref[...]) * ma + jax.nn.sigmoid(gb_ref[...]) * mb
              + jax.nn.sigmoid(gc_ref[...]) * mc)
    o_ref[...] = merged.astype(o_ref.dtype)


def merge(y_a, g_a, o_b, o_c, p_a, p_b, p_c, proj, *, tm, tn):
    m = y_a.shape[0]
    d = p_a.shape[1]
    gate0 = COL_G // tn
    gstep = d // tn

    def row(width):
        return pl.BlockSpec((tm, width), lambda i, j: (i, 0))

    def wcol(kdim):
        return pl.BlockSpec((kdim, tn), lambda i, j: (0, j))

    def gate(branch):
        return pl.BlockSpec((tm, tn), lambda i, j: (i, gate0 + branch * gstep + j))

    return pl.pallas_call(
        _merge_kernel,
        grid=(m // tm, d // tn),
        in_specs=[row(W_A), row(W_A), row(W_B), row(W_C), wcol(W_A), wcol(W_B), wcol(W_C),
                  gate(0), gate(1), gate(2)],
        out_specs=pl.BlockSpec((tm, tn), lambda i, j: (i, j)),
        out_shape=jax.ShapeDtypeStruct((m, d), BF16),
        compiler_params=_cparams(("parallel", "parallel")),
        name="merge",
    )(y_a, g_a, o_b, o_c, p_a, p_b, p_c, proj, proj, proj)


def _softplus(x):
    return jnp.maximum(x, 0.0) + jnp.log(1.0 + jnp.exp(-jnp.abs(x)))


def _rwkv_prep_kernel(p_ref, prev_ref, mu_ref, w0_ref, w2_ref, a0_ref, a2_ref, g2_ref,
                      r_ref, d_ref, k_ref, v_ref, a_ref, g_ref):
    p = p_ref[...]
    ps = p + mu_ref[...] * (prev_ref[...] - p)
    o1 = 3 * W_A
    r_ref[...] = ps[:, :W_A]
    k_ref[...] = ps[:, W_A:2 * W_A]
    v_ref[...] = ps[:, 2 * W_A:o1]
    wc = ps[:, o1:o1 + R_W]
    ac = ps[:, o1 + R_W:o1 + R_W + R_A]
    gc = ps[:, o1 + R_W + R_A:]
    wl = w0_ref[...] + jnp.dot(jnp.tanh(wc).astype(BF16), w2_ref[...], preferred_element_type=F32)
    w = -_softplus(-wl) - 0.5
    d_ref[...] = jnp.exp(-jnp.exp(w))
    al = a0_ref[...] + jnp.dot(ac.astype(BF16), a2_ref[...], preferred_element_type=F32)
    a_ref[...] = jax.nn.sigmoid(al)
    g_ref[...] = jnp.dot(jax.nn.sigmoid(gc).astype(BF16), g2_ref[...], preferred_element_type=F32)


def rwkv_prep(proj, prev, mu, w0, w2, a0, a2, g2, *, tm):
    m = proj.shape[0]
    full = lambda shape: pl.BlockSpec(shape, lambda i: (0, 0))
    out = jax.ShapeDtypeStruct((m, W_A), F32)
    return pl.pallas_call(
        _rwkv_prep_kernel,
        grid=(m // tm,),
        in_specs=[pl.BlockSpec((tm, A_PROJ), lambda i: (i, 0)), pl.BlockSpec((tm, A_PROJ), lambda i: (i, 0)),
                  full((1, A_PROJ)), full((1, W_A)), full((R_W, W_A)), full((1, W_A)), full((R_A, W_A)),
                  full((R_G, W_A))],
        out_specs=[pl.BlockSpec((tm, W_A), lambda i: (i, 0))] * 6,
        out_shape=[out] * 6,
        compiler_params=_cparams(("parallel",)),
        name="rwkv_prep",
    )(proj, prev, mu.reshape(1, A_PROJ), w0.reshape(1, W_A), w2.astype(BF16), a0.reshape(1, W_A),
      a2.astype(BF16), g2.astype(BF16))


def _rwkv_scan_kernel(r_ref, d_ref, k_ref, v_ref, a_ref, s0_ref, kk_ref, ka_ref, rk_ref, lng_ref, lnb_ref,
                      y_ref, st_ref, s_scr, o_scr, *, steps):
    tc = pl.program_id(1)

    @pl.when(tc == 0)
    def _():
        s_scr[...] = s0_ref[...]

    row = lax.broadcasted_iota(jnp.int32, (SUBLANES, 1), 0)

    def step(t, carry):
        r = r_ref[t]
        d = d_ref[t]
        k = k_ref[t]
        a = a_ref[t]
        kk = k * kk_ref[...]
        kk = kk * lax.rsqrt(jnp.maximum(jnp.sum(kk * kk, axis=0, keepdims=True), 1e-24))
        kp = k * (1.0 + (a - 1.0) * ka_ref[...])
        kka = kk * a

        def vblock(vo, c):
            v8 = v_ref[t, pl.ds(vo * SUBLANES, SUBLANES), :]
            o8 = jnp.zeros_like(v8)
            for vi in range(SUBLANES):
                s = s_scr[vo, vi]
                skk = jnp.sum(s * kk, axis=0, keepdims=True)
                s = s * d - skk * kka + v8[vi:vi + 1, :] * kp
                s_scr[vo, vi] = s
                o8 = jnp.where(row == vi, jnp.sum(s * r, axis=0, keepdims=True), o8)
            o_scr[pl.ds(vo * SUBLANES, SUBLANES), :] = o8
            return c

        lax.fori_loop(0, HS_A // SUBLANES, vblock, 0)
        o = o_scr[...]
        v = v_ref[t]
        mean = jnp.mean(o, axis=0, keepdims=True)
        var = jnp.mean(jnp.square(o - mean), axis=0, keepdims=True)
        y = (o - mean) * lax.rsqrt(var + RWKV_LN_EPS) * lng_ref[...] + lnb_ref[...]
        bonus = jnp.sum(r * kp * rk_ref[...], axis=0, keepdims=True) * v
        y_ref[t] = y + bonus
        return carry

    lax.fori_loop(0, steps, step, 0)

    @pl.when(tc == pl.num_programs(1) - 1)
    def _():
        st_ref[...] = s_scr[...]


def rwkv_scan(r, d, k, v, a, s0, kk_p, ka_p, rk_p, lng_p, lnb_p, *, steps):
    t, c, lanes = r.shape
    nl = lanes // LANES
    vo = c // SUBLANES
    s0 = s0.reshape(vo, SUBLANES, c, lanes)
    tok = pl.BlockSpec((steps, c, LANES), lambda l, i: (i, 0, l))
    par = pl.BlockSpec((c, LANES), lambda l, i: (0, l))
    st = pl.BlockSpec((vo, SUBLANES, c, LANES), lambda l, i: (0, 0, 0, l))
    y, s_t = pl.pallas_call(
        functools.partial(_rwkv_scan_kernel, steps=steps),
        grid=(nl, t // steps),
        in_specs=[tok] * 5 + [st] + [par] * 5,
        out_specs=[tok, st],
        out_shape=[jax.ShapeDtypeStruct((t, c, lanes), F32),
                   jax.ShapeDtypeStruct((vo, SUBLANES, c, lanes), F32)],
        scratch_shapes=[pltpu.VMEM((vo, SUBLANES, c, LANES), F32), pltpu.VMEM((c, LANES), F32)],
        compiler_params=_cparams(("parallel", "arbitrary")),
        name="rwkv_scan",
    )(r, d, k, v, a, s0, kk_p, ka_p, rk_p, lng_p, lnb_p)
    return y, s_t.reshape(c, c, lanes)


def _to_lanes(x, batch, seq):
    x = x.reshape(batch, seq, H_A, HS_A)
    return jnp.transpose(x, (1, 3, 0, 2)).reshape(seq, HS_A, batch * H_A)


def _from_lanes(y, batch, seq):
    y = y.reshape(seq, HS_A, batch, H_A)
    return jnp.transpose(y, (2, 0, 3, 1)).reshape(batch * seq, W_A)


def _pad_lanes(x):
    lanes = x.shape[-1]
    pad = -lanes % LANES
    if pad == 0:
        return x
    return jnp.pad(x, [(0, 0)] * (x.ndim - 1) + [(0, pad)])


def _head_param_tile(p, batch):
    return jnp.tile(p.reshape(H_A, HS_A).T, (1, batch))


def rwkv_recurrence(r, d, k, v, a, s0, params, batch, seq, *, steps):
    lanes = batch * H_A
    rl, dl, kl, vl, al = (_pad_lanes(_to_lanes(z, batch, seq)) for z in (r, d, k, v, a))
    s0l = _pad_lanes(jnp.transpose(s0.reshape(lanes, HS_A, HS_A), (1, 2, 0)))
    tiles = [_pad_lanes(_head_param_tile(p, batch)) for p in params]
    y, s_t = rwkv_scan(rl, dl, kl, vl, al, s0l, *tiles, steps=steps)
    y = _from_lanes(y[:, :, :lanes], batch, seq)
    s_t = jnp.transpose(s_t[:, :, :lanes], (2, 0, 1)).reshape(batch, H_A, HS_A, HS_A)
    return y, s_t


def _cumsum_rows(x):
    c = x.shape[0]
    row = lax.broadcasted_iota(jnp.int32, (c, 1), 0)
    shift = 1
    while shift < c:
        x = x + jnp.where(row >= shift, pltpu.roll(x, shift, 0), 0.0)
        shift *= 2
    return x


def _intra_chunk_scores(q, k, b):
    c = q.shape[0]
    nb = c // SUBLANES
    row = lax.broadcasted_iota(jnp.int32, (SUBLANES, 1), 0)
    lane = lax.broadcasted_iota(jnp.int32, (SUBLANES, c), 1)
    blocks = [jnp.zeros((SUBLANES, c), F32) for _ in range(nb)]
    for j in range(c):
        jb = j // SUBLANES
        bj = b[j:j + 1, :]
        kj = k[j:j + 1, :]
        for ib in range(jb, nb):
            lo = ib * SUBLANES
            diff = b[lo:lo + SUBLANES, :] - bj
            if ib == jb:
                causal = row >= (j - lo)
                decay = jnp.where(causal, jnp.exp(jnp.where(causal, diff, 0.0)), 0.0)
            else:
                decay = jnp.exp(diff)
            col = jnp.sum(q[lo:lo + SUBLANES, :] * kj * decay, axis=-1, keepdims=True)
            blocks[ib] = jnp.where(lane == j, col, blocks[ib])
    return jnp.concatenate(blocks, axis=0)


def _gated_chunk(q, k, v, log_a, st_ref, h):
    c = q.shape[0]
    b = _cumsum_rows(log_a)
    att = _intra_chunk_scores(q, k, b)
    v_bf = v.astype(BF16)
    o = jnp.dot(att.astype(BF16), v_bf, preferred_element_type=F32)
    st = st_ref[h]
    qe = (q * jnp.exp(b)).astype(BF16)
    o = o + lax.dot_general(qe, st.astype(BF16), (((1,), (1,)), ((), ())), preferred_element_type=F32)
    b_last = b[c - 1:c, :]
    kd = (k * jnp.exp(b_last - b)).astype(BF16)
    st_ref[h] = st * jnp.exp(b_last) + lax.dot_general(v_bf, kd, (((0,), (0,)), ((), ())),
                                                       preferred_element_type=F32)
    return o


def _head_rms(o, g):
    return o * lax.rsqrt(jnp.mean(o * o, axis=-1, keepdims=True) + HEAD_NORM_EPS) * g


def _load_state(s0_ref, st_scr, heads):
    for h in range(heads):
        st_scr[h] = s0_ref[0, h].T


def _store_state(st_scr, sT_ref, heads):
    for h in range(heads):
        sT_ref[0, h] = st_scr[h].T


def _mask_padding(log_decay, valid):
    if valid >= log_decay.shape[0]:
        return log_decay
    row = lax.broadcasted_iota(jnp.int32, (log_decay.shape[0], 1), 0)
    return jnp.where(row < valid, log_decay, 0.0)


def _gla_kernel(q_ref, k_ref, v_ref, rg_ref, ac_ref, a2_ref, ab_ref, ng_ref, s0_ref, o_ref, sT_ref, st_scr,
                *, valid):
    c = pl.program_id(1)

    @pl.when(c == 0)
    def _():
        _load_state(s0_ref, st_scr, H_B)

    gl = jnp.dot(ac_ref[...].astype(BF16), a2_ref[...], preferred_element_type=F32) + ab_ref[...]
    log_a = _mask_padding(-_softplus(-gl) / GLA_TAU, valid)
    q = q_ref[...] * (DK_B ** -0.5)
    k = k_ref[...]
    v = v_ref[...]
    rg = rg_ref[...]
    for h in range(H_B):
        ks = slice(h * DK_B, (h + 1) * DK_B)
        vs = slice(h * DV_B, (h + 1) * DV_B)
        o = _gated_chunk(q[:, ks], k[:, ks], v[:, vs], log_a[:, ks], st_scr, h)
        o_ref[:, vs] = (_head_rms(o, ng_ref[...]) * jax.nn.silu(rg[:, vs])).astype(o_ref.dtype)

    @pl.when(c == pl.num_programs(1) - 1)
    def _():
        _store_state(st_scr, sT_ref, H_B)


def _hgrn_kernel(q_ref, z_ref, i_ref, g_ref, lb_ref, ng_ref, s0_ref, o_ref, sT_ref, st_scr, *, valid):
    c = pl.program_id(1)

    @pl.when(c == 0)
    def _():
        _load_state(s0_ref, st_scr, H_C)

    q = jax.nn.silu(q_ref[...])
    lb = lb_ref[...]
    f = lb + (1.0 - lb) * jax.nn.sigmoid(z_ref[...])
    log_f = _mask_padding(jnp.log(jnp.maximum(f, 1e-30)), valid)
    k = 1.0 - f
    v = i_ref[...]
    g = g_ref[...]
    for h in range(H_C):
        hs = slice(h * HD_C, (h + 1) * HD_C)
        o = _gated_chunk(q[:, hs], k[:, hs], v[:, hs], log_f[:, hs], st_scr, h)
        o_ref[:, hs] = (_head_rms(o, ng_ref[...]) * jax.nn.sigmoid(g[:, hs])).astype(o_ref.dtype)

    @pl.when(c == pl.num_programs(1) - 1)
    def _():
        _store_state(st_scr, sT_ref, H_C)


def _seq_spec(chunk, width, col_block, nchunks):
    return pl.BlockSpec((chunk, width), lambda b, c: (b * nchunks + c, col_block))


def _const_spec(shape):
    return pl.BlockSpec(shape, lambda b, c: (0,) * len(shape))


def gla_branch(src, cols, a2, a_bias, norm_g, s0, batch, seq, *, chunk, valid=None):
    nch = seq // chunk
    valid = chunk if valid is None else valid
    cq, ck, cv, cg, ca = cols
    a2p = jnp.zeros((LANES, K_B), F32).at[:R_GLA].set(a2).astype(BF16)
    state = pl.BlockSpec((1, H_B, DK_B, DV_B), lambda b, c: (b, 0, 0, 0))
    return pl.pallas_call(
        functools.partial(_gla_kernel, valid=valid),
        grid=(batch, nch),
        in_specs=[_seq_spec(chunk, K_B, cq // K_B, nch), _seq_spec(chunk, K_B, ck // K_B, nch),
                  _seq_spec(chunk, W_B, cv // W_B, nch), _seq_spec(chunk, W_B, cg // W_B, nch),
                  _seq_spec(chunk, LANES, ca // LANES, nch),
                  _const_spec((LANES, K_B)), _const_spec((1, K_B)), _const_spec((1, DV_B)), state],
        out_specs=[pl.BlockSpec((chunk, W_B), lambda b, c: (b * nch + c, 0)), state],
        out_shape=[jax.ShapeDtypeStruct((batch * seq, W_B), BF16),
                   jax.ShapeDtypeStruct((batch, H_B, DK_B, DV_B), F32)],
        scratch_shapes=[pltpu.VMEM((H_B, DV_B, DK_B), F32)],
        compiler_params=_cparams(("parallel", "arbitrary")),
        name="gla",
    )(src, src, src, src, src, a2p, a_bias.reshape(1, K_B), norm_g.reshape(1, DV_B), s0)


def hgrn_branch(src, col0, lb, norm_g, s0, batch, seq, *, chunk, valid=None):
    nch = seq // chunk
    valid = chunk if valid is None else valid
    c0 = col0 // W_C
    state = pl.BlockSpec((1, H_C, HD_C, HD_C), lambda b, c: (b, 0, 0, 0))
    return pl.pallas_call(
        functools.partial(_hgrn_kernel, valid=valid),
        grid=(batch, nch),
        in_specs=[_seq_spec(chunk, W_C, c0 + j, nch) for j in range(4)]
        + [_const_spec((1, W_C)), _const_spec((1, HD_C)), state],
        out_specs=[pl.BlockSpec((chunk, W_C), lambda b, c: (b * nch + c, 0)), state],
        out_shape=[jax.ShapeDtypeStruct((batch * seq, W_C), BF16),
                   jax.ShapeDtypeStruct((batch, H_C, HD_C, HD_C), F32)],
        scratch_shapes=[pltpu.VMEM((H_C, HD_C, HD_C), F32)],
        compiler_params=_cparams(("parallel", "arbitrary")),
        name="hgrn",
    )(src, src, src, src, lb.reshape(1, W_C), norm_g.reshape(1, HD_C), s0)


def _ffn_act_seq_kernel(u_ref, prev_ref, gt_ref, st_ref, cw_ref, cb_ref, o_ref, *, tiles_per_seq):
    i = pl.program_id(0)
    u = u_ref[...]
    first = (i % tiles_per_seq) == 0
    pm1 = jnp.where(first, st_ref[0, 1:2, :], prev_ref[SUBLANES - 1:SUBLANES, :])
    pm2 = jnp.where(first, st_ref[0, 0:1, :], prev_ref[SUBLANES - 2:SUBLANES - 1, :])
    row = lax.broadcasted_iota(jnp.int32, (u.shape[0], 1), 0)
    u1 = jnp.where(row >= 1, pltpu.roll(u, 1, 0), pm1)
    u2 = jnp.where(row >= 2, pltpu.roll(u, 2, 0), jnp.where(row == 1, pm1, pm2))
    c = cb_ref[...] + u2 * cw_ref[0:1, :]
    c = c + u1 * cw_ref[1:2, :]
    c = c + u * cw_ref[2:3, :]
    o_ref[...] = (jax.nn.gelu(c, approximate=True) * gt_ref[...]).astype(o_ref.dtype)


def ffn_act_seq(ug, conv_state, conv_w, conv_b, batch, seq, *, tm, tn):
    tiles_per_seq = seq // tm
    nblk = tm // SUBLANES
    ncol = D_FF // tn
    return pl.pallas_call(
        functools.partial(_ffn_act_seq_kernel, tiles_per_seq=tiles_per_seq),
        grid=(batch * tiles_per_seq, ncol),
        in_specs=[pl.BlockSpec((tm, tn), lambda i, j: (i, j)),
                  pl.BlockSpec((SUBLANES, tn), lambda i, j: (jnp.maximum(i * nblk - 1, 0), j)),
                  pl.BlockSpec((tm, tn), lambda i, j: (i, ncol + j)),
                  pl.BlockSpec((1, CONV_W - 1, tn), lambda i, j: (i // tiles_per_seq, 0, j)),
                  pl.BlockSpec((CONV_W, tn), lambda i, j: (0, j)),
                  pl.BlockSpec((1, tn), lambda i, j: (0, j))],
        out_specs=pl.BlockSpec((tm, tn), lambda i, j: (i, j)),
        out_shape=jax.ShapeDtypeStruct((batch * seq, D_FF), BF16),
        compiler_params=_cparams(("parallel", "parallel")),
        name="ffn_act_seq",
    )(ug, ug, ug, conv_state, conv_w, conv_b.reshape(1, D_FF))


def _ffn_act_taps_kernel(e0_ref, e1_ref, e2_ref, gt_ref, cw_ref, cb_ref, o_ref):
    c = cb_ref[...] + e0_ref[...] * cw_ref[0:1, :]
    c = c + e1_ref[...] * cw_ref[1:2, :]
    c = c + e2_ref[...] * cw_ref[2:3, :]
    o_ref[...] = (jax.nn.gelu(c, approximate=True) * gt_ref[...]).astype(o_ref.dtype)


def ffn_act_taps(e0, e1, e2, gt, conv_w, conv_b, *, tn):
    m = e0.shape[0]
    blk = pl.BlockSpec((m, tn), lambda j: (0, j))
    return pl.pallas_call(
        _ffn_act_taps_kernel,
        grid=(D_FF // tn,),
        in_specs=[blk, blk, blk, blk, pl.BlockSpec((CONV_W, tn), lambda j: (0, j)),
                  pl.BlockSpec((1, tn), lambda j: (0, j))],
        out_specs=blk,
        out_shape=jax.ShapeDtypeStruct((m, D_FF), BF16),
        compiler_params=_cparams(("parallel",)),
        name="ffn_act_taps",
    )(e0, e1, e2, gt, conv_w, conv_b.reshape(1, D_FF))


ROW_TILE = 1088
HALF_ROW_TILE = ROW_TILE // 2
NORM_TILE = 272
PREP_TILE = 136
MERGE_TN = 256
FFN_TM = 512
FFN_TN = 2048
SCAN_STEPS = 16
DEC_PAD = SUBLANES


def _shifted_rows(x, first, seq):
    w = x.shape[-1]
    xs = x.reshape(-1, seq, w)
    return jnp.concatenate([first[:, None, :], xs[:, :-1]], axis=1).reshape(-1, w)


def kernel(x_prompt, x_sample, state_rwkv_shift, state_rwkv, state_gla, state_hgrn, state_ffn_conv, norm_mix_pre, norm_mix_post, norm_ffn_pre, norm_ffn_post, w_in, rwkv_mu, rwkv_w0, rwkv_w2, rwkv_a0, rwkv_a2, rwkv_g2, rwkv_k_k, rwkv_k_a, rwkv_r_k, rwkv_ln_g, rwkv_ln_b, gla_a2, gla_a_bias, gla_norm_g, hgrn_lb_logits, hgrn_norm_g, p_rwkv, p_gla, p_hgrn, w_out, ffn_up, ffn_gate, ffn_conv_w, ffn_conv_b, ffn_down):
    bp, tp, _ = x_prompt.shape
    bs, ts, _ = x_sample.shape
    mp, ms = bp * tp, bs * ts
    sdt = state_rwkv.dtype
    x = jnp.concatenate([x_prompt.reshape(mp, D_MODEL), x_sample.reshape(ms, D_MODEL)], axis=0)

    probs = jax.nn.softmax(hgrn_lb_logits.astype(F32), axis=0)
    lb_all = jnp.cumsum(probs, axis=0) - probs[0]

    z_rwkv = jnp.zeros((bp, H_A, HS_A, HS_A), F32)
    z_gla = jnp.zeros((bp, H_B, DK_B, DV_B), F32)
    z_hgrn = jnp.zeros((bp, H_C, HD_C, HD_C), F32)
    z_shift = jnp.zeros((bp, A_PROJ), F32)
    z_conv = jnp.zeros((bp, CONV_W - 1, D_FF), F32)

    outs = {name: [] for name in ("p_shift", "p_rwkv", "p_gla", "p_hgrn", "p_conv",
                                   "s_shift", "s_rwkv", "s_gla", "s_hgrn", "s_conv")}
    for l in range(DEPTH):
        wl = w_in[l]
        w_cat = jnp.concatenate(
            [wl[:, :COL_BG], wl[:, COL_BG + R_GLA:], wl[:, COL_BG:COL_BG + R_GLA],
             jnp.zeros((D_MODEL, PROJ_COLS - COL_BA - R_GLA), wl.dtype)], axis=1).astype(BF16)
        h = rmsnorm(x, norm_mix_pre[l], tm=NORM_TILE, out_dtype=BF16)
        proj = matmul(h, w_cat, tm=ROW_TILE, tn=PROJ_TN, name="proj_in")

        prev = jnp.concatenate([_shifted_rows(proj[:mp, :A_PROJ], z_shift, tp),
                                _shifted_rows(proj[mp:, :A_PROJ], state_rwkv_shift[l].astype(F32), ts)], axis=0)
        r, d, k, v, a, g = rwkv_prep(proj, prev, rwkv_mu[l], rwkv_w0[l], rwkv_w2[l], rwkv_a0[l], rwkv_a2[l],
                                     rwkv_g2[l], tm=PREP_TILE)
        params = (rwkv_k_k[l], rwkv_k_a[l], rwkv_r_k[l], rwkv_ln_g[l], rwkv_ln_b[l])
        y_p, sa_p = rwkv_recurrence(r[:mp], d[:mp], k[:mp], v[:mp], a[:mp], z_rwkv, params, bp, tp,
                                    steps=SCAN_STEPS)
        y_s, sa_s = rwkv_recurrence(r[mp:], d[mp:], k[mp:], v[mp:], a[mp:], state_rwkv[l].astype(F32), params,
                                    bs, ts, steps=ts)
        y_a = jnp.concatenate([y_p, y_s], axis=0)

        dec = jnp.concatenate([proj[mp:, COL_BQ:COL_G], proj[mp:, COL_BA:COL_BA + LANES]], axis=1)
        dec = jnp.pad(dec.reshape(bs, ts, -1), ((0, 0), (0, DEC_PAD - ts), (0, 0))).reshape(bs * DEC_PAD, -1)
        dec_cols = tuple(cc - COL_BQ for cc in (COL_BQ, COL_BK, COL_BV, COL_BG)) + (COL_G - COL_BQ,)

        ob_p, sb_p = gla_branch(proj, (COL_BQ, COL_BK, COL_BV, COL_BG, COL_BA), gla_a2[l], gla_a_bias[l],
                                gla_norm_g[l], z_gla, bp, tp, chunk=CHUNK)
        ob_s, sb_s = gla_branch(dec, dec_cols, gla_a2[l], gla_a_bias[l], gla_norm_g[l],
                                state_gla[l].astype(F32), bs, DEC_PAD, chunk=DEC_PAD, valid=ts)
        oc_p, sc_p = hgrn_branch(proj, COL_C, lb_all[l], hgrn_norm_g[l], z_hgrn, bp, tp, chunk=CHUNK)
        oc_s, sc_s = hgrn_branch(dec, COL_C - COL_BQ, lb_all[l], hgrn_norm_g[l], state_hgrn[l].astype(F32),
                                 bs, DEC_PAD, chunk=DEC_PAD, valid=ts)
        unpad = lambda o: o.reshape(bs, DEC_PAD, -1)[:, :ts].reshape(ms, -1)
        o_b = jnp.concatenate([ob_p, unpad(ob_s)], axis=0)
        o_c = jnp.concatenate([oc_p, unpad(oc_s)], axis=0)

        merged = merge(y_a, g, o_b, o_c, p_rwkv[l].astype(BF16), p_gla[l].astype(BF16), p_hgrn[l].astype(BF16),
                       proj, tm=HALF_ROW_TILE, tn=MERGE_TN)
        mix = matmul(merged, w_out[l].astype(BF16), tm=ROW_TILE, tn=1024, name="w_out")
        x = add_rmsnorm(x, mix, norm_mix_post[l], tm=NORM_TILE)

        h = rmsnorm(x, norm_ffn_pre[l], tm=NORM_TILE, out_dtype=BF16)
        w_ug = jnp.concatenate([ffn_up[l], ffn_gate[l]], axis=1).astype(BF16)
        ug = matmul(h, w_ug, tm=ROW_TILE, tn=1024, name="ffn_up_gate")
        act_p = ffn_act_seq(ug, z_conv, ffn_conv_w[l], ffn_conv_b[l], bp, tp, tm=FFN_TM, tn=FFN_TN)
        u_s = ug[mp:, :D_FF].reshape(bs, ts, D_FF)
        ext = jnp.concatenate([state_ffn_conv[l].astype(F32), u_s], axis=1)
        taps = [ext[:, j:j + ts].reshape(ms, D_FF) for j in range(CONV_W)]
        act_s = ffn_act_taps(*taps, ug[mp:, D_FF:], ffn_conv_w[l], ffn_conv_b[l], tn=FFN_TN)
        act = jnp.concatenate([act_p, act_s], axis=0)
        f = matmul(act, ffn_down[l].astype(BF16), tm=HALF_ROW_TILE, tn=512, name="ffn_down")
        x = add_rmsnorm(x, f, norm_ffn_post[l], tm=NORM_TILE)

        outs["p_shift"].append(lax.slice(proj, (tp - 1, 0), (mp, A_PROJ), (tp, 1)))
        outs["s_shift"].append(lax.slice(proj, (mp + ts - 1, 0), (mp + ms, A_PROJ), (ts, 1)))
        outs["p_rwkv"].append(sa_p)
        outs["s_rwkv"].append(sa_s)
        outs["p_gla"].append(sb_p)
        outs["s_gla"].append(sb_s)
        outs["p_hgrn"].append(sc_p)
        outs["s_hgrn"].append(sc_s)
        outs["p_conv"].append(jnp.stack([lax.slice(ug, (tp - 2 + j, 0), (mp, D_FF), (tp, 1))
                                         for j in range(CONV_W - 1)], axis=1))
        outs["s_conv"].append(ext[:, ts:])

    st = lambda name: jnp.stack(outs[name]).astype(sdt)
    return (x[:mp].reshape(bp, tp, D_MODEL), x[mp:].reshape(bs, ts, D_MODEL),
            st("p_shift"), st("p_rwkv"), st("p_gla"), st("p_hgrn"), st("p_conv"),
            st("s_shift"), st("s_rwkv"), st("s_gla"), st("s_hgrn"), st("s_conv"))
```

```python
import functools

import jax
import jax.numpy as jnp
from jax import lax
from jax.experimental import pallas as pl
from jax.experimental.pallas import tpu as pltpu

F32 = jnp.float32
BF16 = jnp.bfloat16

D_MODEL = 4096
DEPTH = 2
HS_A = 64
W_A = 3 * D_MODEL // 8
H_A = W_A // HS_A
R_W, R_A, R_G = 128, 128, 256
A_PROJ = 3 * W_A + R_W + R_A + R_G
RWKV_LN_EPS = 64e-5
W_B = 5 * D_MODEL // 16
DV_B = 256
H_B = W_B // DV_B
DK_B = DV_B // 2
K_B = H_B * DK_B
R_GLA = 16
GLA_TAU = 16.0
W_C = 5 * D_MODEL // 16
HD_C = 128
H_C = W_C // HD_C
D_FF = 2 * D_MODEL
CONV_W = 3
RMS_EPS = 1e-6
HEAD_NORM_EPS = 1e-5

LANES = 128
SUBLANES = 8
VMEM_LIMIT_BYTES = 56 * 1024 * 1024

IN_COLS = A_PROJ + 2 * K_B + 2 * W_B + R_GLA + 4 * W_C + 3 * D_MODEL
COL_BQ = A_PROJ
COL_BK = COL_BQ + K_B
COL_BV = COL_BK + K_B
COL_BA = COL_BV + W_B
TAIL_SRC = COL_BA + R_GLA
PROJ_TN = 512
HEAD_TILES = -(-TAIL_SRC // PROJ_TN)
TAIL_DST = 10240
TAIL_TILES = -(-(IN_COLS - TAIL_SRC) // PROJ_TN)
PROJ_COLS = TAIL_DST + TAIL_TILES * PROJ_TN
COL_BG = TAIL_DST
COL_C = COL_BG + W_B
COL_G = COL_C + 4 * W_C
assert (HEAD_TILES * PROJ_TN - TAIL_SRC) % LANES == LANES - R_GLA and TAIL_DST % PROJ_TN == 0
assert COL_BG % W_B == 0 and COL_C % W_C == 0 and COL_BA % LANES == 0


def _cparams(semantics):
    return pltpu.CompilerParams(dimension_semantics=semantics, vmem_limit_bytes=VMEM_LIMIT_BYTES)


def _rmsnorm_kernel(x_ref, g_ref, o_ref):
    x = x_ref[...]
    ms = jnp.mean(x * x, axis=-1, keepdims=True)
    o_ref[...] = (x * lax.rsqrt(ms + RMS_EPS) * g_ref[...]).astype(o_ref.dtype)


def rmsnorm(x, g, *, tm, out_dtype):
    m, d = x.shape
    return pl.pallas_call(
        _rmsnorm_kernel,
        grid=(m // tm,),
        in_specs=[pl.BlockSpec((tm, d), lambda i: (i, 0)), pl.BlockSpec((1, d), lambda i: (0, 0))],
        out_specs=pl.BlockSpec((tm, d), lambda i: (i, 0)),
        out_shape=jax.ShapeDtypeStruct((m, d), out_dtype),
        compiler_params=_cparams(("parallel",)),
        name="rmsnorm",
    )(x, g.reshape(1, d))


def _add_rmsnorm_kernel(x_ref, y_ref, g_ref, o_ref):
    y = y_ref[...]
    ms = jnp.mean(y * y, axis=-1, keepdims=True)
    o_ref[...] = x_ref[...] + y * lax.rsqrt(ms + RMS_EPS) * g_ref[...]


def add_rmsnorm(x, y, g, *, tm):
    m, d = x.shape
    return pl.pallas_call(
        _add_rmsnorm_kernel,
        grid=(m // tm,),
        in_specs=[pl.BlockSpec((tm, d), lambda i: (i, 0)), pl.BlockSpec((tm, d), lambda i: (i, 0)),
                  pl.BlockSpec((1, d), lambda i: (0, 0))],
        out_specs=pl.BlockSpec((tm, d), lambda i: (i, 0)),
        out_shape=jax.ShapeDtypeStruct((m, d), F32),
        compiler_params=_cparams(("parallel",)),
        name="add_rmsnorm",
    )(x, y, g.reshape(1, d))


def _matmul_kernel(x_ref, w_ref, o_ref):
    o_ref[...] = jnp.dot(x_ref[...], w_ref[...], preferred_element_type=F32).astype(o_ref.dtype)


def matmul(x, w, *, tm, tn, out_dtype=F32, name="matmul"):
    m, k = x.shape
    n = w.shape[1]
    return pl.pallas_call(
        _matmul_kernel,
        grid=(m // tm, n // tn),
        in_specs=[pl.BlockSpec((tm, k), lambda i, j: (i, 0)), pl.BlockSpec((k, tn), lambda i, j: (0, j))],
        out_specs=pl.BlockSpec((tm, tn), lambda i, j: (i, j)),
        out_shape=jax.ShapeDtypeStruct((m, n), out_dtype),
        compiler_params=_cparams(("parallel", "parallel")),
        name=name,
    )(x, w)


def _pair_dots(i, prompt_tiles, xp_ref, xs_ref, w_bf, op_ref, os_ref):
    @pl.when(i < prompt_tiles)
    def _():
        op_ref[...] = jnp.dot(xp_ref[...], w_bf[...], preferred_element_type=F32)

    @pl.when(i == prompt_tiles)
    def _():
        os_ref[...] = jnp.dot(xs_ref[...], w_bf[...], preferred_element_type=F32)


def _proj_pair_kernel(xp_ref, xs_ref, wm_ref, we_ref, op_ref, os_ref, w_bf, *, prompt_tiles):
    j = pl.program_id(0)
    i = pl.program_id(1)

    @pl.when((i == 0) & (j < HEAD_TILES))
    def _():
        w_bf[...] = wm_ref[...].astype(BF16)

    @pl.when((i == 0) & (j >= HEAD_TILES))
    def _():
        lane = lax.broadcasted_iota(jnp.int32, (1, LANES), 1)
        nblk = PROJ_TN // LANES
        rows = 512
        for r0 in range(0, wm_ref.shape[0], rows):
            rs = slice(r0, r0 + rows)
            rolled = [pltpu.roll(wm_ref[rs, b * LANES:(b + 1) * LANES], LANES - R_GLA, 1) for b in range(nblk)]
            rolled.append(pltpu.roll(we_ref[rs, :], LANES - R_GLA, 1))
            for b in range(nblk):
                w_bf[rs, b * LANES:(b + 1) * LANES] = jnp.where(lane < LANES - R_GLA, rolled[b],
                                                                rolled[b + 1]).astype(BF16)

    _pair_dots(i, prompt_tiles, xp_ref, xs_ref, w_bf, op_ref, os_ref)


def proj_pair(xp, xs, w_in, layer, *, tm):
    mp, k = xp.shape
    ms = xs.shape[0]
    pt = mp // tm
    last_lane_block = (IN_COLS - 1) // LANES
    skip = TAIL_DST // PROJ_TN - HEAD_TILES
    out_col = lambda j: jnp.where(j < HEAD_TILES, j, j + skip)
    prow = lambda i: jnp.minimum(i, pt - 1)
    return pl.pallas_call(
        functools.partial(_proj_pair_kernel, prompt_tiles=pt),
        grid=(HEAD_TILES + TAIL_TILES, pt + 1),
        in_specs=[pl.BlockSpec((tm, k), lambda j, i: (prow(i), 0)),
                  pl.BlockSpec((ms, k), lambda j, i: (0, 0), pipeline_mode=pl.Buffered(1)),
                  pl.BlockSpec((None, k, PROJ_TN), lambda j, i: (layer, 0, jnp.where(j < HEAD_TILES, j, j - 1))),
                  pl.BlockSpec((None, k, LANES),
                               lambda j, i: (layer, 0, jnp.minimum(j * (PROJ_TN // LANES), last_lane_block)),
                               pipeline_mode=pl.Buffered(1))],
        out_specs=[pl.BlockSpec((tm, PROJ_TN), lambda j, i: (prow(i), out_col(j))),
                   pl.BlockSpec((ms, PROJ_TN), lambda j, i: (0, out_col(j)))],
        out_shape=[jax.ShapeDtypeStruct((mp, PROJ_COLS), F32), jax.ShapeDtypeStruct((ms, PROJ_COLS), F32)],
        scratch_shapes=[pltpu.VMEM((k, PROJ_TN), BF16)],
        compiler_params=_cparams(("arbitrary", "arbitrary")),
        name="proj_pair",
    )(xp, xs, w_in, w_in)


def _merge_kernel(y_ref, g_ref, ob_ref, oc_ref, pa_ref, pb_ref, pc_ref, ga_ref, gb_ref, gc_ref, o_ref):
    oa = (y_ref[...] * g_ref[...]).astype(BF16)
    ma = jnp.dot(oa, pa_ref[...], preferred_element_type=F32)
    mb = jnp.dot(ob_ref[...], pb_ref[...], preferred_element_type=F32)
    mc = jnp.dot(oc_ref[...], pc_ref[...], preferred_element_type=F32)
    merged = (jax.nn.sigmoid(ga_ref[...]) * ma + jax.nn.sigmoid(gb_ref[...]) * mb
              + jax.nn.sigmoid(gc_ref[...]) * mc)
    o_ref[...] = merged.astype(o_ref.dtype)


def merge(y_a, g_a, o_b, o_c, p_a, p_b, p_c, proj, *, tm, tn):
    m = y_a.shape[0]
    d = p_a.shape[1]
    gate0 = COL_G // tn
    gstep = d // tn

    def row(width):
        return pl.BlockSpec((tm, width), lambda i, j: (i, 0))

    def wcol(kdim):
        return pl.BlockSpec((kdim, tn), lambda i, j: (0, j))

    def gate(branch):
        return pl.BlockSpec((tm, tn), lambda i, j: (i, gate0 + branch * gstep + j))

    return pl.pallas_call(
        _merge_kernel,
        grid=(m // tm, d // tn),
        in_specs=[row(W_A), row(W_A), row(W_B), row(W_C), wcol(W_A), wcol(W_B), wcol(W_C),
                  gate(0), gate(1), gate(2)],
        out_specs=pl.BlockSpec((tm, tn), lambda i, j: (i, j)),
        out_shape=jax.ShapeDtypeStruct((m, d), BF16),
        compiler_params=_cparams(("parallel", "parallel")),
        name="merge",
    )(y_a, g_a, o_b, o_c, p_a, p_b, p_c, proj, proj, proj)


def _softplus(x):
    return jnp.maximum(x, 0.0) + jnp.log(1.0 + jnp.exp(-jnp.abs(x)))


def _rwkv_gates(p, prev, mu_ref, w0_ref, w2_ref, a0_ref, a2_ref, g2_ref, r_ref, d_ref, k_ref, v_ref, a_ref, g_ref):
    ps = p + mu_ref[...] * (prev - p)
    o1 = 3 * W_A
    r_ref[...] = ps[:, :W_A]
    k_ref[...] = ps[:, W_A:2 * W_A]
    v_ref[...] = ps[:, 2 * W_A:o1]
    wc = ps[:, o1:o1 + R_W]
    ac = ps[:, o1 + R_W:o1 + R_W + R_A]
    gc = ps[:, o1 + R_W + R_A:]
    wl = w0_ref[...] + jnp.dot(jnp.tanh(wc).astype(BF16), w2_ref[...], preferred_element_type=F32)
    w = -_softplus(-wl) - 0.5
    d_ref[...] = jnp.exp(-jnp.exp(w))
    al = a0_ref[...] + jnp.dot(ac.astype(BF16), a2_ref[...], preferred_element_type=F32)
    a_ref[...] = jax.nn.sigmoid(al)
    g_ref[...] = jnp.dot(jax.nn.sigmoid(gc).astype(BF16), g2_ref[...], preferred_element_type=F32)


def _rwkv_prep_seq_kernel(p_ref, tail_ref, shift_ref, *rest, tiles_per_seq):
    i = pl.program_id(0)
    p = p_ref[...]
    first = (i % tiles_per_seq) == 0
    before = jnp.where(first, shift_ref[0], tail_ref[SUBLANES - 1:SUBLANES, :])
    row = lax.broadcasted_iota(jnp.int32, (p.shape[0], 1), 0)
    prev = jnp.where(row >= 1, pltpu.roll(p, 1, 0), before)
    _rwkv_gates(p, prev, *rest)


def _rwkv_prep_rows_kernel(p_ref, prev_ref, *rest):
    _rwkv_gates(p_ref[...], prev_ref[...], *rest)


def _rwkv_prep_call(kernel, lead_specs, lead_args, m, tm, mu, w0, w2, a0, a2, g2, name):
    full = lambda shape: pl.BlockSpec(shape, lambda i: (0, 0))
    out = jax.ShapeDtypeStruct((m, W_A), F32)
    return pl.pallas_call(
        kernel,
        grid=(m // tm,),
        in_specs=lead_specs + [full((1, A_PROJ)), full((1, W_A)), full((R_W, W_A)), full((1, W_A)),
                               full((R_A, W_A)), full((R_G, W_A))],
        out_specs=[pl.BlockSpec((tm, W_A), lambda i: (i, 0))] * 6,
        out_shape=[out] * 6,
        compiler_params=_cparams(("parallel",)),
        name=name,
    )(*lead_args, mu.reshape(1, A_PROJ), w0.reshape(1, W_A), w2.astype(BF16), a0.reshape(1, W_A),
      a2.astype(BF16), g2.astype(BF16))


def rwkv_prep_seq(proj, shift, weights, batch, seq, *, tm):
    tiles_per_seq = seq // tm
    nblk = tm // SUBLANES
    specs = [pl.BlockSpec((tm, A_PROJ), lambda i: (i, 0)),
             pl.BlockSpec((SUBLANES, A_PROJ), lambda i: (jnp.maximum(i * nblk - 1, 0), 0)),
             pl.BlockSpec((1, 1, A_PROJ), lambda i: (i // tiles_per_seq, 0, 0))]
    kern = functools.partial(_rwkv_prep_seq_kernel, tiles_per_seq=tiles_per_seq)
    return _rwkv_prep_call(kern, specs, (proj, proj, shift.reshape(batch, 1, A_PROJ)), batch * seq, tm,
                           *weights, name="rwkv_prep_seq")


def rwkv_prep_rows(proj, prev, weights, *, tm):
    specs = [pl.BlockSpec((tm, A_PROJ), lambda i: (i, 0)), pl.BlockSpec((tm, A_PROJ), lambda i: (i, 0))]
    return _rwkv_prep_call(_rwkv_prep_rows_kernel, specs, (proj, prev), proj.shape[0], tm, *weights,
                           name="rwkv_prep_rows")


def _rwkv_scan_kernel(r_ref, d_ref, k_ref, v_ref, a_ref, s0_ref, kk_ref, ka_ref, rk_ref, lng_ref, lnb_ref,
                      y_ref, st_ref, s_scr, o_scr, *, steps):
    tc = pl.program_id(1)

    @pl.when(tc == 0)
    def _():
        s_scr[...] = s0_ref[...]

    row = lax.broadcasted_iota(jnp.int32, (SUBLANES, 1), 0)

    def step(t, carry):
        r = r_ref[t]
        d = d_ref[t]
        k = k_ref[t]
        a = a_ref[t]
        kk = k * kk_ref[...]
        kk = kk * lax.rsqrt(jnp.maximum(jnp.sum(kk * kk, axis=0, keepdims=True), 1e-24))
        kp = k * (1.0 + (a - 1.0) * ka_ref[...])
        kka = kk * a

        def vblock(vo, c):
            v8 = v_ref[t, pl.ds(vo * SUBLANES, SUBLANES), :]
            o8 = jnp.zeros_like(v8)
            for vi in range(SUBLANES):
                s = s_scr[vo, vi]
                skk = jnp.sum(s * kk, axis=0, keepdims=True)
                s = s * d - skk * kka + v8[vi:vi + 1, :] * kp
                s_scr[vo, vi] = s
                o8 = jnp.where(row == vi, jnp.sum(s * r, axis=0, keepdims=True), o8)
            o_scr[pl.ds(vo * SUBLANES, SUBLANES), :] = o8
            return c

        lax.fori_loop(0, HS_A // SUBLANES, vblock, 0)
        o = o_scr[...]
        v = v_ref[t]
        mean = jnp.mean(o, axis=0, keepdims=True)
        var = jnp.mean(jnp.square(o - mean), axis=0, keepdims=True)
        y = (o - mean) * lax.rsqrt(var + RWKV_LN_EPS) * lng_ref[...] + lnb_ref[...]
        bonus = jnp.sum(r * kp * rk_ref[...], axis=0, keepdims=True) * v
        y_ref[t] = y + bonus
        return carry

    lax.fori_loop(0, steps, step, 0)

    @pl.when(tc == pl.num_programs(1) - 1)
    def _():
        st_ref[...] = s_scr[...]


def rwkv_scan(r, d, k, v, a, s0, kk_p, ka_p, rk_p, lng_p, lnb_p, *, steps):
    t, c, lanes = r.shape
    lb = LANES if lanes % LANES == 0 else lanes
    nl = lanes // lb
    vo = c // SUBLANES
    s0 = s0.reshape(vo, SUBLANES, c, lanes)
    tok = pl.BlockSpec((steps, c, lb), lambda l, i: (i, 0, l))
    par = pl.BlockSpec((c, lb), lambda l, i: (0, l))
    st = pl.BlockSpec((vo, SUBLANES, c, lb), lambda l, i: (0, 0, 0, l))
    y, s_t = pl.pallas_call(
        functools.partial(_rwkv_scan_kernel, steps=steps),
        grid=(nl, t // steps),
        in_specs=[tok] * 5 + [st] + [par] * 5,
        out_specs=[tok, st],
        out_shape=[jax.ShapeDtypeStruct((t, c, lanes), F32),
                   jax.ShapeDtypeStruct((vo, SUBLANES, c, lanes), F32)],
        scratch_shapes=[pltpu.VMEM((vo, SUBLANES, c, lb), F32), pltpu.VMEM((c, lb), F32)],
        compiler_params=_cparams(("parallel", "arbitrary")),
        name="rwkv_scan",
    )(r, d, k, v, a, s0, kk_p, ka_p, rk_p, lng_p, lnb_p)
    return y, s_t.reshape(c, c, lanes)


def _to_lanes(x, batch, seq):
    x = x.reshape(batch, seq, H_A, HS_A)
    return jnp.transpose(x, (1, 3, 0, 2)).reshape(seq, HS_A, batch * H_A)


def _from_lanes(y, batch, seq):
    y = y.reshape(seq, HS_A, batch, H_A)
    return jnp.transpose(y, (2, 0, 3, 1)).reshape(batch * seq, W_A)


def _pad_lanes(x):
    pad = -x.shape[-1] % LANES
    if pad == 0:
        return x
    return jnp.pad(x, [(0, 0)] * (x.ndim - 1) + [(0, pad)])


def _head_param_tile(p, batch):
    return jnp.tile(p.reshape(H_A, HS_A).T, (1, batch))


def rwkv_recurrence(r, d, k, v, a, s0, params, batch, seq, *, steps):
    lanes = batch * H_A
    rl, dl, kl, vl, al = (_pad_lanes(_to_lanes(z, batch, seq)) for z in (r, d, k, v, a))
    s0l = _pad_lanes(jnp.transpose(s0.reshape(lanes, HS_A, HS_A), (1, 2, 0)))
    tiles = [_pad_lanes(_head_param_tile(p, batch)) for p in params]
    y, s_t = rwkv_scan(rl, dl, kl, vl, al, s0l, *tiles, steps=steps)
    y = _from_lanes(y[:, :, :lanes], batch, seq)
    s_t = jnp.transpose(s_t[:, :, :lanes], (2, 0, 1)).reshape(batch, H_A, HS_A, HS_A)
    return y, s_t


def _block_prefix_sum(x):
    row = lax.broadcasted_iota(jnp.int32, (SUBLANES, 1), 0)
    for shift in (1, 2, 4):
        x = x + jnp.where(row >= shift, pltpu.roll(x, shift, 0), 0.0)
    return x


def _chunk_scores(q, k, blocks):
    c = q.shape[0]
    nb = c // SUBLANES
    row = lax.broadcasted_iota(jnp.int32, (SUBLANES, 1), 0)
    lane = lax.broadcasted_iota(jnp.int32, (SUBLANES, c), 1)
    rows = []
    for ib in range(nb):
        lo = ib * SUBLANES
        qi = q[lo:lo + SUBLANES, :]
        ki = k[lo:lo + SUBLANES, :]
        bi = blocks[ib]
        att = jnp.zeros((SUBLANES, c), F32)
        for j in range(SUBLANES):
            decay = jnp.exp(jnp.minimum(bi - bi[j:j + 1, :], 0.0))
            col = jnp.sum(qi * ki[j:j + 1, :] * decay, axis=-1, keepdims=True)
            att = jnp.where(lane == lo + j, col, att)
        att = jnp.where(lane <= lo + row, att, 0.0)
        if ib > 0:
            ref = blocks[ib - 1][SUBLANES - 1:SUBLANES, :]
            qs = (qi * jnp.exp(bi - ref)).astype(BF16)
            ks = [k[jb * SUBLANES:(jb + 1) * SUBLANES, :] * jnp.exp(ref - blocks[jb]) for jb in range(ib)]
            ks.append(jnp.zeros((c - lo, q.shape[1]), F32))
            ks = jnp.concatenate(ks, axis=0).astype(BF16)
            att = att + lax.dot_general(qs, ks, (((1,), (1,)), ((), ())), preferred_element_type=F32)
        rows.append(att)
    return jnp.concatenate(rows, axis=0)


def _gated_chunk(q, k, v, log_a, s_prev):
    nb = q.shape[0] // SUBLANES
    blocks = []
    carry = None
    for ib in range(nb):
        bi = _block_prefix_sum(log_a[ib * SUBLANES:(ib + 1) * SUBLANES, :])
        if carry is not None:
            bi = bi + carry
        carry = bi[SUBLANES - 1:SUBLANES, :]
        blocks.append(bi)
    b = jnp.concatenate(blocks, axis=0)
    b_last = carry
    att = _chunk_scores(q, k, blocks)
    v_bf = v.astype(BF16)
    o = jnp.dot(att.astype(BF16), v_bf, preferred_element_type=F32)
    qe = (q * jnp.exp(b)).astype(BF16)
    o = o + jnp.dot(qe, s_prev.astype(BF16), preferred_element_type=F32)
    kd = (k * jnp.exp(b_last - b)).astype(BF16)
    e_col = jnp.transpose(jnp.broadcast_to(jnp.exp(b_last), (SUBLANES, q.shape[1])))[:, 0:1]
    s_new = s_prev * e_col + lax.dot_general(kd, v_bf, (((0,), (0,)), ((), ())), preferred_element_type=F32)
    return o, s_new


def _head_rms(o, g):
    return o * lax.rsqrt(jnp.mean(o * o, axis=-1, keepdims=True) + HEAD_NORM_EPS) * g


def _mask_padding(log_decay, valid):
    if valid >= log_decay.shape[0]:
        return log_decay
    row = lax.broadcasted_iota(jnp.int32, (log_decay.shape[0], 1), 0)
    return jnp.where(row < valid, log_decay, 0.0)


def _gla_kernel(q_ref, k_ref, v_ref, rg_ref, ac_ref, a2_ref, ab_ref, ng_ref, s0_ref, o_ref, st_ref, *, valid):
    @pl.when(pl.program_id(1) == 0)
    def _():
        st_ref[...] = s0_ref[...]

    gl = jnp.dot(ac_ref[...].astype(BF16), a2_ref[...], preferred_element_type=F32) + ab_ref[...]
    log_a = _mask_padding(-_softplus(-gl) / GLA_TAU, valid)
    q = q_ref[...] * (DK_B ** -0.5)
    k = k_ref[...]
    v = v_ref[...]
    rg = rg_ref[...]
    for h in range(H_B):
        ks = slice(h * DK_B, (h + 1) * DK_B)
        vs = slice(h * DV_B, (h + 1) * DV_B)
        o, s_new = _gated_chunk(q[:, ks], k[:, ks], v[:, vs], log_a[:, ks], st_ref[0, h])
        st_ref[0, h] = s_new
        o_ref[:, vs] = (_head_rms(o, ng_ref[...]) * jax.nn.silu(rg[:, vs])).astype(o_ref.dtype)


def _hgrn_kernel(q_ref, z_ref, i_ref, g_ref, lb_ref, ng_ref, s0_ref, o_ref, st_ref, *, valid):
    @pl.when(pl.program_id(1) == 0)
    def _():
        st_ref[...] = s0_ref[...]

    q = jax.nn.silu(q_ref[...])
    lb = lb_ref[...]
    f = lb + (1.0 - lb) * jax.nn.sigmoid(z_ref[...])
    log_f = _mask_padding(jnp.log(jnp.maximum(f, 1e-30)), valid)
    k = 1.0 - f
    v = i_ref[...]
    g = g_ref[...]
    for h in range(H_C):
        hs = slice(h * HD_C, (h + 1) * HD_C)
        o, s_new = _gated_chunk(q[:, hs], k[:, hs], v[:, hs], log_f[:, hs], st_ref[0, h])
        st_ref[0, h] = s_new
        o_ref[:, hs] = (_head_rms(o, ng_ref[...]) * jax.nn.sigmoid(g[:, hs])).astype(o_ref.dtype)


def _seq_spec(chunk, width, col_block, nchunks):
    return pl.BlockSpec((chunk, width), lambda b, c: (b * nchunks + c, col_block))


def _const_spec(shape):
    return pl.BlockSpec(shape, lambda b, c: (0,) * len(shape))


def _state_in_spec(layer, tail):
    return pl.BlockSpec((None, 1) + tail, lambda b, c: (layer, b) + (0,) * len(tail))


def gla_branch(src, cols, a2, a_bias, norm_g, s0, layer, batch, seq, *, chunk, valid=None):
    nch = seq // chunk
    valid = chunk if valid is None else valid
    cq, ck, cv, cg, ca = cols
    a2p = jnp.zeros((LANES, K_B), F32).at[:R_GLA].set(a2).astype(BF16)
    state = pl.BlockSpec((1, H_B, DK_B, DV_B), lambda b, c: (b, 0, 0, 0))
    return pl.pallas_call(
        functools.partial(_gla_kernel, valid=valid),
        grid=(batch, nch),
        in_specs=[_seq_spec(chunk, K_B, cq // K_B, nch), _seq_spec(chunk, K_B, ck // K_B, nch),
                  _seq_spec(chunk, W_B, cv // W_B, nch), _seq_spec(chunk, W_B, cg // W_B, nch),
                  _seq_spec(chunk, LANES, ca // LANES, nch),
                  _const_spec((LANES, K_B)), _const_spec((1, K_B)), _const_spec((1, DV_B)),
                  _state_in_spec(layer, (H_B, DK_B, DV_B))],
        out_specs=[pl.BlockSpec((chunk, W_B), lambda b, c: (b * nch + c, 0)), state],
        out_shape=[jax.ShapeDtypeStruct((batch * seq, W_B), BF16),
                   jax.ShapeDtypeStruct((batch, H_B, DK_B, DV_B), F32)],
        compiler_params=_cparams(("parallel", "arbitrary")),
        name="gla",
    )(src, src, src, src, src, a2p, a_bias.reshape(1, K_B), norm_g.reshape(1, DV_B), s0)


def hgrn_branch(src, col0, lb, norm_g, s0, layer, batch, seq, *, chunk, valid=None):
    nch = seq // chunk
    valid = chunk if valid is None else valid
    c0 = col0 // W_C
    state = pl.BlockSpec((1, H_C, HD_C, HD_C), lambda b, c: (b, 0, 0, 0))
    return pl.pallas_call(
        functools.partial(_hgrn_kernel, valid=valid),
        grid=(batch, nch),
        in_specs=[_seq_spec(chunk, W_C, c0 + j, nch) for j in range(4)]
        + [_const_spec((1, W_C)), _const_spec((1, HD_C)), _state_in_spec(layer, (H_C, HD_C, HD_C))],
        out_specs=[pl.BlockSpec((chunk, W_C), lambda b, c: (b * nch + c, 0)), state],
        out_shape=[jax.ShapeDtypeStruct((batch * seq, W_C), BF16),
                   jax.ShapeDtypeStruct((batch, H_C, HD_C, HD_C), F32)],
        compiler_params=_cparams(("parallel", "arbitrary")),
        name="hgrn",
    )(src, src, src, src, lb.reshape(1, W_C), norm_g.reshape(1, HD_C), s0)


def _conv_gelu_gate(u, u1, u2, gate, cw_ref, cb_ref):
    c = cb_ref[...] + u2 * cw_ref[0:1, :]
    c = c + u1 * cw_ref[1:2, :]
    c = c + u * cw_ref[2:3, :]
    return (jax.nn.gelu(c, approximate=True) * gate).astype(BF16)


def _select_rows(sel, x):
    return jnp.dot(sel, x, precision=lax.Precision.HIGHEST, preferred_element_type=F32)


def _ffn_up_pair_kernel(xp_ref, xs_ref, wu_ref, wg_ref, stp_ref, sts_ref, cw_ref, cb_ref,
                        actp_ref, nsp_ref, acts_ref, nss_ref, wu_bf, wg_bf, carry,
                        *, prompt_tiles, tiles_per_seq, sub, short_seq):
    i = pl.program_id(1)

    @pl.when(i == 0)
    def _():
        wu_bf[...] = wu_ref[...].astype(BF16)
        wg_bf[...] = wg_ref[...].astype(BF16)

    @pl.when(i < prompt_tiles)
    def _():
        first = (i % tiles_per_seq) == 0
        pm2 = jnp.where(first, stp_ref[0, 0:1, :], carry[0:1, :])
        pm1 = jnp.where(first, stp_ref[0, 1:2, :], carry[1:2, :])
        row = lax.broadcasted_iota(jnp.int32, (sub, 1), 0)
        for s in range(xp_ref.shape[0] // sub):
            rows = slice(s * sub, (s + 1) * sub)
            x = xp_ref[rows, :]
            u = jnp.dot(x, wu_bf[...], preferred_element_type=F32)
            gate = jnp.dot(x, wg_bf[...], preferred_element_type=F32)
            u1 = jnp.where(row >= 1, pltpu.roll(u, 1, 0), pm1)
            u2 = jnp.where(row >= 2, pltpu.roll(u, 2, 0), jnp.where(row == 1, pm1, pm2))
            actp_ref[rows, :] = _conv_gelu_gate(u, u1, u2, gate, cw_ref, cb_ref)
            pm2 = u[sub - 2:sub - 1, :]
            pm1 = u[sub - 1:sub, :]
        carry[0:1, :] = pm2
        carry[1:2, :] = pm1
        nsp_ref[0, 0:1, :] = pm2
        nsp_ref[0, 1:2, :] = pm1

    @pl.when(i == prompt_tiles)
    def _():
        x = xs_ref[...]
        u = jnp.dot(x, wu_bf[...], preferred_element_type=F32)
        gate = jnp.dot(x, wg_bf[...], preferred_element_type=F32)
        m = u.shape[0]
        nseq = m // short_seq
        r = lax.broadcasted_iota(jnp.int32, (m, nseq), 0)
        start = lax.broadcasted_iota(jnp.int32, (m, nseq), 1) * short_seq
        at0 = (r == start).astype(F32)
        at1 = (r == start + 1).astype(F32)
        st0 = sts_ref[:, 0, :]
        st1 = sts_ref[:, 1, :]
        tap1 = _select_rows(at0, st1)
        tap2 = _select_rows(at0, st0) + _select_rows(at1, st1)
        pos = lax.rem(lax.broadcasted_iota(jnp.int32, (m, 1), 0), short_seq)
        u1 = jnp.where(pos >= 1, pltpu.roll(u, 1, 0), tap1)
        u2 = jnp.where(pos >= 2, pltpu.roll(u, 2, 0), tap2)
        acts_ref[...] = _conv_gelu_gate(u, u1, u2, gate, cw_ref, cb_ref)
        rr = lax.broadcasted_iota(jnp.int32, (nseq, m), 1)
        end = lax.broadcasted_iota(jnp.int32, (nseq, m), 0) * short_seq + short_seq
        nss_ref[:, 0, :] = _select_rows((rr == end - 2).astype(F32), u)
        nss_ref[:, 1, :] = _select_rows((rr == end - 1).astype(F32), u)


def ffn_up_pair(xp, xs, w_up, w_gate, layer, conv_p, conv_s, conv_w, conv_b, bp, tp, bs, ts, *, tm, tn, sub):
    mp, k = xp.shape
    ms = xs.shape[0]
    pt = mp // tm
    tiles_per_seq = tp // tm
    prow = lambda i: jnp.minimum(i, pt - 1)
    wspec = pl.BlockSpec((None, k, tn), lambda j, i: (layer, 0, j))
    pstate = pl.BlockSpec((1, CONV_W - 1, tn), lambda j, i: (prow(i) // tiles_per_seq, 0, j))
    kern = functools.partial(_ffn_up_pair_kernel, prompt_tiles=pt, tiles_per_seq=tiles_per_seq, sub=sub,
                             short_seq=ts)
    return pl.pallas_call(
        kern,
        grid=(D_FF // tn, pt + 1),
        in_specs=[pl.BlockSpec((tm, k), lambda j, i: (prow(i), 0)),
                  pl.BlockSpec((ms, k), lambda j, i: (0, 0), pipeline_mode=pl.Buffered(1)),
                  wspec, wspec, pstate,
                  pl.BlockSpec((None, bs, CONV_W - 1, tn), lambda j, i: (layer, 0, 0, j)),
                  pl.BlockSpec((CONV_W, tn), lambda j, i: (0, j)), pl.BlockSpec((1, tn), lambda j, i: (0, j))],
        out_specs=[pl.BlockSpec((tm, tn), lambda j, i: (prow(i), j)), pstate,
                   pl.BlockSpec((ms, tn), lambda j, i: (0, j)),
                   pl.BlockSpec((bs, CONV_W - 1, tn), lambda j, i: (0, 0, j))],
        out_shape=[jax.ShapeDtypeStruct((mp, D_FF), BF16), jax.ShapeDtypeStruct((bp, CONV_W - 1, D_FF), F32),
                   jax.ShapeDtypeStruct((ms, D_FF), BF16), jax.ShapeDtypeStruct((bs, CONV_W - 1, D_FF), F32)],
        scratch_shapes=[pltpu.VMEM((k, tn), BF16), pltpu.VMEM((k, tn), BF16), pltpu.VMEM((SUBLANES, tn), F32)],
        compiler_params=_cparams(("arbitrary", "arbitrary")),
        name="ffn_up_pair",
    )(xp, xs, w_up, w_gate, conv_p, conv_s, conv_w, conv_b.reshape(1, D_FF))


NORM_TILE = 256
PREP_TILE = 128
MERGE_TM = 512
MERGE_TN = 256
FFN_TM = 1024
FFN_SUB = 512
FFN_TN = 256
DOWN_TM = 512
SCAN_STEPS = 16
SEQ_CHUNK = 64
DEC_PAD = SUBLANES


def _shifted_rows(x, first, seq):
    w = x.shape[-1]
    xs = x.reshape(-1, seq, w)
    return jnp.concatenate([first[:, None, :], xs[:, :-1]], axis=1).reshape(-1, w)


def _last_rows(x, batch, seq, back):
    return jnp.take(x, jnp.arange(batch) * seq + (seq - 1 - back), axis=0)


def _token_mix(x, proj, batch, seq, long_seq, st, layer, lw):
    m = batch * seq
    prep_w = (lw["rwkv_mu"], lw["rwkv_w0"], lw["rwkv_w2"], lw["rwkv_a0"], lw["rwkv_a2"], lw["rwkv_g2"])
    if long_seq:
        r, d, k, v, a, g = rwkv_prep_seq(proj, st["shift"], prep_w, batch, seq, tm=PREP_TILE)
    else:
        prev = _shifted_rows(proj[:, :A_PROJ], st["shift"], seq)
        r, d, k, v, a, g = rwkv_prep_rows(proj, prev, prep_w, tm=PREP_TILE)
    params = (lw["rwkv_k_k"], lw["rwkv_k_a"], lw["rwkv_r_k"], lw["rwkv_ln_g"], lw["rwkv_ln_b"])
    y_a, s_rwkv = rwkv_recurrence(r, d, k, v, a, st["rwkv"], params, batch, seq,
                                  steps=SCAN_STEPS if long_seq else seq)

    if long_seq:
        o_b, s_gla = gla_branch(proj, (COL_BQ, COL_BK, COL_BV, COL_BG, COL_BA), lw["gla_a2"], lw["gla_a_bias"],
                                lw["gla_norm_g"], st["gla"], layer, batch, seq, chunk=SEQ_CHUNK)
        o_c, s_hgrn = hgrn_branch(proj, COL_C, lw["lb"], lw["hgrn_norm_g"], st["hgrn"], layer, batch, seq,
                                  chunk=SEQ_CHUNK)
    else:
        dec = jnp.concatenate([proj[:, COL_BQ:COL_BA], proj[:, COL_BG:COL_G], proj[:, COL_BA:COL_BA + LANES]],
                              axis=1)
        dec = jnp.pad(dec.reshape(batch, seq, -1), ((0, 0), (0, DEC_PAD - seq), (0, 0)))
        dec = dec.reshape(batch * DEC_PAD, -1)
        gate_col = COL_BA - COL_BQ
        dec_cols = (0, COL_BK - COL_BQ, COL_BV - COL_BQ, gate_col, gate_col + COL_G - COL_BG)
        unpad = lambda o: o.reshape(batch, DEC_PAD, -1)[:, :seq].reshape(m, -1)
        o_b, s_gla = gla_branch(dec, dec_cols, lw["gla_a2"], lw["gla_a_bias"], lw["gla_norm_g"], st["gla"],
                                layer, batch, DEC_PAD, chunk=DEC_PAD, valid=seq)
        o_c, s_hgrn = hgrn_branch(dec, gate_col + W_B, lw["lb"], lw["hgrn_norm_g"], st["hgrn"], layer,
                                  batch, DEC_PAD, chunk=DEC_PAD, valid=seq)
        o_b, o_c = unpad(o_b), unpad(o_c)

    tm = 1024 if m % 1024 == 0 else m
    mtm = MERGE_TM if m % MERGE_TM == 0 else m
    merged = merge(y_a, g, o_b, o_c, lw["p_rwkv"], lw["p_gla"], lw["p_hgrn"], proj, tm=mtm, tn=MERGE_TN)
    mix = matmul(merged, lw["w_out"], tm=tm, tn=1024, name="w_out")
    x = add_rmsnorm(x, mix, lw["norm_mix_post"], tm=NORM_TILE)
    s_shift = _last_rows(proj, batch, seq, 0)[:, :A_PROJ]
    return x, dict(shift=s_shift, rwkv=s_rwkv, gla=s_gla, hgrn=s_hgrn)


def kernel(x_prompt, x_sample, state_rwkv_shift, state_rwkv, state_gla, state_hgrn, state_ffn_conv, norm_mix_pre, norm_mix_post, norm_ffn_pre, norm_ffn_post, w_in, rwkv_mu, rwkv_w0, rwkv_w2, rwkv_a0, rwkv_a2, rwkv_g2, rwkv_k_k, rwkv_k_a, rwkv_r_k, rwkv_ln_g, rwkv_ln_b, gla_a2, gla_a_bias, gla_norm_g, hgrn_lb_logits, hgrn_norm_g, p_rwkv, p_gla, p_hgrn, w_out, ffn_up, ffn_gate, ffn_conv_w, ffn_conv_b, ffn_down):
    bp, tp, _ = x_prompt.shape
    bs, ts, _ = x_sample.shape
    sdt = state_rwkv.dtype

    probs = jax.nn.softmax(hgrn_lb_logits.astype(F32), axis=0)
    lb_all = jnp.cumsum(probs, axis=0) - probs[0]

    xp = x_prompt.reshape(bp * tp, D_MODEL)
    xs = x_sample.reshape(bs * ts, D_MODEL)
    names = ("shift", "rwkv", "gla", "hgrn", "conv")
    new_p = {n: [] for n in names}
    new_s = {n: [] for n in names}
    per_layer = dict(norm_mix_pre=norm_mix_pre, norm_mix_post=norm_mix_post, norm_ffn_pre=norm_ffn_pre,
                     norm_ffn_post=norm_ffn_post, rwkv_mu=rwkv_mu, rwkv_w0=rwkv_w0, rwkv_w2=rwkv_w2,
                     rwkv_a0=rwkv_a0, rwkv_a2=rwkv_a2, rwkv_g2=rwkv_g2, rwkv_k_k=rwkv_k_k, rwkv_k_a=rwkv_k_a,
                     rwkv_r_k=rwkv_r_k, rwkv_ln_g=rwkv_ln_g, rwkv_ln_b=rwkv_ln_b, gla_a2=gla_a2,
                     gla_a_bias=gla_a_bias, gla_norm_g=gla_norm_g, hgrn_norm_g=hgrn_norm_g,
                     ffn_conv_w=ffn_conv_w, ffn_conv_b=ffn_conv_b, lb=lb_all)
    w_in = w_in.astype(F32)
    ffn_up = ffn_up.astype(F32)
    ffn_gate = ffn_gate.astype(F32)
    st_p = dict(shift=jnp.zeros((bp, A_PROJ), F32), rwkv=jnp.zeros((bp, H_A, HS_A, HS_A), F32),
                gla=jnp.zeros((1, bp, H_B, DK_B, DV_B), F32), hgrn=jnp.zeros((1, bp, H_C, HD_C, HD_C), F32))
    conv_p = jnp.zeros((bp, CONV_W - 1, D_FF), F32)
    for l in range(DEPTH):
        lw = {n: arr[l] for n, arr in per_layer.items()}
        for n, arr in (("p_rwkv", p_rwkv), ("p_gla", p_gla), ("p_hgrn", p_hgrn), ("w_out", w_out),
                       ("ffn_down", ffn_down)):
            lw[n] = arr[l].astype(BF16)
        st_s = dict(shift=state_rwkv_shift[l].astype(F32), rwkv=state_rwkv[l].astype(F32),
                    gla=state_gla.astype(F32), hgrn=state_hgrn.astype(F32))

        hp = rmsnorm(xp, lw["norm_mix_pre"], tm=NORM_TILE, out_dtype=BF16)
        hs = rmsnorm(xs, lw["norm_mix_pre"], tm=NORM_TILE, out_dtype=BF16)
        proj_p, proj_s = proj_pair(hp, hs, w_in, l, tm=1024)
        xp, out_p = _token_mix(xp, proj_p, bp, tp, True, st_p, 0, lw)
        xs, out_s = _token_mix(xs, proj_s, bs, ts, False, st_s, l, lw)

        hp = rmsnorm(xp, lw["norm_ffn_pre"], tm=NORM_TILE, out_dtype=BF16)
        hs = rmsnorm(xs, lw["norm_ffn_pre"], tm=NORM_TILE, out_dtype=BF16)
        act_p, out_p["conv"], act_s, out_s["conv"] = ffn_up_pair(
            hp, hs, ffn_up, ffn_gate, l, conv_p, state_ffn_conv.astype(F32), lw["ffn_conv_w"], lw["ffn_conv_b"],
            bp, tp, bs, ts, tm=FFN_TM, tn=FFN_TN, sub=FFN_SUB)
        fp = matmul(act_p, lw["ffn_down"], tm=DOWN_TM, tn=512, name="ffn_down")
        fs = matmul(act_s, lw["ffn_down"], tm=DOWN_TM, tn=512, name="ffn_down")
        xp = add_rmsnorm(xp, fp, lw["norm_ffn_post"], tm=NORM_TILE)
        xs = add_rmsnorm(xs, fs, lw["norm_ffn_post"], tm=NORM_TILE)
        for n in names:
            new_p[n].append(out_p[n])
            new_s[n].append(out_s[n])

    stack = lambda seq_: jnp.stack(seq_).astype(sdt)
    return (xp.reshape(bp, tp, D_MODEL), xs.reshape(bs, ts, D_MODEL),
            *(stack(new_p[n]) for n in names), *(stack(new_s[n]) for n in names))
```

```python
import functools

import jax
import jax.numpy as jnp
from jax import lax
from jax.experimental import pallas as pl
from jax.experimental.pallas import tpu as pltpu

F32 = jnp.float32
BF16 = jnp.bfloat16

D_MODEL = 4096
DEPTH = 2
HS_A = 64
W_A = 3 * D_MODEL // 8
H_A = W_A // HS_A
R_W, R_A, R_G = 128, 128, 256
A_PROJ = 3 * W_A + R_W + R_A + R_G
RWKV_LN_EPS = 64e-5
W_B = 5 * D_MODEL // 16
DV_B = 256
H_B = W_B // DV_B
DK_B = DV_B // 2
K_B = H_B * DK_B
R_GLA = 16
GLA_TAU = 16.0
W_C = 5 * D_MODEL // 16
HD_C = 128
H_C = W_C // HD_C
D_FF = 2 * D_MODEL
CONV_W = 3
RMS_EPS = 1e-6
HEAD_NORM_EPS = 1e-5

LANES = 128
SUBLANES = 8
VMEM_LIMIT_BYTES = 56 * 1024 * 1024

IN_COLS = A_PROJ + 2 * K_B + 2 * W_B + R_GLA + 4 * W_C + 3 * D_MODEL
COL_BQ = A_PROJ
COL_BK = COL_BQ + K_B
COL_BV = COL_BK + K_B
COL_BA = COL_BV + W_B
TAIL_SRC = COL_BA + R_GLA
PROJ_TN = 512
HEAD_TILES = -(-TAIL_SRC // PROJ_TN)
COL_BG = 7 * W_B
PROJ_COLS = COL_BG + IN_COLS - TAIL_SRC
TAIL_TILES = -(-(IN_COLS - TAIL_SRC) // PROJ_TN)
TAIL_TILE0 = PROJ_COLS // PROJ_TN - TAIL_TILES
TAIL_ROW0 = IN_COLS - TAIL_TILES * PROJ_TN
COL_C = COL_BG + W_B
COL_G = COL_C + 4 * W_C
assert PROJ_COLS % PROJ_TN == 0 and TAIL_TILE0 >= HEAD_TILES and TAIL_ROW0 % SUBLANES == 0
assert COL_BG % W_B == 0 and COL_C % W_C == 0 and COL_BA % LANES == 0


def _cparams(semantics):
    return pltpu.CompilerParams(dimension_semantics=semantics, vmem_limit_bytes=VMEM_LIMIT_BYTES)


def _rmsnorm_kernel(x_ref, g_ref, o_ref):
    x = x_ref[...]
    ms = jnp.mean(x * x, axis=-1, keepdims=True)
    o_ref[...] = (x * lax.rsqrt(ms + RMS_EPS) * g_ref[...]).astype(o_ref.dtype)


def rmsnorm(x, g, *, tm, out_dtype):
    m, d = x.shape
    return pl.pallas_call(
        _rmsnorm_kernel,
        grid=(m // tm,),
        in_specs=[pl.BlockSpec((tm, d), lambda i: (i, 0)), pl.BlockSpec((1, d), lambda i: (0, 0))],
        out_specs=pl.BlockSpec((tm, d), lambda i: (i, 0)),
        out_shape=jax.ShapeDtypeStruct((m, d), out_dtype),
        compiler_params=_cparams(("parallel",)),
        name="rmsnorm",
    )(x, g.reshape(1, d))


def _add_rmsnorm_kernel(x_ref, y_ref, g_ref, o_ref):
    y = y_ref[...]
    ms = jnp.mean(y * y, axis=-1, keepdims=True)
    o_ref[...] = x_ref[...] + y * lax.rsqrt(ms + RMS_EPS) * g_ref[...]


def add_rmsnorm(x, y, g, *, tm):
    m, d = x.shape
    return pl.pallas_call(
        _add_rmsnorm_kernel,
        grid=(m // tm,),
        in_specs=[pl.BlockSpec((tm, d), lambda i: (i, 0)), pl.BlockSpec((tm, d), lambda i: (i, 0)),
                  pl.BlockSpec((1, d), lambda i: (0, 0))],
        out_specs=pl.BlockSpec((tm, d), lambda i: (i, 0)),
        out_shape=jax.ShapeDtypeStruct((m, d), F32),
        compiler_params=_cparams(("parallel",)),
        name="add_rmsnorm",
    )(x, y, g.reshape(1, d))


def _matmul_kernel(x_ref, w_ref, o_ref):
    o_ref[...] = jnp.dot(x_ref[...], w_ref[...], preferred_element_type=F32).astype(o_ref.dtype)


def matmul(x, w, *, tm, tn, out_dtype=F32, name="matmul"):
    m, k = x.shape
    n = w.shape[1]
    return pl.pallas_call(
        _matmul_kernel,
        grid=(m // tm, n // tn),
        in_specs=[pl.BlockSpec((tm, k), lambda i, j: (i, 0)), pl.BlockSpec((k, tn), lambda i, j: (0, j))],
        out_specs=pl.BlockSpec((tm, tn), lambda i, j: (i, j)),
        out_shape=jax.ShapeDtypeStruct((m, n), out_dtype),
        compiler_params=_cparams(("parallel", "parallel")),
        name=name,
    )(x, w)


def _proj_pair_kernel(xp_ref, xs_ref, wt_ref, op_ref, os_ref, w_bf, *, prompt_tiles):
    i = pl.program_id(1)

    @pl.when(i == 0)
    def _():
        w_bf[...] = wt_ref[...].astype(BF16)

    contract_last = (((1,), (1,)), ((), ()))

    @pl.when(i < prompt_tiles)
    def _():
        op_ref[...] = lax.dot_general(xp_ref[...], w_bf[...], contract_last, preferred_element_type=F32)

    @pl.when(i == prompt_tiles)
    def _():
        os_ref[...] = lax.dot_general(xs_ref[...], w_bf[...], contract_last, preferred_element_type=F32)


def proj_pair(xp, xs, w_in_t, layer, *, tm):
    mp, k = xp.shape
    ms = xs.shape[0]
    pt = mp // tm
    prow = lambda i: jnp.minimum(i, pt - 1)
    out_col = lambda j: jnp.where(j < HEAD_TILES, j, j + (TAIL_TILE0 - HEAD_TILES))
    w_row = lambda j: pl.multiple_of(
        jnp.where(j < HEAD_TILES, j * PROJ_TN, TAIL_ROW0 + (j - HEAD_TILES) * PROJ_TN), SUBLANES)
    return pl.pallas_call(
        functools.partial(_proj_pair_kernel, prompt_tiles=pt),
        grid=(HEAD_TILES + TAIL_TILES, pt + 1),
        in_specs=[pl.BlockSpec((tm, k), lambda j, i: (prow(i), 0)),
                  pl.BlockSpec((ms, k), lambda j, i: (0, 0), pipeline_mode=pl.Buffered(1)),
                  pl.BlockSpec((None, pl.Element(PROJ_TN), pl.Element(k)), lambda j, i: (layer, w_row(j), 0))],
        out_specs=[pl.BlockSpec((tm, PROJ_TN), lambda j, i: (prow(i), out_col(j))),
                   pl.BlockSpec((ms, PROJ_TN), lambda j, i: (0, out_col(j)))],
        out_shape=[jax.ShapeDtypeStruct((mp, PROJ_COLS), F32), jax.ShapeDtypeStruct((ms, PROJ_COLS), F32)],
        scratch_shapes=[pltpu.VMEM((PROJ_TN, k), BF16)],
        compiler_params=_cparams(("arbitrary", "arbitrary")),
        name="proj_pair",
    )(xp, xs, w_in_t)


def _merge_kernel(y_ref, g_ref, ob_ref, oc_ref, pa_ref, pb_ref, pc_ref, ga_ref, gb_ref, gc_ref, o_ref):
    oa = (y_ref[...] * g_ref[...]).astype(BF16)
    ma = jnp.dot(oa, pa_ref[...], preferred_element_type=F32)
    mb = jnp.dot(ob_ref[...], pb_ref[...], preferred_element_type=F32)
    mc = jnp.dot(oc_ref[...], pc_ref[...], preferred_element_type=F32)
    merged = (jax.nn.sigmoid(ga_ref[...]) * ma + jax.nn.sigmoid(gb_ref[...]) * mb
              + jax.nn.sigmoid(gc_ref[...]) * mc)
    o_ref[...] = merged.astype(o_ref.dtype)


def merge(y_a, g_a, o_b, o_c, p_a, p_b, p_c, proj, *, tm, tn):
    m = y_a.shape[0]
    d = p_a.shape[1]
    gate0 = COL_G // tn
    gstep = d // tn

    def row(width):
        return pl.BlockSpec((tm, width), lambda i, j: (i, 0))

    def wcol(kdim):
        return pl.BlockSpec((kdim, tn), lambda i, j: (0, j))

    def gate(branch):
        return pl.BlockSpec((tm, tn), lambda i, j: (i, gate0 + branch * gstep + j))

    return pl.pallas_call(
        _merge_kernel,
        grid=(m // tm, d // tn),
        in_specs=[row(W_A), row(W_A), row(W_B), row(W_C), wcol(W_A), wcol(W_B), wcol(W_C),
                  gate(0), gate(1), gate(2)],
        out_specs=pl.BlockSpec((tm, tn), lambda i, j: (i, j)),
        out_shape=jax.ShapeDtypeStruct((m, d), BF16),
        compiler_params=_cparams(("parallel", "parallel")),
        name="merge",
    )(y_a, g_a, o_b, o_c, p_a, p_b, p_c, proj, proj, proj)


def _softplus(x):
    return jnp.maximum(x, 0.0) + jnp.log(1.0 + jnp.exp(-jnp.abs(x)))


def _rwkv_gates(p, prev, mu_ref, w0_ref, w2_ref, a0_ref, a2_ref, g2_ref, r_ref, d_ref, k_ref, v_ref, a_ref, g_ref):
    ps = p + mu_ref[...] * (prev - p)
    o1 = 3 * W_A
    r_ref[...] = ps[:, :W_A]
    k_ref[...] = ps[:, W_A:2 * W_A]
    v_ref[...] = ps[:, 2 * W_A:o1]
    wc = ps[:, o1:o1 + R_W]
    ac = ps[:, o1 + R_W:o1 + R_W + R_A]
    gc = ps[:, o1 + R_W + R_A:]
    wl = w0_ref[...] + jnp.dot(jnp.tanh(wc).astype(BF16), w2_ref[...], preferred_element_type=F32)
    w = -_softplus(-wl) - 0.5
    d_ref[...] = jnp.exp(-jnp.exp(w))
    al = a0_ref[...] + jnp.dot(ac.astype(BF16), a2_ref[...], preferred_element_type=F32)
    a_ref[...] = jax.nn.sigmoid(al)
    g_ref[...] = jnp.dot(jax.nn.sigmoid(gc).astype(BF16), g2_ref[...], preferred_element_type=F32)


def _rwkv_prep_seq_kernel(p_ref, tail_ref, shift_ref, *rest, tiles_per_seq):
    i = pl.program_id(0)
    p = p_ref[...]
    first = (i % tiles_per_seq) == 0
    before = jnp.where(first, shift_ref[0], tail_ref[SUBLANES - 1:SUBLANES, :])
    row = lax.broadcasted_iota(jnp.int32, (p.shape[0], 1), 0)
    prev = jnp.where(row >= 1, pltpu.roll(p, 1, 0), before)
    _rwkv_gates(p, prev, *rest)


def _rwkv_prep_rows_kernel(p_ref, prev_ref, *rest):
    _rwkv_gates(p_ref[...], prev_ref[...], *rest)


def _rwkv_prep_call(kernel, lead_specs, lead_args, m, tm, mu, w0, w2, a0, a2, g2, name):
    full = lambda shape: pl.BlockSpec(shape, lambda i: (0, 0))
    out = jax.ShapeDtypeStruct((m, W_A), F32)
    return pl.pallas_call(
        kernel,
        grid=(m // tm,),
        in_specs=lead_specs + [full((1, A_PROJ)), full((1, W_A)), full((R_W, W_A)), full((1, W_A)),
                               full((R_A, W_A)), full((R_G, W_A))],
        out_specs=[pl.BlockSpec((tm, W_A), lambda i: (i, 0))] * 6,
        out_shape=[out] * 6,
        compiler_params=_cparams(("parallel",)),
        name=name,
    )(*lead_args, mu.reshape(1, A_PROJ), w0.reshape(1, W_A), w2.astype(BF16), a0.reshape(1, W_A),
      a2.astype(BF16), g2.astype(BF16))


def rwkv_prep_seq(proj, shift, weights, batch, seq, *, tm):
    tiles_per_seq = seq // tm
    nblk = tm // SUBLANES
    specs = [pl.BlockSpec((tm, A_PROJ), lambda i: (i, 0)),
             pl.BlockSpec((SUBLANES, A_PROJ), lambda i: (jnp.maximum(i * nblk - 1, 0), 0)),
             pl.BlockSpec((1, 1, A_PROJ), lambda i: (i // tiles_per_seq, 0, 0))]
    kern = functools.partial(_rwkv_prep_seq_kernel, tiles_per_seq=tiles_per_seq)
    return _rwkv_prep_call(kern, specs, (proj, proj, shift.reshape(batch, 1, A_PROJ)), batch * seq, tm,
                           *weights, name="rwkv_prep_seq")


def rwkv_prep_rows(proj, prev, weights, *, tm):
    specs = [pl.BlockSpec((tm, A_PROJ), lambda i: (i, 0)), pl.BlockSpec((tm, A_PROJ), lambda i: (i, 0))]
    return _rwkv_prep_call(_rwkv_prep_rows_kernel, specs, (proj, prev), proj.shape[0], tm, *weights,
                           name="rwkv_prep_rows")


def _rwkv_scan_kernel(r_ref, d_ref, k_ref, v_ref, a_ref, s0_ref, kk_ref, ka_ref, rk_ref, lng_ref, lnb_ref,
                      y_ref, st_ref, s_scr, o_scr, *, steps):
    tc = pl.program_id(1)

    @pl.when(tc == 0)
    def _():
        s_scr[...] = s0_ref[...]

    row = lax.broadcasted_iota(jnp.int32, (SUBLANES, 1), 0)

    def step(t, carry):
        r = r_ref[t]
        d = d_ref[t]
        k = k_ref[t]
        a = a_ref[t]
        kk = k * kk_ref[...]
        kk = kk * lax.rsqrt(jnp.maximum(jnp.sum(kk * kk, axis=0, keepdims=True), 1e-24))
        kp = k * (1.0 + (a - 1.0) * ka_ref[...])
        kka = kk * a

        def vblock(vo, c):
            v8 = v_ref[t, pl.ds(vo * SUBLANES, SUBLANES), :]
            o8 = jnp.zeros_like(v8)
            for vi in range(SUBLANES):
                s = s_scr[vo, vi]
                skk = jnp.sum(s * kk, axis=0, keepdims=True)
                s = s * d - skk * kka + v8[vi:vi + 1, :] * kp
                s_scr[vo, vi] = s
                o8 = jnp.where(row == vi, jnp.sum(s * r, axis=0, keepdims=True), o8)
            o_scr[pl.ds(vo * SUBLANES, SUBLANES), :] = o8
            return c

        lax.fori_loop(0, HS_A // SUBLANES, vblock, 0)
        o = o_scr[...]
        v = v_ref[t]
        mean = jnp.mean(o, axis=0, keepdims=True)
        var = jnp.mean(jnp.square(o - mean), axis=0, keepdims=True)
        y = (o - mean) * lax.rsqrt(var + RWKV_LN_EPS) * lng_ref[...] + lnb_ref[...]
        bonus = jnp.sum(r * kp * rk_ref[...], axis=0, keepdims=True) * v
        y_ref[t] = y + bonus
        return carry

    lax.fori_loop(0, steps, step, 0)

    @pl.when(tc == pl.num_programs(1) - 1)
    def _():
        st_ref[...] = s_scr[...]


def rwkv_scan(r, d, k, v, a, s0, layer, kk_p, ka_p, rk_p, lng_p, lnb_p, *, steps):
    t, c, lanes = r.shape
    nl = lanes // LANES
    vo = c // SUBLANES
    s0 = s0.reshape(s0.shape[0], nl, vo, SUBLANES, c, LANES)
    tok = pl.BlockSpec((steps, c, LANES), lambda l, i: (i, 0, l))
    par = pl.BlockSpec((c, LANES), lambda l, i: (0, l))
    st_in = pl.BlockSpec((None, None, vo, SUBLANES, c, LANES), lambda l, i: (layer, l, 0, 0, 0, 0))
    st_out = pl.BlockSpec((None, vo, SUBLANES, c, LANES), lambda l, i: (l, 0, 0, 0, 0))
    y, s_t = pl.pallas_call(
        functools.partial(_rwkv_scan_kernel, steps=steps),
        grid=(nl, t // steps),
        in_specs=[tok] * 5 + [st_in] + [par] * 5,
        out_specs=[tok, st_out],
        out_shape=[jax.ShapeDtypeStruct((t, c, lanes), F32),
                   jax.ShapeDtypeStruct((nl, vo, SUBLANES, c, LANES), F32)],
        scratch_shapes=[pltpu.VMEM((vo, SUBLANES, c, LANES), F32), pltpu.VMEM((c, LANES), F32)],
        compiler_params=_cparams(("parallel", "arbitrary")),
        name="rwkv_scan",
    )(r, d, k, v, a, s0, kk_p, ka_p, rk_p, lng_p, lnb_p)
    return y, s_t.reshape(nl, c, c, LANES)


def _pad_lanes(x):
    pad = -x.shape[-1] % LANES
    if pad == 0:
        return x
    return jnp.pad(x, [(0, 0)] * (x.ndim - 1) + [(0, pad)])


def rwkv_recurrence(r, d, k, v, a, s0, layer, params, batch, seq, *, steps, head_major):
    lanes = batch * H_A
    to_lanes = (1, 3, 2, 0) if head_major else (1, 3, 0, 2)
    from_lanes = (3, 0, 2, 1) if head_major else (2, 0, 3, 1)

    def lanes_of(z):
        z = jnp.transpose(z.reshape(batch, seq, H_A, HS_A), to_lanes)
        return _pad_lanes(z.reshape(seq, HS_A, lanes))

    def param_tile(p):
        pt = p.reshape(H_A, HS_A).T
        return _pad_lanes(jnp.repeat(pt, batch, axis=1) if head_major else jnp.tile(pt, (1, batch)))

    if head_major:
        s0l = s0
    else:
        s0l = _pad_lanes(jnp.transpose(s0.reshape(lanes, HS_A, HS_A), (1, 2, 0)))[None, None]
    y, s_t = rwkv_scan(*(lanes_of(z) for z in (r, d, k, v, a)), s0l, layer, *(param_tile(p) for p in params),
                       steps=steps)
    shape4 = (seq, HS_A, H_A, batch) if head_major else (seq, HS_A, batch, H_A)
    y = jnp.transpose(y[:, :, :lanes].reshape(shape4), from_lanes).reshape(batch * seq, W_A)
    if not head_major:
        s_t = jnp.transpose(s_t[0, :, :, :lanes], (2, 0, 1)).reshape(batch, H_A, HS_A, HS_A)
    return y, s_t


def _block_prefix_sum(x):
    row = lax.broadcasted_iota(jnp.int32, (SUBLANES, 1), 0)
    for shift in (1, 2, 4):
        x = x + jnp.where(row >= shift, pltpu.roll(x, shift, 0), 0.0)
    return x


def _chunk_scores(q, k, blocks):
    c = q.shape[0]
    nb = c // SUBLANES
    row = lax.broadcasted_iota(jnp.int32, (SUBLANES, 1), 0)
    lane = lax.broadcasted_iota(jnp.int32, (SUBLANES, c), 1)
    rows = []
    for ib in range(nb):
        lo = ib * SUBLANES
        qi = q[lo:lo + SUBLANES, :]
        ki = k[lo:lo + SUBLANES, :]
        bi = blocks[ib]
        att = jnp.zeros((SUBLANES, c), F32)
        for j in range(SUBLANES):
            decay = jnp.exp(jnp.minimum(bi - bi[j:j + 1, :], 0.0))
            col = jnp.sum(qi * ki[j:j + 1, :] * decay, axis=-1, keepdims=True)
            att = jnp.where(lane == lo + j, col, att)
        att = jnp.where(lane <= lo + row, att, 0.0)
        if ib > 0:
            ref = blocks[ib - 1][SUBLANES - 1:SUBLANES, :]
            qs = (qi * jnp.exp(bi - ref)).astype(BF16)
            ks = [k[jb * SUBLANES:(jb + 1) * SUBLANES, :] * jnp.exp(ref - blocks[jb]) for jb in range(ib)]
            ks.append(jnp.zeros((c - lo, q.shape[1]), F32))
            ks = jnp.concatenate(ks, axis=0).astype(BF16)
            att = att + lax.dot_general(qs, ks, (((1,), (1,)), ((), ())), preferred_element_type=F32)
        rows.append(att)
    return jnp.concatenate(rows, axis=0)


def _gated_chunk(q, k, v, log_a, s_prev):
    nb = q.shape[0] // SUBLANES
    blocks = []
    carry = None
    for ib in range(nb):
        bi = _block_prefix_sum(log_a[ib * SUBLANES:(ib + 1) * SUBLANES, :])
        if carry is not None:
            bi = bi + carry
        carry = bi[SUBLANES - 1:SUBLANES, :]
        blocks.append(bi)
    b = jnp.concatenate(blocks, axis=0)
    b_last = carry
    att = _chunk_scores(q, k, blocks)
    v_bf = v.astype(BF16)
    o = jnp.dot(att.astype(BF16), v_bf, preferred_element_type=F32)
    qe = (q * jnp.exp(b)).astype(BF16)
    o = o + jnp.dot(qe, s_prev.astype(BF16), preferred_element_type=F32)
    kd = (k * jnp.exp(b_last - b)).astype(BF16)
    e_col = jnp.transpose(jnp.broadcast_to(jnp.exp(b_last), (SUBLANES, q.shape[1])))[:, 0:1]
    s_new = s_prev * e_col + lax.dot_general(kd, v_bf, (((0,), (0,)), ((), ())), preferred_element_type=F32)
    return o, s_new


def _head_rms(o, g):
    return o * lax.rsqrt(jnp.mean(o * o, axis=-1, keepdims=True) + HEAD_NORM_EPS) * g


def _mask_padding(log_decay, valid):
    if valid >= log_decay.shape[0]:
        return log_decay
    row = lax.broadcasted_iota(jnp.int32, (log_decay.shape[0], 1), 0)
    return jnp.where(row < valid, log_decay, 0.0)


def _gla_kernel(q_ref, k_ref, v_ref, rg_ref, ac_ref, a2_ref, ab_ref, ng_ref, s0_ref, o_ref, st_ref, *, valid):
    @pl.when(pl.program_id(1) == 0)
    def _():
        st_ref[...] = s0_ref[...]

    gl = jnp.dot(ac_ref[...].astype(BF16), a2_ref[...], preferred_element_type=F32) + ab_ref[...]
    log_a = _mask_padding(-_softplus(-gl) / GLA_TAU, valid)
    q = q_ref[...] * (DK_B ** -0.5)
    k = k_ref[...]
    v = v_ref[...]
    rg = rg_ref[...]
    for h in range(H_B):
        ks = slice(h * DK_B, (h + 1) * DK_B)
        vs = slice(h * DV_B, (h + 1) * DV_B)
        o, s_new = _gated_chunk(q[:, ks], k[:, ks], v[:, vs], log_a[:, ks], st_ref[0, h])
        st_ref[0, h] = s_new
        o_ref[:, vs] = (_head_rms(o, ng_ref[...]) * jax.nn.silu(rg[:, vs])).astype(o_ref.dtype)


def _hgrn_kernel(q_ref, z_ref, i_ref, g_ref, lb_ref, ng_ref, s0_ref, o_ref, st_ref, *, valid):
    @pl.when(pl.program_id(1) == 0)
    def _():
        st_ref[...] = s0_ref[...]

    q = jax.nn.silu(q_ref[...])
    lb = lb_ref[...]
    f = lb + (1.0 - lb) * jax.nn.sigmoid(z_ref[...])
    log_f = _mask_padding(jnp.log(jnp.maximum(f, 1e-30)), valid)
    k = 1.0 - f
    v = i_ref[...]
    g = g_ref[...]
    for h in range(H_C):
        hs = slice(h * HD_C, (h + 1) * HD_C)
        o, s_new = _gated_chunk(q[:, hs], k[:, hs], v[:, hs], log_f[:, hs], st_ref[0, h])
        st_ref[0, h] = s_new
        o_ref[:, hs] = (_head_rms(o, ng_ref[...]) * jax.nn.sigmoid(g[:, hs])).astype(o_ref.dtype)


def _seq_spec(chunk, width, col_block, nchunks):
    return pl.BlockSpec((chunk, width), lambda b, c: (b * nchunks + c, col_block))


def _const_spec(shape):
    return pl.BlockSpec(shape, lambda b, c: (0,) * len(shape))


def _state_in_spec(layer, tail):
    return pl.BlockSpec((None, 1) + tail, lambda b, c: (layer, b) + (0,) * len(tail))


def gla_branch(src, cols, a2, a_bias, norm_g, s0, layer, batch, seq, *, chunk, valid=None):
    nch = seq // chunk
    valid = chunk if valid is None else valid
    cq, ck, cv, cg, ca = cols
    a2p = jnp.zeros((LANES, K_B), F32).at[:R_GLA].set(a2).astype(BF16)
    state = pl.BlockSpec((1, H_B, DK_B, DV_B), lambda b, c: (b, 0, 0, 0))
    return pl.pallas_call(
        functools.partial(_gla_kernel, valid=valid),
        grid=(batch, nch),
        in_specs=[_seq_spec(chunk, K_B, cq // K_B, nch), _seq_spec(chunk, K_B, ck // K_B, nch),
                  _seq_spec(chunk, W_B, cv // W_B, nch), _seq_spec(chunk, W_B, cg // W_B, nch),
                  _seq_spec(chunk, LANES, ca // LANES, nch),
                  _const_spec((LANES, K_B)), _const_spec((1, K_B)), _const_spec((1, DV_B)),
                  _state_in_spec(layer, (H_B, DK_B, DV_B))],
        out_specs=[pl.BlockSpec((chunk, W_B), lambda b, c: (b * nch + c, 0)), state],
        out_shape=[jax.ShapeDtypeStruct((batch * seq, W_B), BF16),
                   jax.ShapeDtypeStruct((batch, H_B, DK_B, DV_B), F32)],
        compiler_params=_cparams(("parallel", "arbitrary")),
        name="gla",
    )(src, src, src, src, src, a2p, a_bias.reshape(1, K_B), norm_g.reshape(1, DV_B), s0)


def hgrn_branch(src, col0, lb, norm_g, s0, layer, batch, seq, *, chunk, valid=None):
    nch = seq // chunk
    valid = chunk if valid is None else valid
    c0 = col0 // W_C
    state = pl.BlockSpec((1, H_C, HD_C, HD_C), lambda b, c: (b, 0, 0, 0))
    return pl.pallas_call(
        functools.partial(_hgrn_kernel, valid=valid),
        grid=(batch, nch),
        in_specs=[_seq_spec(chunk, W_C, c0 + j, nch) for j in range(4)]
        + [_const_spec((1, W_C)), _const_spec((1, HD_C)), _state_in_spec(layer, (H_C, HD_C, HD_C))],
        out_specs=[pl.BlockSpec((chunk, W_C), lambda b, c: (b * nch + c, 0)), state],
        out_shape=[jax.ShapeDtypeStruct((batch * seq, W_C), BF16),
                   jax.ShapeDtypeStruct((batch, H_C, HD_C, HD_C), F32)],
        compiler_params=_cparams(("parallel", "arbitrary")),
        name="hgrn",
    )(src, src, src, src, lb.reshape(1, W_C), norm_g.reshape(1, HD_C), s0)


def _conv_gelu_gate(u, u1, u2, gate, cw_ref, cb_ref):
    c = cb_ref[...] + u2 * cw_ref[0:1, :]
    c = c + u1 * cw_ref[1:2, :]
    c = c + u * cw_ref[2:3, :]
    return (jax.nn.gelu(c, approximate=True) * gate).astype(BF16)


def _select_rows(sel, x):
    return jnp.dot(sel, x, precision=lax.Precision.HIGHEST, preferred_element_type=F32)


def _ffn_up_pair_kernel(xp_ref, xs_ref, wu_ref, wg_ref, stp_ref, sts_ref, cw_ref, cb_ref,
                        actp_ref, nsp_ref, acts_ref, nss_ref, wu_bf, wg_bf, carry,
                        *, prompt_tiles, tiles_per_seq, sub, short_seq):
    i = pl.program_id(1)

    @pl.when(i == 0)
    def _():
        wu_bf[...] = wu_ref[...].astype(BF16)
        wg_bf[...] = wg_ref[...].astype(BF16)

    @pl.when(i < prompt_tiles)
    def _():
        first = (i % tiles_per_seq) == 0
        pm2 = jnp.where(first, stp_ref[0, 0:1, :], carry[0:1, :])
        pm1 = jnp.where(first, stp_ref[0, 1:2, :], carry[1:2, :])
        row = lax.broadcasted_iota(jnp.int32, (sub, 1), 0)
        for s in range(xp_ref.shape[0] // sub):
            rows = slice(s * sub, (s + 1) * sub)
            x = xp_ref[rows, :]
            u = jnp.dot(x, wu_bf[...], preferred_element_type=F32)
            gate = jnp.dot(x, wg_bf[...], preferred_element_type=F32)
            u1 = jnp.where(row >= 1, pltpu.roll(u, 1, 0), pm1)
            u2 = jnp.where(row >= 2, pltpu.roll(u, 2, 0), jnp.where(row == 1, pm1, pm2))
            actp_ref[rows, :] = _conv_gelu_gate(u, u1, u2, gate, cw_ref, cb_ref)
            pm2 = u[sub - 2:sub - 1, :]
            pm1 = u[sub - 1:sub, :]
        carry[0:1, :] = pm2
        carry[1:2, :] = pm1
        nsp_ref[0, 0:1, :] = pm2
        nsp_ref[0, 1:2, :] = pm1

    @pl.when(i == prompt_tiles)
    def _():
        x = xs_ref[...]
        u = jnp.dot(x, wu_bf[...], preferred_element_type=F32)
        gate = jnp.dot(x, wg_bf[...], preferred_element_type=F32)
        m = u.shape[0]
        nseq = m // short_seq
        r = lax.broadcasted_iota(jnp.int32, (m, nseq), 0)
        start = lax.broadcasted_iota(jnp.int32, (m, nseq), 1) * short_seq
        at0 = (r == start).astype(F32)
        at1 = (r == start + 1).astype(F32)
        st0 = sts_ref[:, 0, :]
        st1 = sts_ref[:, 1, :]
        tap1 = _select_rows(at0, st1)
        tap2 = _select_rows(at0, st0) + _select_rows(at1, st1)
        pos = lax.rem(lax.broadcasted_iota(jnp.int32, (m, 1), 0), short_seq)
        u1 = jnp.where(pos >= 1, pltpu.roll(u, 1, 0), tap1)
        u2 = jnp.where(pos >= 2, pltpu.roll(u, 2, 0), tap2)
        acts_ref[...] = _conv_gelu_gate(u, u1, u2, gate, cw_ref, cb_ref)
        rr = lax.broadcasted_iota(jnp.int32, (nseq, m), 1)
        end = lax.broadcasted_iota(jnp.int32, (nseq, m), 0) * short_seq + short_seq
        nss_ref[:, 0, :] = _select_rows((rr == end - 2).astype(F32), u)
        nss_ref[:, 1, :] = _select_rows((rr == end - 1).astype(F32), u)


def ffn_up_pair(xp, xs, w_up, w_gate, layer, conv_p, conv_s, conv_w, conv_b, bp, tp, bs, ts, *, tm, tn, sub):
    mp, k = xp.shape
    ms = xs.shape[0]
    pt = mp // tm
    tiles_per_seq = tp // tm
    prow = lambda i: jnp.minimum(i, pt - 1)
    wspec = pl.BlockSpec((None, k, tn), lambda j, i: (layer, 0, j))
    pstate = pl.BlockSpec((1, CONV_W - 1, tn), lambda j, i: (prow(i) // tiles_per_seq, 0, j))
    kern = functools.partial(_ffn_up_pair_kernel, prompt_tiles=pt, tiles_per_seq=tiles_per_seq, sub=sub,
                             short_seq=ts)
    return pl.pallas_call(
        kern,
        grid=(D_FF // tn, pt + 1),
        in_specs=[pl.BlockSpec((tm, k), lambda j, i: (prow(i), 0)),
                  pl.BlockSpec((ms, k), lambda j, i: (0, 0), pipeline_mode=pl.Buffered(1)),
                  wspec, wspec, pstate,
                  pl.BlockSpec((None, bs, CONV_W - 1, tn), lambda j, i: (layer, 0, 0, j)),
                  pl.BlockSpec((CONV_W, tn), lambda j, i: (0, j)), pl.BlockSpec((1, tn), lambda j, i: (0, j))],
        out_specs=[pl.BlockSpec((tm, tn), lambda j, i: (prow(i), j)), pstate,
                   pl.BlockSpec((ms, tn), lambda j, i: (0, j)),
                   pl.BlockSpec((bs, CONV_W - 1, tn), lambda j, i: (0, 0, j))],
        out_shape=[jax.ShapeDtypeStruct((mp, D_FF), BF16), jax.ShapeDtypeStruct((bp, CONV_W - 1, D_FF), F32),
                   jax.ShapeDtypeStruct((ms, D_FF), BF16), jax.ShapeDtypeStruct((bs, CONV_W - 1, D_FF), F32)],
        scratch_shapes=[pltpu.VMEM((k, tn), BF16), pltpu.VMEM((k, tn), BF16), pltpu.VMEM((SUBLANES, tn), F32)],
        compiler_params=_cparams(("arbitrary", "arbitrary")),
        name="ffn_up_pair",
    )(xp, xs, w_up, w_gate, conv_p, conv_s, conv_w, conv_b.reshape(1, D_FF))


NORM_TILE = 256
PREP_TILE = 128
MERGE_TM = 512
MERGE_TN = 256
FFN_TM = 1024
FFN_SUB = 512
FFN_TN = 256
DOWN_TM = 512
SCAN_STEPS = 16
SEQ_CHUNK = 64
DEC_PAD = SUBLANES


def _shifted_rows(x, first, seq):
    w = x.shape[-1]
    xs = x.reshape(-1, seq, w)
    return jnp.concatenate([first[:, None, :], xs[:, :-1]], axis=1).reshape(-1, w)


def _last_rows(x, batch, seq, back):
    return jnp.take(x, jnp.arange(batch) * seq + (seq - 1 - back), axis=0)


def _token_mix(x, proj, batch, seq, long_seq, st, layer, lw):
    m = batch * seq
    prep_w = (lw["rwkv_mu"], lw["rwkv_w0"], lw["rwkv_w2"], lw["rwkv_a0"], lw["rwkv_a2"], lw["rwkv_g2"])
    if long_seq:
        r, d, k, v, a, g = rwkv_prep_seq(proj, st["shift"], prep_w, batch, seq, tm=PREP_TILE)
    else:
        prev = _shifted_rows(proj[:, :A_PROJ], st["shift"], seq)
        r, d, k, v, a, g = rwkv_prep_rows(proj, prev, prep_w, tm=PREP_TILE)
    params = (lw["rwkv_k_k"], lw["rwkv_k_a"], lw["rwkv_r_k"], lw["rwkv_ln_g"], lw["rwkv_ln_b"])
    y_a, s_rwkv = rwkv_recurrence(r, d, k, v, a, st["rwkv"], layer, params, batch, seq,
                                  steps=SCAN_STEPS if long_seq else seq, head_major=not long_seq)

    if long_seq:
        o_b, s_gla = gla_branch(proj, (COL_BQ, COL_BK, COL_BV, COL_BG, COL_BA), lw["gla_a2"], lw["gla_a_bias"],
                                lw["gla_norm_g"], st["gla"], layer, batch, seq, chunk=SEQ_CHUNK)
        o_c, s_hgrn = hgrn_branch(proj, COL_C, lw["lb"], lw["hgrn_norm_g"], st["hgrn"], layer, batch, seq,
                                  chunk=SEQ_CHUNK)
    else:
        dec = jnp.concatenate([proj[:, COL_BQ:COL_BA], proj[:, COL_BG:COL_G], proj[:, COL_BA:COL_BA + LANES]],
                              axis=1)
        dec = jnp.pad(dec.reshape(batch, seq, -1), ((0, 0), (0, DEC_PAD - seq), (0, 0)))
        dec = dec.reshape(batch * DEC_PAD, -1)
        gate_col = COL_BA - COL_BQ
        dec_cols = (0, COL_BK - COL_BQ, COL_BV - COL_BQ, gate_col, gate_col + COL_G - COL_BG)
        unpad = lambda o: o.reshape(batch, DEC_PAD, -1)[:, :seq].reshape(m, -1)
        o_b, s_gla = gla_branch(dec, dec_cols, lw["gla_a2"], lw["gla_a_bias"], lw["gla_norm_g"], st["gla"],
                                layer, batch, DEC_PAD, chunk=DEC_PAD, valid=seq)
        o_c, s_hgrn = hgrn_branch(dec, gate_col + W_B, lw["lb"], lw["hgrn_norm_g"], st["hgrn"], layer,
                                  batch, DEC_PAD, chunk=DEC_PAD, valid=seq)
        o_b, o_c = unpad(o_b), unpad(o_c)

    tm = 1024 if m % 1024 == 0 else m
    mtm = MERGE_TM if m % MERGE_TM == 0 else m
    merged = merge(y_a, g, o_b, o_c, lw["p_rwkv"], lw["p_gla"], lw["p_hgrn"], proj, tm=mtm, tn=MERGE_TN)
    mix = matmul(merged, lw["w_out"], tm=tm, tn=1024, name="w_out")
    x = add_rmsnorm(x, mix, lw["norm_mix_post"], tm=NORM_TILE)
    s_shift = _last_rows(proj, batch, seq, 0)[:, :A_PROJ]
    return x, dict(shift=s_shift, rwkv=s_rwkv, gla=s_gla, hgrn=s_hgrn)


def kernel(x_prompt, x_sample, state_rwkv_shift, state_rwkv, state_gla, state_hgrn, state_ffn_conv, norm_mix_pre, norm_mix_post, norm_ffn_pre, norm_ffn_post, w_in, rwkv_mu, rwkv_w0, rwkv_w2, rwkv_a0, rwkv_a2, rwkv_g2, rwkv_k_k, rwkv_k_a, rwkv_r_k, rwkv_ln_g, rwkv_ln_b, gla_a2, gla_a_bias, gla_norm_g, hgrn_lb_logits, hgrn_norm_g, p_rwkv, p_gla, p_hgrn, w_out, ffn_up, ffn_gate, ffn_conv_w, ffn_conv_b, ffn_down):
    bp, tp, _ = x_prompt.shape
    bs, ts, _ = x_sample.shape
    sdt = state_rwkv.dtype

    probs = jax.nn.softmax(hgrn_lb_logits.astype(F32), axis=0)
    lb_all = jnp.cumsum(probs, axis=0) - probs[0]

    xp = x_prompt.reshape(bp * tp, D_MODEL)
    xs = x_sample.reshape(bs * ts, D_MODEL)
    names = ("shift", "rwkv", "gla", "hgrn", "conv")
    new_p = {n: [] for n in names}
    new_s = {n: [] for n in names}
    per_layer = dict(norm_mix_pre=norm_mix_pre, norm_mix_post=norm_mix_post, norm_ffn_pre=norm_ffn_pre,
                     norm_ffn_post=norm_ffn_post, rwkv_mu=rwkv_mu, rwkv_w0=rwkv_w0, rwkv_w2=rwkv_w2,
                     rwkv_a0=rwkv_a0, rwkv_a2=rwkv_a2, rwkv_g2=rwkv_g2, rwkv_k_k=rwkv_k_k, rwkv_k_a=rwkv_k_a,
                     rwkv_r_k=rwkv_r_k, rwkv_ln_g=rwkv_ln_g, rwkv_ln_b=rwkv_ln_b, gla_a2=gla_a2,
                     gla_a_bias=gla_a_bias, gla_norm_g=gla_norm_g, hgrn_norm_g=hgrn_norm_g,
                     ffn_conv_w=ffn_conv_w, ffn_conv_b=ffn_conv_b, lb=lb_all)
    w_in_t = jnp.swapaxes(w_in.astype(F32), 1, 2)
    ffn_up = ffn_up.astype(F32)
    ffn_gate = ffn_gate.astype(F32)
    st_p = dict(shift=jnp.zeros((bp, A_PROJ), F32), rwkv=jnp.zeros((bp, H_A, HS_A, HS_A), F32),
                gla=jnp.zeros((1, bp, H_B, DK_B, DV_B), F32), hgrn=jnp.zeros((1, bp, H_C, HD_C, HD_C), F32))
    conv_p = jnp.zeros((bp, CONV_W - 1, D_FF), F32)
    rwkv_s0 = jnp.transpose(state_rwkv.astype(F32), (0, 2, 3, 4, 1))
    for l in range(DEPTH):
        lw = {n: arr[l] for n, arr in per_layer.items()}
        for n, arr in (("p_rwkv", p_rwkv), ("p_gla", p_gla), ("p_hgrn", p_hgrn), ("w_out", w_out),
                       ("ffn_down", ffn_down)):
            lw[n] = arr[l].astype(BF16)
        st_s = dict(shift=state_rwkv_shift[l].astype(F32), rwkv=rwkv_s0,
                    gla=state_gla.astype(F32), hgrn=state_hgrn.astype(F32))

        hp = rmsnorm(xp, lw["norm_mix_pre"], tm=NORM_TILE, out_dtype=BF16)
        hs = rmsnorm(xs, lw["norm_mix_pre"], tm=NORM_TILE, out_dtype=BF16)
        proj_p, proj_s = proj_pair(hp, hs, w_in_t, l, tm=1024)
        xp, out_p = _token_mix(xp, proj_p, bp, tp, True, st_p, 0, lw)
        xs, out_s = _token_mix(xs, proj_s, bs, ts, False, st_s, l, lw)

        hp = rmsnorm(xp, lw["norm_ffn_pre"], tm=NORM_TILE, out_dtype=BF16)
        hs = rmsnorm(xs, lw["norm_ffn_pre"], tm=NORM_TILE, out_dtype=BF16)
        act_p, out_p["conv"], act_s, out_s["conv"] = ffn_up_pair(
            hp, hs, ffn_up, ffn_gate, l, conv_p, state_ffn_conv.astype(F32), lw["ffn_conv_w"], lw["ffn_conv_b"],
            bp, tp, bs, ts, tm=FFN_TM, tn=FFN_TN, sub=FFN_SUB)
        fp = matmul(act_p, lw["ffn_down"], tm=DOWN_TM, tn=512, name="ffn_down")
        fs = matmul(act_s, lw["ffn_down"], tm=DOWN_TM, tn=512, name="ffn_down")
        xp = add_rmsnorm(xp, fp, lw["norm_ffn_post"], tm=NORM_TILE)
        xs = add_rmsnorm(xs, fs, lw["norm_ffn_post"], tm=NORM_TILE)
        for n in names:
            new_p[n].append(out_p[n])
            new_s[n].append(out_s[n])

    stack = lambda seq_: jnp.stack(seq_).astype(sdt)
    out_s = {n: stack(new_s[n]) for n in names}
    out_s["rwkv"] = jnp.transpose(out_s["rwkv"], (0, 4, 1, 2, 3))
    return (xp.reshape(bp, tp, D_MODEL), xs.reshape(bs, ts, D_MODEL),
            *(stack(new_p[n]) for n in names), *(out_s[n] for n in names))
```

```python
import functools

import jax
import jax.numpy as jnp
from jax import lax
from jax.experimental import pallas as pl
from jax.experimental.pallas import tpu as pltpu

F32 = jnp.float32
BF16 = jnp.bfloat16

D_MODEL = 4096
DEPTH = 2
HS_A = 64
W_A = 3 * D_MODEL // 8
H_A = W_A // HS_A
R_W, R_A, R_G = 128, 128, 256
A_PROJ = 3 * W_A + R_W + R_A + R_G
RWKV_LN_EPS = 64e-5
W_B = 5 * D_MODEL // 16
DV_B = 256
H_B = W_B // DV_B
DK_B = DV_B // 2
K_B = H_B * DK_B
R_GLA = 16
GLA_TAU = 16.0
W_C = 5 * D_MODEL // 16
HD_C = 128
H_C = W_C // HD_C
D_FF = 2 * D_MODEL
CONV_W = 3
RMS_EPS = 1e-6
HEAD_NORM_EPS = 1e-5

LANES = 128
SUBLANES = 8
VMEM_LIMIT_BYTES = 56 * 1024 * 1024

IN_COLS = A_PROJ + 2 * K_B + 2 * W_B + R_GLA + 4 * W_C + 3 * D_MODEL
COL_BQ = A_PROJ
COL_BK = COL_BQ + K_B
COL_BV = COL_BK + K_B
COL_BA = COL_BV + W_B
TAIL_SRC = COL_BA + R_GLA
PROJ_TN = 512
HEAD_TILES = -(-TAIL_SRC // PROJ_TN)
COL_BG = 7 * W_B
PROJ_COLS = COL_BG + IN_COLS - TAIL_SRC
TAIL_TILES = -(-(IN_COLS - TAIL_SRC) // PROJ_TN)
TAIL_TILE0 = PROJ_COLS // PROJ_TN - TAIL_TILES
TAIL_ROW0 = IN_COLS - TAIL_TILES * PROJ_TN
COL_C = COL_BG + W_B
COL_G = COL_C + 4 * W_C
assert PROJ_COLS % PROJ_TN == 0 and TAIL_TILE0 >= HEAD_TILES and TAIL_ROW0 % SUBLANES == 0
assert COL_BG % W_B == 0 and COL_C % W_C == 0 and COL_BA % LANES == 0


def _cparams(semantics):
    return pltpu.CompilerParams(dimension_semantics=semantics, vmem_limit_bytes=VMEM_LIMIT_BYTES)


def _rmsnorm_kernel(x_ref, g_ref, o_ref):
    x = x_ref[...]
    ms = jnp.mean(x * x, axis=-1, keepdims=True)
    o_ref[...] = (x * lax.rsqrt(ms + RMS_EPS) * g_ref[...]).astype(o_ref.dtype)


def rmsnorm(x, g, *, tm, out_dtype):
    m, d = x.shape
    return pl.pallas_call(
        _rmsnorm_kernel,
        grid=(m // tm,),
        in_specs=[pl.BlockSpec((tm, d), lambda i: (i, 0)), pl.BlockSpec((1, d), lambda i: (0, 0))],
        out_specs=pl.BlockSpec((tm, d), lambda i: (i, 0)),
        out_shape=jax.ShapeDtypeStruct((m, d), out_dtype),
        compiler_params=_cparams(("parallel",)),
        name="rmsnorm",
    )(x, g.reshape(1, d))


def _add_rmsnorm_kernel(x_ref, y_ref, g_ref, o_ref):
    y = y_ref[...]
    ms = jnp.mean(y * y, axis=-1, keepdims=True)
    o_ref[...] = x_ref[...] + y * lax.rsqrt(ms + RMS_EPS) * g_ref[...]


def add_rmsnorm(x, y, g, *, tm):
    m, d = x.shape
    return pl.pallas_call(
        _add_rmsnorm_kernel,
        grid=(m // tm,),
        in_specs=[pl.BlockSpec((tm, d), lambda i: (i, 0)), pl.BlockSpec((tm, d), lambda i: (i, 0)),
                  pl.BlockSpec((1, d), lambda i: (0, 0))],
        out_specs=pl.BlockSpec((tm, d), lambda i: (i, 0)),
        out_shape=jax.ShapeDtypeStruct((m, d), F32),
        compiler_params=_cparams(("parallel",)),
        name="add_rmsnorm",
    )(x, y, g.reshape(1, d))


def _matmul_kernel(x_ref, w_ref, o_ref):
    o_ref[...] = jnp.dot(x_ref[...], w_ref[...], preferred_element_type=F32).astype(o_ref.dtype)


def matmul(x, w, *, tm, tn, out_dtype=F32, name="matmul"):
    m, k = x.shape
    n = w.shape[1]
    return pl.pallas_call(
        _matmul_kernel,
        grid=(m // tm, n // tn),
        in_specs=[pl.BlockSpec((tm, k), lambda i, j: (i, 0)), pl.BlockSpec((k, tn), lambda i, j: (0, j))],
        out_specs=pl.BlockSpec((tm, tn), lambda i, j: (i, j)),
        out_shape=jax.ShapeDtypeStruct((m, n), out_dtype),
        compiler_params=_cparams(("parallel", "parallel")),
        name=name,
    )(x, w)


def _proj_pair_kernel(xp_ref, xs_ref, wt_ref, op_ref, os_ref, w_bf, *, prompt_tiles):
    i = pl.program_id(1)

    @pl.when(i == 0)
    def _():
        w_bf[...] = wt_ref[...].astype(BF16)

    contract_last = (((1,), (1,)), ((), ()))

    @pl.when(i < prompt_tiles)
    def _():
        op_ref[...] = lax.dot_general(xp_ref[...], w_bf[...], contract_last, preferred_element_type=F32)

    @pl.when(i == prompt_tiles)
    def _():
        os_ref[...] = lax.dot_general(xs_ref[...], w_bf[...], contract_last, preferred_element_type=F32)


def proj_pair(xp, xs, w_in_t, layer, *, tm):
    mp, k = xp.shape
    ms = xs.shape[0]
    pt = mp // tm
    prow = lambda i: jnp.minimum(i, pt - 1)
    out_col = lambda j: jnp.where(j < HEAD_TILES, j, j + (TAIL_TILE0 - HEAD_TILES))
    w_row = lambda j: pl.multiple_of(
        jnp.where(j < HEAD_TILES, j * PROJ_TN, TAIL_ROW0 + (j - HEAD_TILES) * PROJ_TN), SUBLANES)
    return pl.pallas_call(
        functools.partial(_proj_pair_kernel, prompt_tiles=pt),
        grid=(HEAD_TILES + TAIL_TILES, pt + 1),
        in_specs=[pl.BlockSpec((tm, k), lambda j, i: (prow(i), 0)),
                  pl.BlockSpec((ms, k), lambda j, i: (0, 0), pipeline_mode=pl.Buffered(1)),
                  pl.BlockSpec((None, pl.Element(PROJ_TN), pl.Element(k)), lambda j, i: (layer, w_row(j), 0))],
        out_specs=[pl.BlockSpec((tm, PROJ_TN), lambda j, i: (prow(i), out_col(j))),
                   pl.BlockSpec((ms, PROJ_TN), lambda j, i: (0, out_col(j)))],
        out_shape=[jax.ShapeDtypeStruct((mp, PROJ_COLS), F32), jax.ShapeDtypeStruct((ms, PROJ_COLS), F32)],
        scratch_shapes=[pltpu.VMEM((PROJ_TN, k), BF16)],
        compiler_params=_cparams(("arbitrary", "arbitrary")),
        name="proj_pair",
    )(xp, xs, w_in_t)


def _merge_kernel(y_ref, g_ref, ob_ref, oc_ref, pa_ref, pb_ref, pc_ref, ga_ref, gb_ref, gc_ref, o_ref):
    oa = (y_ref[...] * g_ref[...]).astype(BF16)
    ma = jnp.dot(oa, pa_ref[...], preferred_element_type=F32)
    mb = jnp.dot(ob_ref[...], pb_ref[...], preferred_element_type=F32)
    mc = jnp.dot(oc_ref[...], pc_ref[...], preferred_element_type=F32)
    merged = (jax.nn.sigmoid(ga_ref[...]) * ma + jax.nn.sigmoid(gb_ref[...]) * mb
              + jax.nn.sigmoid(gc_ref[...]) * mc)
    o_ref[...] = merged.astype(o_ref.dtype)


def merge(y_a, g_a, o_b, o_c, p_a, p_b, p_c, proj, *, tm, tn):
    m = y_a.shape[0]
    d = p_a.shape[1]
    gate0 = COL_G // tn
    gstep = d // tn

    def row(width):
        return pl.BlockSpec((tm, width), lambda i, j: (i, 0))

    def wcol(kdim):
        return pl.BlockSpec((kdim, tn), lambda i, j: (0, j))

    def gate(branch):
        return pl.BlockSpec((tm, tn), lambda i, j: (i, gate0 + branch * gstep + j))

    return pl.pallas_call(
        _merge_kernel,
        grid=(m // tm, d // tn),
        in_specs=[row(W_A), row(W_A), row(W_B), row(W_C), wcol(W_A), wcol(W_B), wcol(W_C),
                  gate(0), gate(1), gate(2)],
        out_specs=pl.BlockSpec((tm, tn), lambda i, j: (i, j)),
        out_shape=jax.ShapeDtypeStruct((m, d), BF16),
        compiler_params=_cparams(("parallel", "parallel")),
        name="merge",
    )(y_a, g_a, o_b, o_c, p_a, p_b, p_c, proj, proj, proj)


def _softplus(x):
    return jnp.maximum(x, 0.0) + jnp.log(1.0 + jnp.exp(-jnp.abs(x)))


def _rwkv_gates(p, prev, mu_ref, w0_ref, w2_ref, a0_ref, a2_ref, g2_ref, r_ref, d_ref, k_ref, v_ref, a_ref, g_ref):
    ps = p + mu_ref[...] * (prev - p)
    o1 = 3 * W_A
    r_ref[...] = ps[:, :W_A]
    k_ref[...] = ps[:, W_A:2 * W_A]
    v_ref[...] = ps[:, 2 * W_A:o1]
    wc = ps[:, o1:o1 + R_W]
    ac = ps[:, o1 + R_W:o1 + R_W + R_A]
    gc = ps[:, o1 + R_W + R_A:]
    wl = w0_ref[...] + jnp.dot(jnp.tanh(wc).astype(BF16), w2_ref[...], preferred_element_type=F32)
    w = -_softplus(-wl) - 0.5
    d_ref[...] = jnp.exp(-jnp.exp(w))
    al = a0_ref[...] + jnp.dot(ac.astype(BF16), a2_ref[...], preferred_element_type=F32)
    a_ref[...] = jax.nn.sigmoid(al)
    g_ref[...] = jnp.dot(jax.nn.sigmoid(gc).astype(BF16), g2_ref[...], preferred_element_type=F32)


def _rwkv_prep_seq_kernel(p_ref, tail_ref, shift_ref, *rest, tiles_per_seq):
    i = pl.program_id(0)
    p = p_ref[...]
    first = (i % tiles_per_seq) == 0
    before = jnp.where(first, shift_ref[0], tail_ref[SUBLANES - 1:SUBLANES, :])
    row = lax.broadcasted_iota(jnp.int32, (p.shape[0], 1), 0)
    prev = jnp.where(row >= 1, pltpu.roll(p, 1, 0), before)
    _rwkv_gates(p, prev, *rest)


def _rwkv_prep_rows_kernel(p_ref, prev_ref, *rest):
    _rwkv_gates(p_ref[...], prev_ref[...], *rest)


def _rwkv_prep_call(kernel, lead_specs, lead_args, m, tm, mu, w0, w2, a0, a2, g2, name):
    full = lambda shape: pl.BlockSpec(shape, lambda i: (0, 0))
    out = jax.ShapeDtypeStruct((m, W_A), F32)
    return pl.pallas_call(
        kernel,
        grid=(m // tm,),
        in_specs=lead_specs + [full((1, A_PROJ)), full((1, W_A)), full((R_W, W_A)), full((1, W_A)),
                               full((R_A, W_A)), full((R_G, W_A))],
        out_specs=[pl.BlockSpec((tm, W_A), lambda i: (i, 0))] * 6,
        out_shape=[out] * 6,
        compiler_params=_cparams(("parallel",)),
        name=name,
    )(*lead_args, mu.reshape(1, A_PROJ), w0.reshape(1, W_A), w2.astype(BF16), a0.reshape(1, W_A),
      a2.astype(BF16), g2.astype(BF16))


def rwkv_prep_seq(proj, shift, weights, batch, seq, *, tm):
    tiles_per_seq = seq // tm
    nblk = tm // SUBLANES
    specs = [pl.BlockSpec((tm, A_PROJ), lambda i: (i, 0)),
             pl.BlockSpec((SUBLANES, A_PROJ), lambda i: (jnp.maximum(i * nblk - 1, 0), 0)),
             pl.BlockSpec((1, 1, A_PROJ), lambda i: (i // tiles_per_seq, 0, 0))]
    kern = functools.partial(_rwkv_prep_seq_kernel, tiles_per_seq=tiles_per_seq)
    return _rwkv_prep_call(kern, specs, (proj, proj, shift.reshape(batch, 1, A_PROJ)), batch * seq, tm,
                           *weights, name="rwkv_prep_seq")


def rwkv_prep_rows(proj, prev, weights, *, tm):
    specs = [pl.BlockSpec((tm, A_PROJ), lambda i: (i, 0)), pl.BlockSpec((tm, A_PROJ), lambda i: (i, 0))]
    return _rwkv_prep_call(_rwkv_prep_rows_kernel, specs, (proj, prev), proj.shape[0], tm, *weights,
                           name="rwkv_prep_rows")


def _rwkv_scan_kernel(r_ref, d_ref, k_ref, v_ref, a_ref, s0_ref, kk_ref, ka_ref, rk_ref, lng_ref, lnb_ref,
                      y_ref, st_ref, s_scr, o_scr, *, steps, tsub):
    tc = pl.program_id(1)

    @pl.when(tc == 0)
    def _():
        s_scr[...] = s0_ref[...]

    row = lax.broadcasted_iota(jnp.int32, (SUBLANES, 1), 0)

    def channel_rows(t, c0, n):
        tile, pos = t // tsub, lax.rem(t, tsub)
        return pl.ds((tile * HS_A + c0) * tsub + pos, n, stride=tsub)

    def step(t, carry):
        chans = channel_rows(t, 0, HS_A)
        r = r_ref[chans, :]
        d = d_ref[chans, :]
        k = k_ref[chans, :]
        a = a_ref[chans, :]
        kk = k * kk_ref[...]
        kk = kk * lax.rsqrt(jnp.maximum(jnp.sum(kk * kk, axis=0, keepdims=True), 1e-24))
        kp = k * (1.0 + (a - 1.0) * ka_ref[...])
        kka = kk * a

        def vblock(vo, c):
            v8 = v_ref[channel_rows(t, vo * SUBLANES, SUBLANES), :]
            o8 = jnp.zeros_like(v8)
            for vi in range(SUBLANES):
                s = s_scr[vo, vi]
                skk = jnp.sum(s * kk, axis=0, keepdims=True)
                s = s * d - skk * kka + v8[vi:vi + 1, :] * kp
                s_scr[vo, vi] = s
                o8 = jnp.where(row == vi, jnp.sum(s * r, axis=0, keepdims=True), o8)
            o_scr[pl.ds(vo * SUBLANES, SUBLANES), :] = o8
            return c

        lax.fori_loop(0, HS_A // SUBLANES, vblock, 0)
        o = o_scr[...]
        v = v_ref[chans, :]
        mean = jnp.mean(o, axis=0, keepdims=True)
        var = jnp.mean(jnp.square(o - mean), axis=0, keepdims=True)
        y = (o - mean) * lax.rsqrt(var + RWKV_LN_EPS) * lng_ref[...] + lnb_ref[...]
        bonus = jnp.sum(r * kp * rk_ref[...], axis=0, keepdims=True) * v
        y_ref[chans, :] = y + bonus
        return carry

    lax.fori_loop(0, steps, step, 0)

    @pl.when(tc == pl.num_programs(1) - 1)
    def _():
        st_ref[...] = s_scr[...]


def rwkv_scan(r, d, k, v, a, s0, layer, kk_p, ka_p, rk_p, lng_p, lnb_p, *, steps, tsub):
    rows, lanes = r.shape
    c = HS_A
    nl = lanes // LANES
    vo = c // SUBLANES
    s0 = s0.reshape(s0.shape[0], nl, vo, SUBLANES, c, LANES)
    tok = pl.BlockSpec((steps * c, LANES), lambda l, i: (i, l))
    par = pl.BlockSpec((c, LANES), lambda l, i: (0, l))
    st_in = pl.BlockSpec((None, None, vo, SUBLANES, c, LANES), lambda l, i: (layer, l, 0, 0, 0, 0))
    st_out = pl.BlockSpec((None, vo, SUBLANES, c, LANES), lambda l, i: (l, 0, 0, 0, 0))
    y, s_t = pl.pallas_call(
        functools.partial(_rwkv_scan_kernel, steps=steps, tsub=tsub),
        grid=(nl, rows // (steps * c)),
        in_specs=[tok] * 5 + [st_in] + [par] * 5,
        out_specs=[tok, st_out],
        out_shape=[jax.ShapeDtypeStruct((rows, lanes), F32),
                   jax.ShapeDtypeStruct((nl, vo, SUBLANES, c, LANES), F32)],
        scratch_shapes=[pltpu.VMEM((vo, SUBLANES, c, LANES), F32), pltpu.VMEM((c, LANES), F32)],
        compiler_params=_cparams(("parallel", "arbitrary")),
        name="rwkv_scan",
    )(r, d, k, v, a, s0, kk_p, ka_p, rk_p, lng_p, lnb_p)
    return y, s_t.reshape(nl, c, c, LANES)


LANE_SLOTS = 32
RELAYOUT_STEPS = 128


def _to_lane_tiles_kernel(x_ref, o_ref, xt_scr):
    nb, tt, _ = x_ref.shape
    for b in range(nb):
        xt_scr[b] = x_ref[b].T
    idle = jnp.zeros((LANE_SLOTS - H_A, tt), F32)
    for c in range(HS_A):
        parts = []
        for b in range(nb):
            parts += [xt_scr[b, pl.ds(c, H_A, stride=HS_A), :], idle]
        tile = jnp.concatenate(parts, axis=0).T
        o_ref[:, c, :, :] = tile.reshape(tt // SUBLANES, SUBLANES, LANES)


def _from_lane_tiles_kernel(y_ref, o_ref, xt_scr):
    nb, tt, _ = o_ref.shape
    for c in range(HS_A):
        heads = y_ref[:, c, :, :].reshape(tt, LANES).T
        for b in range(nb):
            xt_scr[b, pl.ds(c, H_A, stride=HS_A), :] = heads[b * LANE_SLOTS:b * LANE_SLOTS + H_A, :]
    for b in range(nb):
        o_ref[b] = xt_scr[b].T


def to_lane_tiles(x, batch, seq):
    tt = RELAYOUT_STEPS
    out = pl.pallas_call(
        _to_lane_tiles_kernel,
        grid=(seq // tt,),
        in_specs=[pl.BlockSpec((batch, tt, W_A), lambda i: (0, i, 0))],
        out_specs=pl.BlockSpec((tt // SUBLANES, HS_A, SUBLANES, LANES), lambda i: (i, 0, 0, 0)),
        out_shape=jax.ShapeDtypeStruct((seq // SUBLANES, HS_A, SUBLANES, LANES), F32),
        scratch_shapes=[pltpu.VMEM((batch, W_A, tt), F32)],
        compiler_params=_cparams(("parallel",)),
        name="to_lane_tiles",
    )(x.reshape(batch, seq, W_A))
    return out.reshape(seq * HS_A, LANES)


def from_lane_tiles(y, batch, seq):
    tt = RELAYOUT_STEPS
    out = pl.pallas_call(
        _from_lane_tiles_kernel,
        grid=(seq // tt,),
        in_specs=[pl.BlockSpec((tt // SUBLANES, HS_A, SUBLANES, LANES), lambda i: (i, 0, 0, 0))],
        out_specs=pl.BlockSpec((batch, tt, W_A), lambda i: (0, i, 0)),
        out_shape=jax.ShapeDtypeStruct((batch, seq, W_A), F32),
        scratch_shapes=[pltpu.VMEM((batch, W_A, tt), F32)],
        compiler_params=_cparams(("parallel",)),
        name="from_lane_tiles",
    )(y.reshape(seq // SUBLANES, HS_A, SUBLANES, LANES))
    return out.reshape(batch * seq, W_A)


def rwkv_recurrence_long(r, d, k, v, a, s0, params, batch, seq, *, steps):
    idle = LANE_SLOTS - H_A

    def param_tile(p):
        return jnp.tile(jnp.pad(p.reshape(H_A, HS_A).T, ((0, 0), (0, idle))), (1, batch))

    s0l = jnp.pad(jnp.transpose(s0, (2, 3, 0, 1)), ((0, 0), (0, 0), (0, 0), (0, idle)))
    s0l = s0l.reshape(1, 1, HS_A, HS_A, LANES)
    y, s_t = rwkv_scan(*(to_lane_tiles(z, batch, seq) for z in (r, d, k, v, a)), s0l, 0,
                       *(param_tile(p) for p in params), steps=steps, tsub=SUBLANES)
    s_t = s_t.reshape(HS_A, HS_A, batch, LANE_SLOTS)[..., :H_A]
    return from_lane_tiles(y, batch, seq), jnp.transpose(s_t, (2, 3, 0, 1))


def rwkv_recurrence_short(r, d, k, v, a, s0, layer, params, batch, seq):
    def lanes_of(z):
        z = jnp.transpose(z.reshape(batch, seq, H_A, HS_A), (3, 1, 2, 0))
        return z.reshape(HS_A * seq, H_A * batch)

    def param_tile(p):
        return jnp.repeat(p.reshape(H_A, HS_A).T, batch, axis=1)

    y, s_t = rwkv_scan(*(lanes_of(z) for z in (r, d, k, v, a)), s0, layer, *(param_tile(p) for p in params),
                       steps=seq, tsub=seq)
    y = jnp.transpose(y.reshape(HS_A, seq, H_A, batch), (3, 1, 2, 0)).reshape(batch * seq, W_A)
    return y, s_t


def _block_prefix_sum(x):
    row = lax.broadcasted_iota(jnp.int32, (SUBLANES, 1), 0)
    for shift in (1, 2, 4):
        x = x + jnp.where(row >= shift, pltpu.roll(x, shift, 0), 0.0)
    return x


def _chunk_scores(q, k, blocks):
    c = q.shape[0]
    nb = c // SUBLANES
    row = lax.broadcasted_iota(jnp.int32, (SUBLANES, 1), 0)
    lane = lax.broadcasted_iota(jnp.int32, (SUBLANES, c), 1)
    rows = []
    for ib in range(nb):
        lo = ib * SUBLANES
        qi = q[lo:lo + SUBLANES, :]
        ki = k[lo:lo + SUBLANES, :]
        bi = blocks[ib]
        att = jnp.zeros((SUBLANES, c), F32)
        for j in range(SUBLANES):
            decay = jnp.exp(jnp.minimum(bi - bi[j:j + 1, :], 0.0))
            col = jnp.sum(qi * ki[j:j + 1, :] * decay, axis=-1, keepdims=True)
            att = jnp.where(lane == lo + j, col, att)
        att = jnp.where(lane <= lo + row, att, 0.0)
        if ib > 0:
            ref = blocks[ib - 1][SUBLANES - 1:SUBLANES, :]
            qs = (qi * jnp.exp(bi - ref)).astype(BF16)
            ks = [k[jb * SUBLANES:(jb + 1) * SUBLANES, :] * jnp.exp(ref - blocks[jb]) for jb in range(ib)]
            ks.append(jnp.zeros((c - lo, q.shape[1]), F32))
            ks = jnp.concatenate(ks, axis=0).astype(BF16)
            att = att + lax.dot_general(qs, ks, (((1,), (1,)), ((), ())), preferred_element_type=F32)
        rows.append(att)
    return jnp.concatenate(rows, axis=0)


def _gated_chunk(q, k, v, log_a, s_prev):
    nb = q.shape[0] // SUBLANES
    blocks = []
    carry = None
    for ib in range(nb):
        bi = _block_prefix_sum(log_a[ib * SUBLANES:(ib + 1) * SUBLANES, :])
        if carry is not None:
            bi = bi + carry
        carry = bi[SUBLANES - 1:SUBLANES, :]
        blocks.append(bi)
    b = jnp.concatenate(blocks, axis=0)
    b_last = carry
    att = _chunk_scores(q, k, blocks)
    v_bf = v.astype(BF16)
    o = jnp.dot(att.astype(BF16), v_bf, preferred_element_type=F32)
    qe = (q * jnp.exp(b)).astype(BF16)
    o = o + jnp.dot(qe, s_prev.astype(BF16), preferred_element_type=F32)
    kd = (k * jnp.exp(b_last - b)).astype(BF16)
    e_col = jnp.transpose(jnp.broadcast_to(jnp.exp(b_last), (SUBLANES, q.shape[1])))[:, 0:1]
    s_new = s_prev * e_col + lax.dot_general(kd, v_bf, (((0,), (0,)), ((), ())), preferred_element_type=F32)
    return o, s_new


def _head_rms(o, g):
    return o * lax.rsqrt(jnp.mean(o * o, axis=-1, keepdims=True) + HEAD_NORM_EPS) * g


def _mask_padding(log_decay, valid):
    if valid >= log_decay.shape[0]:
        return log_decay
    row = lax.broadcasted_iota(jnp.int32, (log_decay.shape[0], 1), 0)
    return jnp.where(row < valid, log_decay, 0.0)


def _gla_kernel(q_ref, k_ref, v_ref, rg_ref, ac_ref, a2_ref, ab_ref, ng_ref, s0_ref, o_ref, st_ref, *, valid):
    @pl.when(pl.program_id(1) == 0)
    def _():
        st_ref[...] = s0_ref[...]

    gl = jnp.dot(ac_ref[...].astype(BF16), a2_ref[...], preferred_element_type=F32) + ab_ref[...]
    log_a = _mask_padding(-_softplus(-gl) / GLA_TAU, valid)
    q = q_ref[...] * (DK_B ** -0.5)
    k = k_ref[...]
    v = v_ref[...]
    rg = rg_ref[...]
    for h in range(H_B):
        ks = slice(h * DK_B, (h + 1) * DK_B)
        vs = slice(h * DV_B, (h + 1) * DV_B)
        o, s_new = _gated_chunk(q[:, ks], k[:, ks], v[:, vs], log_a[:, ks], st_ref[0, h])
        st_ref[0, h] = s_new
        o_ref[:, vs] = (_head_rms(o, ng_ref[...]) * jax.nn.silu(rg[:, vs])).astype(o_ref.dtype)


def _hgrn_kernel(q_ref, z_ref, i_ref, g_ref, lb_ref, ng_ref, s0_ref, o_ref, st_ref, *, valid):
    @pl.when(pl.program_id(1) == 0)
    def _():
        st_ref[...] = s0_ref[...]

    q = jax.nn.silu(q_ref[...])
    lb = lb_ref[...]
    f = lb + (1.0 - lb) * jax.nn.sigmoid(z_ref[...])
    log_f = _mask_padding(jnp.log(jnp.maximum(f, 1e-30)), valid)
    k = 1.0 - f
    v = i_ref[...]
    g = g_ref[...]
    for h in range(H_C):
        hs = slice(h * HD_C, (h + 1) * HD_C)
        o, s_new = _gated_chunk(q[:, hs], k[:, hs], v[:, hs], log_f[:, hs], st_ref[0, h])
        st_ref[0, h] = s_new
        o_ref[:, hs] = (_head_rms(o, ng_ref[...]) * jax.nn.sigmoid(g[:, hs])).astype(o_ref.dtype)


def _seq_spec(chunk, width, col_block, nchunks):
    return pl.BlockSpec((chunk, width), lambda b, c: (b * nchunks + c, col_block))


def _const_spec(shape):
    return pl.BlockSpec(shape, lambda b, c: (0,) * len(shape))


def _state_in_spec(layer, tail):
    return pl.BlockSpec((None, 1) + tail, lambda b, c: (layer, b) + (0,) * len(tail))


def gla_branch(src, cols, a2, a_bias, norm_g, s0, layer, batch, seq, *, chunk, valid=None):
    nch = seq // chunk
    valid = chunk if valid is None else valid
    cq, ck, cv, cg, ca = cols
    a2p = jnp.zeros((LANES, K_B), F32).at[:R_GLA].set(a2).astype(BF16)
    state = pl.BlockSpec((1, H_B, DK_B, DV_B), lambda b, c: (b, 0, 0, 0))
    return pl.pallas_call(
        functools.partial(_gla_kernel, valid=valid),
        grid=(batch, nch),
        in_specs=[_seq_spec(chunk, K_B, cq // K_B, nch), _seq_spec(chunk, K_B, ck // K_B, nch),
                  _seq_spec(chunk, W_B, cv // W_B, nch), _seq_spec(chunk, W_B, cg // W_B, nch),
                  _seq_spec(chunk, LANES, ca // LANES, nch),
                  _const_spec((LANES, K_B)), _const_spec((1, K_B)), _const_spec((1, DV_B)),
                  _state_in_spec(layer, (H_B, DK_B, DV_B))],
        out_specs=[pl.BlockSpec((chunk, W_B), lambda b, c: (b * nch + c, 0)), state],
        out_shape=[jax.ShapeDtypeStruct((batch * seq, W_B), BF16),
                   jax.ShapeDtypeStruct((batch, H_B, DK_B, DV_B), F32)],
        compiler_params=_cparams(("parallel", "arbitrary")),
        name="gla",
    )(src, src, src, src, src, a2p, a_bias.reshape(1, K_B), norm_g.reshape(1, DV_B), s0)


def hgrn_branch(src, col0, lb, norm_g, s0, layer, batch, seq, *, chunk, valid=None):
    nch = seq // chunk
    valid = chunk if valid is None else valid
    c0 = col0 // W_C
    state = pl.BlockSpec((1, H_C, HD_C, HD_C), lambda b, c: (b, 0, 0, 0))
    return pl.pallas_call(
        functools.partial(_hgrn_kernel, valid=valid),
        grid=(batch, nch),
        in_specs=[_seq_spec(chunk, W_C, c0 + j, nch) for j in range(4)]
        + [_const_spec((1, W_C)), _const_spec((1, HD_C)), _state_in_spec(layer, (H_C, HD_C, HD_C))],
        out_specs=[pl.BlockSpec((chunk, W_C), lambda b, c: (b * nch + c, 0)), state],
        out_shape=[jax.ShapeDtypeStruct((batch * seq, W_C), BF16),
                   jax.ShapeDtypeStruct((batch, H_C, HD_C, HD_C), F32)],
        compiler_params=_cparams(("parallel", "arbitrary")),
        name="hgrn",
    )(src, src, src, src, lb.reshape(1, W_C), norm_g.reshape(1, HD_C), s0)


def _conv_gelu_gate(u, u1, u2, gate, cw_ref, cb_ref):
    c = cb_ref[...] + u2 * cw_ref[0:1, :]
    c = c + u1 * cw_ref[1:2, :]
    c = c + u * cw_ref[2:3, :]
    return (jax.nn.gelu(c, approximate=True) * gate).astype(BF16)


def _select_rows(sel, x):
    return jnp.dot(sel, x, precision=lax.Precision.HIGHEST, preferred_element_type=F32)


def _ffn_up_pair_kernel(xp_ref, xs_ref, wu_ref, wg_ref, stp_ref, sts_ref, cw_ref, cb_ref,
                        actp_ref, nsp_ref, acts_ref, nss_ref, wu_bf, wg_bf, carry,
                        *, prompt_tiles, tiles_per_seq, sub, short_seq):
    i = pl.program_id(1)

    @pl.when(i == 0)
    def _():
        wu_bf[...] = wu_ref[...].astype(BF16)
        wg_bf[...] = wg_ref[...].astype(BF16)

    @pl.when(i < prompt_tiles)
    def _():
        first = (i % tiles_per_seq) == 0
        pm2 = jnp.where(first, stp_ref[0, 0:1, :], carry[0:1, :])
        pm1 = jnp.where(first, stp_ref[0, 1:2, :], carry[1:2, :])
        row = lax.broadcasted_iota(jnp.int32, (sub, 1), 0)
        for s in range(xp_ref.shape[0] // sub):
            rows = slice(s * sub, (s + 1) * sub)
            x = xp_ref[rows, :]
            u = jnp.dot(x, wu_bf[...], preferred_element_type=F32)
            gate = jnp.dot(x, wg_bf[...], preferred_element_type=F32)
            u1 = jnp.where(row >= 1, pltpu.roll(u, 1, 0), pm1)
            u2 = jnp.where(row >= 2, pltpu.roll(u, 2, 0), jnp.where(row == 1, pm1, pm2))
            actp_ref[rows, :] = _conv_gelu_gate(u, u1, u2, gate, cw_ref, cb_ref)
            pm2 = u[sub - 2:sub - 1, :]
            pm1 = u[sub - 1:sub, :]
        carry[0:1, :] = pm2
        carry[1:2, :] = pm1
        nsp_ref[0, 0:1, :] = pm2
        nsp_ref[0, 1:2, :] = pm1

    @pl.when(i == prompt_tiles)
    def _():
        x = xs_ref[...]
        u = jnp.dot(x, wu_bf[...], preferred_element_type=F32)
        gate = jnp.dot(x, wg_bf[...], preferred_element_type=F32)
        m = u.shape[0]
        nseq = m // short_seq
        r = lax.broadcasted_iota(jnp.int32, (m, nseq), 0)
        start = lax.broadcasted_iota(jnp.int32, (m, nseq), 1) * short_seq
        at0 = (r == start).astype(F32)
        at1 = (r == start + 1).astype(F32)
        st0 = sts_ref[:, 0, :]
        st1 = sts_ref[:, 1, :]
        tap1 = _select_rows(at0, st1)
        tap2 = _select_rows(at0, st0) + _select_rows(at1, st1)
        pos = lax.rem(lax.broadcasted_iota(jnp.int32, (m, 1), 0), short_seq)
        u1 = jnp.where(pos >= 1, pltpu.roll(u, 1, 0), tap1)
        u2 = jnp.where(pos >= 2, pltpu.roll(u, 2, 0), tap2)
        acts_ref[...] = _conv_gelu_gate(u, u1, u2, gate, cw_ref, cb_ref)
        rr = lax.broadcasted_iota(jnp.int32, (nseq, m), 1)
        end = lax.broadcasted_iota(jnp.int32, (nseq, m), 0) * short_seq + short_seq
        nss_ref[:, 0, :] = _select_rows((rr == end - 2).astype(F32), u)
        nss_ref[:, 1, :] = _select_rows((rr == end - 1).astype(F32), u)


def ffn_up_pair(xp, xs, w_up, w_gate, layer, conv_p, conv_s, conv_w, conv_b, bp, tp, bs, ts, *, tm, tn, sub):
    mp, k = xp.shape
    ms = xs.shape[0]
    pt = mp // tm
    tiles_per_seq = tp // tm
    prow = lambda i: jnp.minimum(i, pt - 1)
    wspec = pl.BlockSpec((None, k, tn), lambda j, i: (layer, 0, j))
    pstate = pl.BlockSpec((1, CONV_W - 1, tn), lambda j, i: (prow(i) // tiles_per_seq, 0, j))
    kern = functools.partial(_ffn_up_pair_kernel, prompt_tiles=pt, tiles_per_seq=tiles_per_seq, sub=sub,
                             short_seq=ts)
    return pl.pallas_call(
        kern,
        grid=(D_FF // tn, pt + 1),
        in_specs=[pl.BlockSpec((tm, k), lambda j, i: (prow(i), 0)),
                  pl.BlockSpec((ms, k), lambda j, i: (0, 0), pipeline_mode=pl.Buffered(1)),
                  wspec, wspec, pstate,
                  pl.BlockSpec((None, bs, CONV_W - 1, tn), lambda j, i: (layer, 0, 0, j)),
                  pl.BlockSpec((CONV_W, tn), lambda j, i: (0, j)), pl.BlockSpec((1, tn), lambda j, i: (0, j))],
        out_specs=[pl.BlockSpec((tm, tn), lambda j, i: (prow(i), j)), pstate,
                   pl.BlockSpec((ms, tn), lambda j, i: (0, j)),
                   pl.BlockSpec((bs, CONV_W - 1, tn), lambda j, i: (0, 0, j))],
        out_shape=[jax.ShapeDtypeStruct((mp, D_FF), BF16), jax.ShapeDtypeStruct((bp, CONV_W - 1, D_FF), F32),
                   jax.ShapeDtypeStruct((ms, D_FF), BF16), jax.ShapeDtypeStruct((bs, CONV_W - 1, D_FF), F32)],
        scratch_shapes=[pltpu.VMEM((k, tn), BF16), pltpu.VMEM((k, tn), BF16), pltpu.VMEM((SUBLANES, tn), F32)],
        compiler_params=_cparams(("arbitrary", "arbitrary")),
        name="ffn_up_pair",
    )(xp, xs, w_up, w_gate, conv_p, conv_s, conv_w, conv_b.reshape(1, D_FF))


NORM_TILE = 256
PREP_TILE = 128
MERGE_TM = 512
MERGE_TN = 256
FFN_TM = 1024
FFN_SUB = 512
FFN_TN = 256
DOWN_TM = 512
SCAN_STEPS = 16
SEQ_CHUNK = 64
DEC_PAD = SUBLANES


def _shifted_rows(x, first, seq):
    w = x.shape[-1]
    xs = x.reshape(-1, seq, w)
    return jnp.concatenate([first[:, None, :], xs[:, :-1]], axis=1).reshape(-1, w)


def _last_rows(x, batch, seq, back):
    return jnp.take(x, jnp.arange(batch) * seq + (seq - 1 - back), axis=0)


def _token_mix(x, proj, batch, seq, long_seq, st, layer, lw):
    m = batch * seq
    prep_w = (lw["rwkv_mu"], lw["rwkv_w0"], lw["rwkv_w2"], lw["rwkv_a0"], lw["rwkv_a2"], lw["rwkv_g2"])
    if long_seq:
        r, d, k, v, a, g = rwkv_prep_seq(proj, st["shift"], prep_w, batch, seq, tm=PREP_TILE)
    else:
        prev = _shifted_rows(proj[:, :A_PROJ], st["shift"], seq)
        r, d, k, v, a, g = rwkv_prep_rows(proj, prev, prep_w, tm=PREP_TILE)
    params = (lw["rwkv_k_k"], lw["rwkv_k_a"], lw["rwkv_r_k"], lw["rwkv_ln_g"], lw["rwkv_ln_b"])
    if long_seq:
        y_a, s_rwkv = rwkv_recurrence_long(r, d, k, v, a, st["rwkv"], params, batch, seq, steps=SCAN_STEPS)
    else:
        y_a, s_rwkv = rwkv_recurrence_short(r, d, k, v, a, st["rwkv"], layer, params, batch, seq)

    if long_seq:
        o_b, s_gla = gla_branch(proj, (COL_BQ, COL_BK, COL_BV, COL_BG, COL_BA), lw["gla_a2"], lw["gla_a_bias"],
                                lw["gla_norm_g"], st["gla"], layer, batch, seq, chunk=SEQ_CHUNK)
        o_c, s_hgrn = hgrn_branch(proj, COL_C, lw["lb"], lw["hgrn_norm_g"], st["hgrn"], layer, batch, seq,
                                  chunk=SEQ_CHUNK)
    else:
        dec = jnp.concatenate([proj[:, COL_BQ:COL_BA], proj[:, COL_BG:COL_G], proj[:, COL_BA:COL_BA + LANES]],
                              axis=1)
        dec = jnp.pad(dec.reshape(batch, seq, -1), ((0, 0), (0, DEC_PAD - seq), (0, 0)))
        dec = dec.reshape(batch * DEC_PAD, -1)
        gate_col = COL_BA - COL_BQ
        dec_cols = (0, COL_BK - COL_BQ, COL_BV - COL_BQ, gate_col, gate_col + COL_G - COL_BG)
        unpad = lambda o: o.reshape(batch, DEC_PAD, -1)[:, :seq].reshape(m, -1)
        o_b, s_gla = gla_branch(dec, dec_cols, lw["gla_a2"], lw["gla_a_bias"], lw["gla_norm_g"], st["gla"],
                                layer, batch, DEC_PAD, chunk=DEC_PAD, valid=seq)
        o_c, s_hgrn = hgrn_branch(dec, gate_col + W_B, lw["lb"], lw["hgrn_norm_g"], st["hgrn"], layer,
                                  batch, DEC_PAD, chunk=DEC_PAD, valid=seq)
        o_b, o_c = unpad(o_b), unpad(o_c)

    tm = 1024 if m % 1024 == 0 else m
    mtm = MERGE_TM if m % MERGE_TM == 0 else m
    merged = merge(y_a, g, o_b, o_c, lw["p_rwkv"], lw["p_gla"], lw["p_hgrn"], proj, tm=mtm, tn=MERGE_TN)
    mix = matmul(merged, lw["w_out"], tm=tm, tn=1024, name="w_out")
    x = add_rmsnorm(x, mix, lw["norm_mix_post"], tm=NORM_TILE)
    s_shift = _last_rows(proj, batch, seq, 0)[:, :A_PROJ]
    return x, dict(shift=s_shift, rwkv=s_rwkv, gla=s_gla, hgrn=s_hgrn)


def kernel(x_prompt, x_sample, state_rwkv_shift, state_rwkv, state_gla, state_hgrn, state_ffn_conv, norm_mix_pre, norm_mix_post, norm_ffn_pre, norm_ffn_post, w_in, rwkv_mu, rwkv_w0, rwkv_w2, rwkv_a0, rwkv_a2, rwkv_g2, rwkv_k_k, rwkv_k_a, rwkv_r_k, rwkv_ln_g, rwkv_ln_b, gla_a2, gla_a_bias, gla_norm_g, hgrn_lb_logits, hgrn_norm_g, p_rwkv, p_gla, p_hgrn, w_out, ffn_up, ffn_gate, ffn_conv_w, ffn_conv_b, ffn_down):
    bp, tp, _ = x_prompt.shape
    bs, ts, _ = x_sample.shape
    sdt = state_rwkv.dtype

    probs = jax.nn.softmax(hgrn_lb_logits.astype(F32), axis=0)
    lb_all = jnp.cumsum(probs, axis=0) - probs[0]

    xp = x_prompt.reshape(bp * tp, D_MODEL)
    xs = x_sample.reshape(bs * ts, D_MODEL)
    names = ("shift", "rwkv", "gla", "hgrn", "conv")
    new_p = {n: [] for n in names}
    new_s = {n: [] for n in names}
    per_layer = dict(norm_mix_pre=norm_mix_pre, norm_mix_post=norm_mix_post, norm_ffn_pre=norm_ffn_pre,
                     norm_ffn_post=norm_ffn_post, rwkv_mu=rwkv_mu, rwkv_w0=rwkv_w0, rwkv_w2=rwkv_w2,
                     rwkv_a0=rwkv_a0, rwkv_a2=rwkv_a2, rwkv_g2=rwkv_g2, rwkv_k_k=rwkv_k_k, rwkv_k_a=rwkv_k_a,
                     rwkv_r_k=rwkv_r_k, rwkv_ln_g=rwkv_ln_g, rwkv_ln_b=rwkv_ln_b, gla_a2=gla_a2,
                     gla_a_bias=gla_a_bias, gla_norm_g=gla_norm_g, hgrn_norm_g=hgrn_norm_g,
                     ffn_conv_w=ffn_conv_w, ffn_conv_b=ffn_conv_b, lb=lb_all)
    w_in_t = jnp.swapaxes(w_in.astype(F32), 1, 2)
    ffn_up = ffn_up.astype(F32)
    ffn_gate = ffn_gate.astype(F32)
    st_p = dict(shift=jnp.zeros((bp, A_PROJ), F32), rwkv=jnp.zeros((bp, H_A, HS_A, HS_A), F32),
                gla=jnp.zeros((1, bp, H_B, DK_B, DV_B), F32), hgrn=jnp.zeros((1, bp, H_C, HD_C, HD_C), F32))
    conv_p = jnp.zeros((bp, CONV_W - 1, D_FF), F32)
    rwkv_s0 = jnp.transpose(state_rwkv.astype(F32), (0, 2, 3, 4, 1))
    for l in range(DEPTH):
        lw = {n: arr[l] for n, arr in per_layer.items()}
        for n, arr in (("p_rwkv", p_rwkv), ("p_gla", p_gla), ("p_hgrn", p_hgrn), ("w_out", w_out),
                       ("ffn_down", ffn_down)):
            lw[n] = arr[l].astype(BF16)
        st_s = dict(shift=state_rwkv_shift[l].astype(F32), rwkv=rwkv_s0,
                    gla=state_gla.astype(F32), hgrn=state_hgrn.astype(F32))

        hp = rmsnorm(xp, lw["norm_mix_pre"], tm=NORM_TILE, out_dtype=BF16)
        hs = rmsnorm(xs, lw["norm_mix_pre"], tm=NORM_TILE, out_dtype=BF16)
        proj_p, proj_s = proj_pair(hp, hs, w_in_t, l, tm=1024)
        xp, out_p = _token_mix(xp, proj_p, bp, tp, True, st_p, 0, lw)
        xs, out_s = _token_mix(xs, proj_s, bs, ts, False, st_s, l, lw)

        hp = rmsnorm(xp, lw["norm_ffn_pre"], tm=NORM_TILE, out_dtype=BF16)
        hs = rmsnorm(xs, lw["norm_ffn_pre"], tm=NORM_TILE, out_dtype=BF16)
        act_p, out_p["conv"], act_s, out_s["conv"] = ffn_up_pair(
            hp, hs, ffn_up, ffn_gate, l, conv_p, state_ffn_conv.astype(F32), lw["ffn_conv_w"], lw["ffn_conv_b"],
            bp, tp, bs, ts, tm=FFN_TM, tn=FFN_TN, sub=FFN_SUB)
        fp = matmul(act_p, lw["ffn_down"], tm=DOWN_TM, tn=512, name="ffn_down")
        fs = matmul(act_s, lw["ffn_down"], tm=DOWN_TM, tn=512, name="ffn_down")
        xp = add_rmsnorm(xp, fp, lw["norm_ffn_post"], tm=NORM_TILE)
        xs = add_rmsnorm(xs, fs, lw["norm_ffn_post"], tm=NORM_TILE)
        for n in names:
            new_p[n].append(out_p[n])
            new_s[n].append(out_s[n])

    stack = lambda seq_: jnp.stack(seq_).astype(sdt)
    out_s = {n: stack(new_s[n]) for n in names}
    out_s["rwkv"] = jnp.transpose(out_s["rwkv"], (0, 4, 1, 2, 3))
    return (xp.reshape(bp, tp, D_MODEL), xs.reshape(bs, ts, D_MODEL),
            *(stack(new_p[n]) for n in names), *(out_s[n] for n in names))
```

```python
import functools

import jax
import jax.numpy as jnp
from jax import lax
from jax.experimental import pallas as pl
from jax.experimental.pallas import tpu as pltpu

F32 = jnp.float32
BF16 = jnp.bfloat16

D_MODEL = 4096
DEPTH = 2
HS_A = 64
W_A = 3 * D_MODEL // 8
H_A = W_A // HS_A
R_W, R_A, R_G = 128, 128, 256
A_PROJ = 3 * W_A + R_W + R_A + R_G
RWKV_LN_EPS = 64e-5
W_B = 5 * D_MODEL // 16
DV_B = 256
H_B = W_B // DV_B
DK_B = DV_B // 2
K_B = H_B * DK_B
R_GLA = 16
GLA_TAU = 16.0
W_C = 5 * D_MODEL // 16
HD_C = 128
H_C = W_C // HD_C
D_FF = 2 * D_MODEL
CONV_W = 3
RMS_EPS = 1e-6
HEAD_NORM_EPS = 1e-5

LANES = 128
SUBLANES = 8
VMEM_LIMIT_BYTES = 56 * 1024 * 1024

IN_COLS = A_PROJ + 2 * K_B + 2 * W_B + R_GLA + 4 * W_C + 3 * D_MODEL
COL_BQ = A_PROJ
COL_BK = COL_BQ + K_B
COL_BV = COL_BK + K_B
COL_BA = COL_BV + W_B
TAIL_SRC = COL_BA + R_GLA
PROJ_TN = 512
HEAD_TILES = -(-TAIL_SRC // PROJ_TN)
COL_BG = 7 * W_B
PROJ_COLS = COL_BG + IN_COLS - TAIL_SRC
TAIL_TILES = -(-(IN_COLS - TAIL_SRC) // PROJ_TN)
TAIL_TILE0 = PROJ_COLS // PROJ_TN - TAIL_TILES
TAIL_ROW0 = IN_COLS - TAIL_TILES * PROJ_TN
COL_C = COL_BG + W_B
COL_G = COL_C + 4 * W_C
assert PROJ_COLS % PROJ_TN == 0 and TAIL_TILE0 >= HEAD_TILES and TAIL_ROW0 % SUBLANES == 0
assert COL_BG % W_B == 0 and COL_C % W_C == 0 and COL_BA % LANES == 0


def _cparams(semantics):
    return pltpu.CompilerParams(dimension_semantics=semantics, vmem_limit_bytes=VMEM_LIMIT_BYTES)


def _rmsnorm_kernel(x_ref, g_ref, o_ref):
    x = x_ref[...]
    ms = jnp.mean(x * x, axis=-1, keepdims=True)
    o_ref[...] = (x * lax.rsqrt(ms + RMS_EPS) * g_ref[...]).astype(o_ref.dtype)


def rmsnorm(x, g, *, tm, out_dtype):
    m, d = x.shape
    return pl.pallas_call(
        _rmsnorm_kernel,
        grid=(m // tm,),
        in_specs=[pl.BlockSpec((tm, d), lambda i: (i, 0)), pl.BlockSpec((1, d), lambda i: (0, 0))],
        out_specs=pl.BlockSpec((tm, d), lambda i: (i, 0)),
        out_shape=jax.ShapeDtypeStruct((m, d), out_dtype),
        compiler_params=_cparams(("parallel",)),
        name="rmsnorm",
    )(x, g.reshape(1, d))


def _add_rmsnorm_kernel(x_ref, y_ref, g_ref, o_ref):
    y = y_ref[...]
    ms = jnp.mean(y * y, axis=-1, keepdims=True)
    o_ref[...] = x_ref[...] + y * lax.rsqrt(ms + RMS_EPS) * g_ref[...]


def add_rmsnorm(x, y, g, *, tm):
    m, d = x.shape
    return pl.pallas_call(
        _add_rmsnorm_kernel,
        grid=(m // tm,),
        in_specs=[pl.BlockSpec((tm, d), lambda i: (i, 0)), pl.BlockSpec((tm, d), lambda i: (i, 0)),
                  pl.BlockSpec((1, d), lambda i: (0, 0))],
        out_specs=pl.BlockSpec((tm, d), lambda i: (i, 0)),
        out_shape=jax.ShapeDtypeStruct((m, d), F32),
        compiler_params=_cparams(("parallel",)),
        name="add_rmsnorm",
    )(x, y, g.reshape(1, d))


def _add_rmsnorm_rmsnorm_kernel(x_ref, y_ref, g_ref, gn_ref, o_ref, h_ref):
    y = y_ref[...]
    ms = jnp.mean(y * y, axis=-1, keepdims=True)
    x = x_ref[...] + y * lax.rsqrt(ms + RMS_EPS) * g_ref[...]
    o_ref[...] = x
    ms = jnp.mean(x * x, axis=-1, keepdims=True)
    h_ref[...] = (x * lax.rsqrt(ms + RMS_EPS) * gn_ref[...]).astype(h_ref.dtype)


def add_rmsnorm_rmsnorm(x, y, g, g_next, *, tm):
    m, d = x.shape
    row = pl.BlockSpec((tm, d), lambda i: (i, 0))
    vec = pl.BlockSpec((1, d), lambda i: (0, 0))
    return pl.pallas_call(
        _add_rmsnorm_rmsnorm_kernel,
        grid=(m // tm,),
        in_specs=[row, row, vec, vec],
        out_specs=[row, row],
        out_shape=[jax.ShapeDtypeStruct((m, d), F32), jax.ShapeDtypeStruct((m, d), BF16)],
        compiler_params=_cparams(("parallel",)),
        name="add_rmsnorm_rmsnorm",
    )(x, y, g.reshape(1, d), g_next.reshape(1, d))


def _matmul_kernel(x_ref, w_ref, o_ref):
    o_ref[...] = jnp.dot(x_ref[...], w_ref[...], preferred_element_type=F32).astype(o_ref.dtype)


def matmul(x, w, *, tm, tn, out_dtype=F32, name="matmul"):
    m, k = x.shape
    n = w.shape[1]
    return pl.pallas_call(
        _matmul_kernel,
        grid=(m // tm, n // tn),
        in_specs=[pl.BlockSpec((tm, k), lambda i, j: (i, 0)), pl.BlockSpec((k, tn), lambda i, j: (0, j))],
        out_specs=pl.BlockSpec((tm, tn), lambda i, j: (i, j)),
        out_shape=jax.ShapeDtypeStruct((m, n), out_dtype),
        compiler_params=_cparams(("parallel", "parallel")),
        name=name,
    )(x, w)


def _proj_pair_kernel(xp_ref, xs_ref, wt_ref, op_ref, os_ref, w_bf, *, prompt_tiles):
    i = pl.program_id(1)

    @pl.when(i == 0)
    def _():
        w_bf[...] = wt_ref[...].astype(BF16)

    contract_last = (((1,), (1,)), ((), ()))

    @pl.when(i < prompt_tiles)
    def _():
        op_ref[...] = lax.dot_general(xp_ref[...], w_bf[...], contract_last, preferred_element_type=F32)

    @pl.when(i == prompt_tiles)
    def _():
        os_ref[...] = lax.dot_general(xs_ref[...], w_bf[...], contract_last, preferred_element_type=F32)


def proj_pair(xp, xs, w_in_t, layer, *, tm):
    mp, k = xp.shape
    ms = xs.shape[0]
    pt = mp // tm
    prow = lambda i: jnp.minimum(i, pt - 1)
    out_col = lambda j: jnp.where(j < HEAD_TILES, j, j + (TAIL_TILE0 - HEAD_TILES))
    w_row = lambda j: pl.multiple_of(
        jnp.where(j < HEAD_TILES, j * PROJ_TN, TAIL_ROW0 + (j - HEAD_TILES) * PROJ_TN), SUBLANES)
    return pl.pallas_call(
        functools.partial(_proj_pair_kernel, prompt_tiles=pt),
        grid=(HEAD_TILES + TAIL_TILES, pt + 1),
        in_specs=[pl.BlockSpec((tm, k), lambda j, i: (prow(i), 0)),
                  pl.BlockSpec((ms, k), lambda j, i: (0, 0), pipeline_mode=pl.Buffered(1)),
                  pl.BlockSpec((None, pl.Element(PROJ_TN), pl.Element(k)), lambda j, i: (layer, w_row(j), 0))],
        out_specs=[pl.BlockSpec((tm, PROJ_TN), lambda j, i: (prow(i), out_col(j))),
                   pl.BlockSpec((ms, PROJ_TN), lambda j, i: (0, out_col(j)))],
        out_shape=[jax.ShapeDtypeStruct((mp, PROJ_COLS), F32), jax.ShapeDtypeStruct((ms, PROJ_COLS), F32)],
        scratch_shapes=[pltpu.VMEM((PROJ_TN, k), BF16)],
        compiler_params=_cparams(("arbitrary", "arbitrary")),
        name="proj_pair",
    )(xp, xs, w_in_t)


def _merge_kernel(y_ref, g_ref, ob_ref, oc_ref, pa_ref, pb_ref, pc_ref, ga_ref, gb_ref, gc_ref, o_ref, oa_scr):
    @pl.when(pl.program_id(1) == 0)
    def _():
        oa_scr[...] = (y_ref[...] * g_ref[...]).astype(BF16)

    ma = jnp.dot(oa_scr[...], pa_ref[...], preferred_element_type=F32)
    mb = jnp.dot(ob_ref[...], pb_ref[...], preferred_element_type=F32)
    mc = jnp.dot(oc_ref[...], pc_ref[...], preferred_element_type=F32)
    merged = (jax.nn.sigmoid(ga_ref[...]) * ma + jax.nn.sigmoid(gb_ref[...]) * mb
              + jax.nn.sigmoid(gc_ref[...]) * mc)
    o_ref[...] = merged.astype(o_ref.dtype)


def merge(y_a, g_a, o_b, o_c, p_a, p_b, p_c, proj, *, tm, tn):
    m = y_a.shape[0]
    d = p_a.shape[1]
    gate0 = COL_G // tn
    gstep = d // tn

    def row(width):
        return pl.BlockSpec((tm, width), lambda i, j: (i, 0))

    def wcol(kdim):
        return pl.BlockSpec((kdim, tn), lambda i, j: (0, j))

    def gate(branch):
        return pl.BlockSpec((tm, tn), lambda i, j: (i, gate0 + branch * gstep + j))

    return pl.pallas_call(
        _merge_kernel,
        grid=(m // tm, d // tn),
        in_specs=[row(W_A), row(W_A), row(W_B), row(W_C), wcol(W_A), wcol(W_B), wcol(W_C),
                  gate(0), gate(1), gate(2)],
        out_specs=pl.BlockSpec((tm, tn), lambda i, j: (i, j)),
        out_shape=jax.ShapeDtypeStruct((m, d), BF16),
        scratch_shapes=[pltpu.VMEM((tm, W_A), BF16)],
        compiler_params=_cparams(("parallel", "arbitrary")),
        name="merge",
    )(y_a, g_a, o_b, o_c, p_a, p_b, p_c, proj, proj, proj)


def _softplus(x):
    return jnp.maximum(x, 0.0) + jnp.log(1.0 + jnp.exp(-jnp.abs(x)))


def _rwkv_gates(p, prev, mu_ref, w0_ref, w2_ref, a0_ref, a2_ref, g2_ref, r_ref, d_ref, k_ref, v_ref, a_ref, g_ref):
    ps = p + mu_ref[...] * (prev - p)
    o1 = 3 * W_A
    r_ref[...] = ps[:, :W_A]
    k_ref[...] = ps[:, W_A:2 * W_A]
    v_ref[...] = ps[:, 2 * W_A:o1]
    wc = ps[:, o1:o1 + R_W]
    ac = ps[:, o1 + R_W:o1 + R_W + R_A]
    gc = ps[:, o1 + R_W + R_A:]
    wl = w0_ref[...] + jnp.dot(jnp.tanh(wc).astype(BF16), w2_ref[...], preferred_element_type=F32)
    w = -_softplus(-wl) - 0.5
    d_ref[...] = jnp.exp(-jnp.exp(w))
    al = a0_ref[...] + jnp.dot(ac.astype(BF16), a2_ref[...], preferred_element_type=F32)
    a_ref[...] = jax.nn.sigmoid(al)
    g_ref[...] = jnp.dot(jax.nn.sigmoid(gc).astype(BF16), g2_ref[...], preferred_element_type=F32)


def _rwkv_prep_seq_kernel(p_ref, tail_ref, shift_ref, *rest, tiles_per_seq):
    i = pl.program_id(0)
    p = p_ref[...]
    first = (i % tiles_per_seq) == 0
    before = jnp.where(first, shift_ref[0], tail_ref[SUBLANES - 1:SUBLANES, :])
    row = lax.broadcasted_iota(jnp.int32, (p.shape[0], 1), 0)
    prev = jnp.where(row >= 1, pltpu.roll(p, 1, 0), before)
    _rwkv_gates(p, prev, *rest)


def _rwkv_prep_rows_kernel(p_ref, prev_ref, *rest):
    _rwkv_gates(p_ref[...], prev_ref[...], *rest)


def _rwkv_prep_call(kernel, lead_specs, lead_args, m, tm, mu, w0, w2, a0, a2, g2, name):
    full = lambda shape: pl.BlockSpec(shape, lambda i: (0, 0))
    out = jax.ShapeDtypeStruct((m, W_A), F32)
    return pl.pallas_call(
        kernel,
        grid=(m // tm,),
        in_specs=lead_specs + [full((1, A_PROJ)), full((1, W_A)), full((R_W, W_A)), full((1, W_A)),
                               full((R_A, W_A)), full((R_G, W_A))],
        out_specs=[pl.BlockSpec((tm, W_A), lambda i: (i, 0))] * 6,
        out_shape=[out] * 6,
        compiler_params=_cparams(("parallel",)),
        name=name,
    )(*lead_args, mu.reshape(1, A_PROJ), w0.reshape(1, W_A), w2.astype(BF16), a0.reshape(1, W_A),
      a2.astype(BF16), g2.astype(BF16))


def rwkv_prep_seq(proj, shift, weights, batch, seq, *, tm):
    tiles_per_seq = seq // tm
    nblk = tm // SUBLANES
    specs = [pl.BlockSpec((tm, A_PROJ), lambda i: (i, 0)),
             pl.BlockSpec((SUBLANES, A_PROJ), lambda i: (jnp.maximum(i * nblk - 1, 0), 0)),
             pl.BlockSpec((1, 1, A_PROJ), lambda i: (i // tiles_per_seq, 0, 0))]
    kern = functools.partial(_rwkv_prep_seq_kernel, tiles_per_seq=tiles_per_seq)
    return _rwkv_prep_call(kern, specs, (proj, proj, shift.reshape(batch, 1, A_PROJ)), batch * seq, tm,
                           *weights, name="rwkv_prep_seq")


def rwkv_prep_rows(proj, prev, weights, *, tm):
    specs = [pl.BlockSpec((tm, A_PROJ), lambda i: (i, 0)), pl.BlockSpec((tm, A_PROJ), lambda i: (i, 0))]
    return _rwkv_prep_call(_rwkv_prep_rows_kernel, specs, (proj, prev), proj.shape[0], tm, *weights,
                           name="rwkv_prep_rows")


def _rwkv_scan_kernel(r_ref, d_ref, k_ref, v_ref, a_ref, s0_ref, kk_ref, ka_ref, rk_ref, lng_ref, lnb_ref,
                      y_ref, st_ref, s_scr, o_scr, *, steps, tsub):
    tc = pl.program_id(1)

    @pl.when(tc == 0)
    def _():
        s_scr[...] = s0_ref[...]

    row = lax.broadcasted_iota(jnp.int32, (SUBLANES, 1), 0)

    def channel_rows(t, c0, n):
        tile, pos = t // tsub, lax.rem(t, tsub)
        return pl.ds((tile * HS_A + c0) * tsub + pos, n, stride=tsub)

    def step(t, carry):
        chans = channel_rows(t, 0, HS_A)
        r = r_ref[chans, :]
        d = d_ref[chans, :]
        k = k_ref[chans, :]
        a = a_ref[chans, :]
        kk = k * kk_ref[...]
        kk = kk * lax.rsqrt(jnp.maximum(jnp.sum(kk * kk, axis=0, keepdims=True), 1e-24))
        kp = k * (1.0 + (a - 1.0) * ka_ref[...])
        kka = kk * a

        def vblock(vo, c):
            v8 = v_ref[channel_rows(t, vo * SUBLANES, SUBLANES), :]
            o8 = jnp.zeros_like(v8)
            for vi in range(SUBLANES):
                s = s_scr[vo, vi]
                skk = jnp.sum(s * kk, axis=0, keepdims=True)
                s = s * d - skk * kka + v8[vi:vi + 1, :] * kp
                s_scr[vo, vi] = s
                o8 = jnp.where(row == vi, jnp.sum(s * r, axis=0, keepdims=True), o8)
            o_scr[pl.ds(vo * SUBLANES, SUBLANES), :] = o8
            return c

        lax.fori_loop(0, HS_A // SUBLANES, vblock, 0)
        o = o_scr[...]
        v = v_ref[chans, :]
        mean = jnp.mean(o, axis=0, keepdims=True)
        var = jnp.mean(jnp.square(o - mean), axis=0, keepdims=True)
        y = (o - mean) * lax.rsqrt(var + RWKV_LN_EPS) * lng_ref[...] + lnb_ref[...]
        bonus = jnp.sum(r * kp * rk_ref[...], axis=0, keepdims=True) * v
        y_ref[chans, :] = y + bonus
        return carry

    lax.fori_loop(0, steps, step, 0)

    @pl.when(tc == pl.num_programs(1) - 1)
    def _():
        st_ref[...] = s_scr[...]


def rwkv_scan(r, d, k, v, a, s0, layer, kk_p, ka_p, rk_p, lng_p, lnb_p, *, steps, tsub):
    rows, lanes = r.shape
    c = HS_A
    nl = lanes // LANES
    vo = c // SUBLANES
    s0 = s0.reshape(s0.shape[0], nl, vo, SUBLANES, c, LANES)
    tok = pl.BlockSpec((steps * c, LANES), lambda l, i: (i, l))
    par = pl.BlockSpec((c, LANES), lambda l, i: (0, l))
    st_in = pl.BlockSpec((None, None, vo, SUBLANES, c, LANES), lambda l, i: (layer, l, 0, 0, 0, 0))
    st_out = pl.BlockSpec((None, vo, SUBLANES, c, LANES), lambda l, i: (l, 0, 0, 0, 0))
    y, s_t = pl.pallas_call(
        functools.partial(_rwkv_scan_kernel, steps=steps, tsub=tsub),
        grid=(nl, rows // (steps * c)),
        in_specs=[tok] * 5 + [st_in] + [par] * 5,
        out_specs=[tok, st_out],
        out_shape=[jax.ShapeDtypeStruct((rows, lanes), F32),
                   jax.ShapeDtypeStruct((nl, vo, SUBLANES, c, LANES), F32)],
        scratch_shapes=[pltpu.VMEM((vo, SUBLANES, c, LANES), F32), pltpu.VMEM((c, LANES), F32)],
        compiler_params=_cparams(("parallel", "arbitrary")),
        name="rwkv_scan",
    )(r, d, k, v, a, s0, kk_p, ka_p, rk_p, lng_p, lnb_p)
    return y, s_t.reshape(nl, c, c, LANES)


LANE_SLOTS = 32
RELAYOUT_STEPS = 128


def _to_lane_tiles_kernel(x_ref, o_ref, xt_scr):
    nb, tt, _ = x_ref.shape
    for b in range(nb):
        xt_scr[b] = x_ref[b].T
    idle = jnp.zeros((LANE_SLOTS - H_A, tt), F32)
    for c in range(HS_A):
        parts = []
        for b in range(nb):
            parts += [xt_scr[b, pl.ds(c, H_A, stride=HS_A), :], idle]
        tile = jnp.concatenate(parts, axis=0).T
        o_ref[:, c, :, :] = tile.reshape(tt // SUBLANES, SUBLANES, LANES)


def _from_lane_tiles_kernel(y_ref, o_ref, xt_scr):
    nb, tt, _ = o_ref.shape
    for c in range(HS_A):
        heads = y_ref[:, c, :, :].reshape(tt, LANES).T
        for b in range(nb):
            xt_scr[b, pl.ds(c, H_A, stride=HS_A), :] = heads[b * LANE_SLOTS:b * LANE_SLOTS + H_A, :]
    for b in range(nb):
        o_ref[b] = xt_scr[b].T


def to_lane_tiles(x, batch, seq):
    tt = RELAYOUT_STEPS
    out = pl.pallas_call(
        _to_lane_tiles_kernel,
        grid=(seq // tt,),
        in_specs=[pl.BlockSpec((batch, tt, W_A), lambda i: (0, i, 0))],
        out_specs=pl.BlockSpec((tt // SUBLANES, HS_A, SUBLANES, LANES), lambda i: (i, 0, 0, 0)),
        out_shape=jax.ShapeDtypeStruct((seq // SUBLANES, HS_A, SUBLANES, LANES), F32),
        scratch_shapes=[pltpu.VMEM((batch, W_A, tt), F32)],
        compiler_params=_cparams(("parallel",)),
        name="to_lane_tiles",
    )(x.reshape(batch, seq, W_A))
    return out.reshape(seq * HS_A, LANES)


def from_lane_tiles(y, batch, seq):
    tt = RELAYOUT_STEPS
    out = pl.pallas_call(
        _from_lane_tiles_kernel,
        grid=(seq // tt,),
        in_specs=[pl.BlockSpec((tt // SUBLANES, HS_A, SUBLANES, LANES), lambda i: (i, 0, 0, 0))],
        out_specs=pl.BlockSpec((batch, tt, W_A), lambda i: (0, i, 0)),
        out_shape=jax.ShapeDtypeStruct((batch, seq, W_A), F32),
        scratch_shapes=[pltpu.VMEM((batch, W_A, tt), F32)],
        compiler_params=_cparams(("parallel",)),
        name="from_lane_tiles",
    )(y.reshape(seq // SUBLANES, HS_A, SUBLANES, LANES))
    return out.reshape(batch * seq, W_A)


def rwkv_recurrence_long(r, d, k, v, a, s0, params, batch, seq, *, steps):
    idle = LANE_SLOTS - H_A

    def param_tile(p):
        return jnp.tile(jnp.pad(p.reshape(H_A, HS_A).T, ((0, 0), (0, idle))), (1, batch))

    s0l = jnp.pad(jnp.transpose(s0, (2, 3, 0, 1)), ((0, 0), (0, 0), (0, 0), (0, idle)))
    s0l = s0l.reshape(1, 1, HS_A, HS_A, LANES)
    y, s_t = rwkv_scan(*(to_lane_tiles(z, batch, seq) for z in (r, d, k, v, a)), s0l, 0,
                       *(param_tile(p) for p in params), steps=steps, tsub=SUBLANES)
    s_t = s_t.reshape(HS_A, HS_A, batch, LANE_SLOTS)[..., :H_A]
    return from_lane_tiles(y, batch, seq), jnp.transpose(s_t, (2, 3, 0, 1))


def rwkv_recurrence_short(r, d, k, v, a, s0, layer, params, batch, seq):
    def lanes_of(z):
        z = jnp.transpose(z.reshape(batch, seq, H_A, HS_A), (3, 1, 2, 0))
        return z.reshape(HS_A * seq, H_A * batch)

    def param_tile(p):
        return jnp.repeat(p.reshape(H_A, HS_A).T, batch, axis=1)

    y, s_t = rwkv_scan(*(lanes_of(z) for z in (r, d, k, v, a)), s0, layer, *(param_tile(p) for p in params),
                       steps=seq, tsub=seq)
    y = jnp.transpose(y.reshape(HS_A, seq, H_A, batch), (3, 1, 2, 0)).reshape(batch * seq, W_A)
    return y, s_t


def _block_prefix_sum(x):
    row = lax.broadcasted_iota(jnp.int32, (SUBLANES, 1), 0)
    for shift in (1, 2, 4):
        x = x + jnp.where(row >= shift, pltpu.roll(x, shift, 0), 0.0)
    return x


def _chunk_scores(q, k, blocks):
    c = q.shape[0]
    nb = c // SUBLANES
    row = lax.broadcasted_iota(jnp.int32, (SUBLANES, 1), 0)
    lane = lax.broadcasted_iota(jnp.int32, (SUBLANES, c), 1)
    rows = []
    for ib in range(nb):
        lo = ib * SUBLANES
        qi = q[lo:lo + SUBLANES, :]
        ki = k[lo:lo + SUBLANES, :]
        bi = blocks[ib]
        att = jnp.zeros((SUBLANES, c), F32)
        for j in range(SUBLANES):
            decay = jnp.exp(jnp.minimum(bi - bi[j:j + 1, :], 0.0))
            col = jnp.sum(qi * ki[j:j + 1, :] * decay, axis=-1, keepdims=True)
            att = jnp.where(lane == lo + j, col, att)
        att = jnp.where(lane <= lo + row, att, 0.0)
        if ib > 0:
            ref = blocks[ib - 1][SUBLANES - 1:SUBLANES, :]
            qs = (qi * jnp.exp(bi - ref)).astype(BF16)
            ks = [k[jb * SUBLANES:(jb + 1) * SUBLANES, :] * jnp.exp(ref - blocks[jb]) for jb in range(ib)]
            ks.append(jnp.zeros((c - lo, q.shape[1]), F32))
            ks = jnp.concatenate(ks, axis=0).astype(BF16)
            att = att + lax.dot_general(qs, ks, (((1,), (1,)), ((), ())), preferred_element_type=F32)
        rows.append(att)
    return jnp.concatenate(rows, axis=0)


def _gated_chunk(q, k, v, log_a, s_prev):
    nb = q.shape[0] // SUBLANES
    blocks = []
    carry = None
    for ib in range(nb):
        bi = _block_prefix_sum(log_a[ib * SUBLANES:(ib + 1) * SUBLANES, :])
        if carry is not None:
            bi = bi + carry
        carry = bi[SUBLANES - 1:SUBLANES, :]
        blocks.append(bi)
    b = jnp.concatenate(blocks, axis=0)
    b_last = carry
    att = _chunk_scores(q, k, blocks)
    v_bf = v.astype(BF16)
    o = jnp.dot(att.astype(BF16), v_bf, preferred_element_type=F32)
    qe = (q * jnp.exp(b)).astype(BF16)
    o = o + jnp.dot(qe, s_prev.astype(BF16), preferred_element_type=F32)
    kd = (k * jnp.exp(b_last - b)).astype(BF16)
    e_col = jnp.transpose(jnp.broadcast_to(jnp.exp(b_last), (SUBLANES, q.shape[1])))[:, 0:1]
    s_new = s_prev * e_col + lax.dot_general(kd, v_bf, (((0,), (0,)), ((), ())), preferred_element_type=F32)
    return o, s_new


def _head_rms(o, g):
    return o * lax.rsqrt(jnp.mean(o * o, axis=-1, keepdims=True) + HEAD_NORM_EPS) * g


def _mask_padding(log_decay, valid, chunk):
    if valid >= chunk:
        return log_decay
    row = lax.broadcasted_iota(jnp.int32, (log_decay.shape[0], 1), 0)
    return jnp.where(lax.rem(row, chunk) < valid, log_decay, 0.0)


def _gla_kernel(q_ref, k_ref, v_ref, rg_ref, ac_ref, a2_ref, ab_ref, ng_ref, s0_ref, o_ref, st_ref,
                *, valid, chunk):
    @pl.when(pl.program_id(1) == 0)
    def _():
        st_ref[...] = s0_ref[...]

    gl = jnp.dot(ac_ref[...].astype(BF16), a2_ref[...], preferred_element_type=F32) + ab_ref[...]
    log_a = _mask_padding(-_softplus(-gl) / GLA_TAU, valid, chunk)
    q = q_ref[...] * (DK_B ** -0.5)
    k = k_ref[...]
    v = v_ref[...]
    rg = rg_ref[...]
    for s in range(st_ref.shape[0]):
        rs = slice(s * chunk, (s + 1) * chunk)
        for h in range(H_B):
            ks = slice(h * DK_B, (h + 1) * DK_B)
            vs = slice(h * DV_B, (h + 1) * DV_B)
            o, s_new = _gated_chunk(q[rs, ks], k[rs, ks], v[rs, vs], log_a[rs, ks], st_ref[s, h])
            st_ref[s, h] = s_new
            o_ref[rs, vs] = (_head_rms(o, ng_ref[...]) * jax.nn.silu(rg[rs, vs])).astype(o_ref.dtype)


def _hgrn_kernel(q_ref, z_ref, i_ref, g_ref, lb_ref, ng_ref, s0_ref, o_ref, st_ref, *, valid, chunk):
    @pl.when(pl.program_id(1) == 0)
    def _():
        st_ref[...] = s0_ref[...]

    q = jax.nn.silu(q_ref[...])
    lb = lb_ref[...]
    f = lb + (1.0 - lb) * jax.nn.sigmoid(z_ref[...])
    log_f = _mask_padding(jnp.log(jnp.maximum(f, 1e-30)), valid, chunk)
    k = 1.0 - f
    v = i_ref[...]
    g = g_ref[...]
    for s in range(st_ref.shape[0]):
        rs = slice(s * chunk, (s + 1) * chunk)
        for h in range(H_C):
            hs = slice(h * HD_C, (h + 1) * HD_C)
            o, s_new = _gated_chunk(q[rs, hs], k[rs, hs], v[rs, hs], log_f[rs, hs], st_ref[s, h])
            st_ref[s, h] = s_new
            o_ref[rs, hs] = (_head_rms(o, ng_ref[...]) * jax.nn.sigmoid(g[rs, hs])).astype(o_ref.dtype)


def _seq_spec(rows, width, col_block, nchunks):
    return pl.BlockSpec((rows, width), lambda b, c: (b * nchunks + c, col_block))


def _const_spec(shape):
    return pl.BlockSpec(shape, lambda b, c: (0,) * len(shape))


def _state_in_spec(layer, nseq, tail):
    return pl.BlockSpec((None, nseq) + tail, lambda b, c: (layer, b) + (0,) * len(tail))


def gla_branch(src, cols, a2, a_bias, norm_g, s0, layer, batch, seq, *, chunk, valid=None, nseq=1):
    nch = seq // chunk
    assert nseq == 1 or nch == 1
    valid = chunk if valid is None else valid
    cq, ck, cv, cg, ca = cols
    rows = chunk * nseq
    a2p = jnp.zeros((LANES, K_B), F32).at[:R_GLA].set(a2).astype(BF16)
    state = pl.BlockSpec((nseq, H_B, DK_B, DV_B), lambda b, c: (b, 0, 0, 0))
    return pl.pallas_call(
        functools.partial(_gla_kernel, valid=valid, chunk=chunk),
        grid=(batch // nseq, nch),
        in_specs=[_seq_spec(rows, K_B, cq // K_B, nch), _seq_spec(rows, K_B, ck // K_B, nch),
                  _seq_spec(rows, W_B, cv // W_B, nch), _seq_spec(rows, W_B, cg // W_B, nch),
                  _seq_spec(rows, LANES, ca // LANES, nch),
                  _const_spec((LANES, K_B)), _const_spec((1, K_B)), _const_spec((1, DV_B)),
                  _state_in_spec(layer, nseq, (H_B, DK_B, DV_B))],
        out_specs=[pl.BlockSpec((rows, W_B), lambda b, c: (b * nch + c, 0)), state],
        out_shape=[jax.ShapeDtypeStruct((batch * seq, W_B), BF16),
                   jax.ShapeDtypeStruct((batch, H_B, DK_B, DV_B), F32)],
        compiler_params=_cparams(("parallel", "arbitrary")),
        name="gla",
    )(src, src, src, src, src, a2p, a_bias.reshape(1, K_B), norm_g.reshape(1, DV_B), s0)


def hgrn_branch(src, col0, lb, norm_g, s0, layer, batch, seq, *, chunk, valid=None, nseq=1):
    nch = seq // chunk
    assert nseq == 1 or nch == 1
    valid = chunk if valid is None else valid
    c0 = col0 // W_C
    rows = chunk * nseq
    state = pl.BlockSpec((nseq, H_C, HD_C, HD_C), lambda b, c: (b, 0, 0, 0))
    return pl.pallas_call(
        functools.partial(_hgrn_kernel, valid=valid, chunk=chunk),
        grid=(batch // nseq, nch),
        in_specs=[_seq_spec(rows, W_C, c0 + j, nch) for j in range(4)]
        + [_const_spec((1, W_C)), _const_spec((1, HD_C)), _state_in_spec(layer, nseq, (H_C, HD_C, HD_C))],
        out_specs=[pl.BlockSpec((rows, W_C), lambda b, c: (b * nch + c, 0)), state],
        out_shape=[jax.ShapeDtypeStruct((batch * seq, W_C), BF16),
                   jax.ShapeDtypeStruct((batch, H_C, HD_C, HD_C), F32)],
        compiler_params=_cparams(("parallel", "arbitrary")),
        name="hgrn",
    )(src, src, src, src, lb.reshape(1, W_C), norm_g.reshape(1, HD_C), s0)


def _conv_gelu_gate(u, u1, u2, gate, cw_ref, cb_ref):
    c = cb_ref[...] + u2 * cw_ref[0:1, :]
    c = c + u1 * cw_ref[1:2, :]
    c = c + u * cw_ref[2:3, :]
    return (jax.nn.gelu(c, approximate=True) * gate).astype(BF16)


def _select_rows(sel, x):
    return jnp.dot(sel, x, precision=lax.Precision.HIGHEST, preferred_element_type=F32)


def _ffn_up_pair_kernel(xp_ref, xs_ref, wu_ref, wg_ref, stp_ref, sts_ref, cw_ref, cb_ref,
                        actp_ref, nsp_ref, acts_ref, nss_ref, wu_bf, wg_bf, carry,
                        *, prompt_tiles, tiles_per_seq, sub, short_seq):
    i = pl.program_id(1)

    @pl.when(i == 0)
    def _():
        wu_bf[...] = wu_ref[...].astype(BF16)
        wg_bf[...] = wg_ref[...].astype(BF16)

    @pl.when(i < prompt_tiles)
    def _():
        first = (i % tiles_per_seq) == 0
        pm2 = jnp.where(first, stp_ref[0, 0:1, :], carry[0:1, :])
        pm1 = jnp.where(first, stp_ref[0, 1:2, :], carry[1:2, :])
        row = lax.broadcasted_iota(jnp.int32, (sub, 1), 0)
        for s in range(xp_ref.shape[0] // sub):
            rows = slice(s * sub, (s + 1) * sub)
            x = xp_ref[rows, :]
            u = jnp.dot(x, wu_bf[...], preferred_element_type=F32)
            gate = jnp.dot(x, wg_bf[...], preferred_element_type=F32)
            u1 = jnp.where(row >= 1, pltpu.roll(u, 1, 0), pm1)
            u2 = jnp.where(row >= 2, pltpu.roll(u, 2, 0), jnp.where(row == 1, pm1, pm2))
            actp_ref[rows, :] = _conv_gelu_gate(u, u1, u2, gate, cw_ref, cb_ref)
            pm2 = u[sub - 2:sub - 1, :]
            pm1 = u[sub - 1:sub, :]
        carry[0:1, :] = pm2
        carry[1:2, :] = pm1
        nsp_ref[0, 0:1, :] = pm2
        nsp_ref[0, 1:2, :] = pm1

    @pl.when(i == prompt_tiles)
    def _():
        x = xs_ref[...]
        u = jnp.dot(x, wu_bf[...], preferred_element_type=F32)
        gate = jnp.dot(x, wg_bf[...], preferred_element_type=F32)
        m = u.shape[0]
        nseq = m // short_seq
        r = lax.broadcasted_iota(jnp.int32, (m, nseq), 0)
        start = lax.broadcasted_iota(jnp.int32, (m, nseq), 1) * short_seq
        at0 = (r == start).astype(F32)
        at1 = (r == start + 1).astype(F32)
        st0 = sts_ref[:, 0, :]
        st1 = sts_ref[:, 1, :]
        tap1 = _select_rows(at0, st1)
        tap2 = _select_rows(at0, st0) + _select_rows(at1, st1)
        pos = lax.rem(lax.broadcasted_iota(jnp.int32, (m, 1), 0), short_seq)
        u1 = jnp.where(pos >= 1, pltpu.roll(u, 1, 0), tap1)
        u2 = jnp.where(pos >= 2, pltpu.roll(u, 2, 0), tap2)
        acts_ref[...] = _conv_gelu_gate(u, u1, u2, gate, cw_ref, cb_ref)
        rr = lax.broadcasted_iota(jnp.int32, (nseq, m), 1)
        end = lax.broadcasted_iota(jnp.int32, (nseq, m), 0) * short_seq + short_seq
        nss_ref[:, 0, :] = _select_rows((rr == end - 2).astype(F32), u)
        nss_ref[:, 1, :] = _select_rows((rr == end - 1).astype(F32), u)


def ffn_up_pair(xp, xs, w_up, w_gate, layer, conv_p, conv_s, conv_w, conv_b, bp, tp, bs, ts, *, tm, tn, sub):
    mp, k = xp.shape
    ms = xs.shape[0]
    pt = mp // tm
    tiles_per_seq = tp // tm
    prow = lambda i: jnp.minimum(i, pt - 1)
    wspec = pl.BlockSpec((None, k, tn), lambda j, i: (layer, 0, j))
    pstate = pl.BlockSpec((1, CONV_W - 1, tn), lambda j, i: (prow(i) // tiles_per_seq, 0, j))
    kern = functools.partial(_ffn_up_pair_kernel, prompt_tiles=pt, tiles_per_seq=tiles_per_seq, sub=sub,
                             short_seq=ts)
    return pl.pallas_call(
        kern,
        grid=(D_FF // tn, pt + 1),
        in_specs=[pl.BlockSpec((tm, k), lambda j, i: (prow(i), 0)),
                  pl.BlockSpec((ms, k), lambda j, i: (0, 0), pipeline_mode=pl.Buffered(1)),
                  wspec, wspec, pstate,
                  pl.BlockSpec((None, bs, CONV_W - 1, tn), lambda j, i: (layer, 0, 0, j)),
                  pl.BlockSpec((CONV_W, tn), lambda j, i: (0, j)), pl.BlockSpec((1, tn), lambda j, i: (0, j))],
        out_specs=[pl.BlockSpec((tm, tn), lambda j, i: (prow(i), j)), pstate,
                   pl.BlockSpec((ms, tn), lambda j, i: (0, j)),
                   pl.BlockSpec((bs, CONV_W - 1, tn), lambda j, i: (0, 0, j))],
        out_shape=[jax.ShapeDtypeStruct((mp, D_FF), BF16), jax.ShapeDtypeStruct((bp, CONV_W - 1, D_FF), F32),
                   jax.ShapeDtypeStruct((ms, D_FF), BF16), jax.ShapeDtypeStruct((bs, CONV_W - 1, D_FF), F32)],
        scratch_shapes=[pltpu.VMEM((k, tn), BF16), pltpu.VMEM((k, tn), BF16), pltpu.VMEM((SUBLANES, tn), F32)],
        compiler_params=_cparams(("arbitrary", "arbitrary")),
        name="ffn_up_pair",
    )(xp, xs, w_up, w_gate, conv_p, conv_s, conv_w, conv_b.reshape(1, D_FF))


NORM_TILE = 256
PREP_TILE = 128
MERGE_TM = 1024
MERGE_TN = 256
FFN_TM = 1024
FFN_SUB = 512
FFN_TN = 256
DOWN_TM = 512
SCAN_STEPS = 16
SEQ_CHUNK = 64
DEC_PAD = SUBLANES
DEC_SEQS = 1


def _shifted_rows(x, first, seq):
    w = x.shape[-1]
    xs = x.reshape(-1, seq, w)
    return jnp.concatenate([first[:, None, :], xs[:, :-1]], axis=1).reshape(-1, w)


def _last_rows(x, batch, seq, back):
    return jnp.take(x, jnp.arange(batch) * seq + (seq - 1 - back), axis=0)


def _token_mix(x, proj, batch, seq, long_seq, st, layer, lw):
    m = batch * seq
    prep_w = (lw["rwkv_mu"], lw["rwkv_w0"], lw["rwkv_w2"], lw["rwkv_a0"], lw["rwkv_a2"], lw["rwkv_g2"])
    if long_seq:
        r, d, k, v, a, g = rwkv_prep_seq(proj, st["shift"], prep_w, batch, seq, tm=PREP_TILE)
    else:
        prev = _shifted_rows(proj[:, :A_PROJ], st["shift"], seq)
        r, d, k, v, a, g = rwkv_prep_rows(proj, prev, prep_w, tm=PREP_TILE)
    params = (lw["rwkv_k_k"], lw["rwkv_k_a"], lw["rwkv_r_k"], lw["rwkv_ln_g"], lw["rwkv_ln_b"])
    if long_seq:
        y_a, s_rwkv = rwkv_recurrence_long(r, d, k, v, a, st["rwkv"], params, batch, seq, steps=SCAN_STEPS)
    else:
        y_a, s_rwkv = rwkv_recurrence_short(r, d, k, v, a, st["rwkv"], layer, params, batch, seq)

    if long_seq:
        o_b, s_gla = gla_branch(proj, (COL_BQ, COL_BK, COL_BV, COL_BG, COL_BA), lw["gla_a2"], lw["gla_a_bias"],
                                lw["gla_norm_g"], st["gla"], layer, batch, seq, chunk=SEQ_CHUNK)
        o_c, s_hgrn = hgrn_branch(proj, COL_C, lw["lb"], lw["hgrn_norm_g"], st["hgrn"], layer, batch, seq,
                                  chunk=SEQ_CHUNK)
    else:
        dec = jnp.concatenate([proj[:, COL_BQ:COL_BA], proj[:, COL_BG:COL_G], proj[:, COL_BA:COL_BA + LANES]],
                              axis=1)
        dec = jnp.pad(dec.reshape(batch, seq, -1), ((0, 0), (0, DEC_PAD - seq), (0, 0)))
        dec = dec.reshape(batch * DEC_PAD, -1)
        gate_col = COL_BA - COL_BQ
        dec_cols = (0, COL_BK - COL_BQ, COL_BV - COL_BQ, gate_col, gate_col + COL_G - COL_BG)
        unpad = lambda o: o.reshape(batch, DEC_PAD, -1)[:, :seq].reshape(m, -1)
        o_b, s_gla = gla_branch(dec, dec_cols, lw["gla_a2"], lw["gla_a_bias"], lw["gla_norm_g"], st["gla"],
                                layer, batch, DEC_PAD, chunk=DEC_PAD, valid=seq, nseq=DEC_SEQS)
        o_c, s_hgrn = hgrn_branch(dec, gate_col + W_B, lw["lb"], lw["hgrn_norm_g"], st["hgrn"], layer,
                                  batch, DEC_PAD, chunk=DEC_PAD, valid=seq, nseq=DEC_SEQS)
        o_b, o_c = unpad(o_b), unpad(o_c)

    tm = 1024 if m % 1024 == 0 else m
    mtm = MERGE_TM if m % MERGE_TM == 0 else m
    merged = merge(y_a, g, o_b, o_c, lw["p_rwkv"], lw["p_gla"], lw["p_hgrn"], proj, tm=mtm, tn=MERGE_TN)
    mix = matmul(merged, lw["w_out"], tm=tm, tn=1024, name="w_out")
    x, h = add_rmsnorm_rmsnorm(x, mix, lw["norm_mix_post"], lw["norm_ffn_pre"], tm=NORM_TILE)
    s_shift = _last_rows(proj, batch, seq, 0)[:, :A_PROJ]
    return x, h, dict(shift=s_shift, rwkv=s_rwkv, gla=s_gla, hgrn=s_hgrn)


def kernel(x_prompt, x_sample, state_rwkv_shift, state_rwkv, state_gla, state_hgrn, state_ffn_conv, norm_mix_pre, norm_mix_post, norm_ffn_pre, norm_ffn_post, w_in, rwkv_mu, rwkv_w0, rwkv_w2, rwkv_a0, rwkv_a2, rwkv_g2, rwkv_k_k, rwkv_k_a, rwkv_r_k, rwkv_ln_g, rwkv_ln_b, gla_a2, gla_a_bias, gla_norm_g, hgrn_lb_logits, hgrn_norm_g, p_rwkv, p_gla, p_hgrn, w_out, ffn_up, ffn_gate, ffn_conv_w, ffn_conv_b, ffn_down):
    bp, tp, _ = x_prompt.shape
    bs, ts, _ = x_sample.shape
    sdt = state_rwkv.dtype

    probs = jax.nn.softmax(hgrn_lb_logits.astype(F32), axis=0)
    lb_all = jnp.cumsum(probs, axis=0) - probs[0]

    xp = x_prompt.reshape(bp * tp, D_MODEL)
    xs = x_sample.reshape(bs * ts, D_MODEL)
    names = ("shift", "rwkv", "gla", "hgrn", "conv")
    new_p = {n: [] for n in names}
    new_s = {n: [] for n in names}
    per_layer = dict(norm_mix_pre=norm_mix_pre, norm_mix_post=norm_mix_post, norm_ffn_pre=norm_ffn_pre,
                     norm_ffn_post=norm_ffn_post, rwkv_mu=rwkv_mu, rwkv_w0=rwkv_w0, rwkv_w2=rwkv_w2,
                     rwkv_a0=rwkv_a0, rwkv_a2=rwkv_a2, rwkv_g2=rwkv_g2, rwkv_k_k=rwkv_k_k, rwkv_k_a=rwkv_k_a,
                     rwkv_r_k=rwkv_r_k, rwkv_ln_g=rwkv_ln_g, rwkv_ln_b=rwkv_ln_b, gla_a2=gla_a2,
                     gla_a_bias=gla_a_bias, gla_norm_g=gla_norm_g, hgrn_norm_g=hgrn_norm_g,
                     ffn_conv_w=ffn_conv_w, ffn_conv_b=ffn_conv_b, lb=lb_all)
    w_in_t = jnp.swapaxes(w_in.astype(F32), 1, 2)
    ffn_up = ffn_up.astype(F32)
    ffn_gate = ffn_gate.astype(F32)
    st_p = dict(shift=jnp.zeros((bp, A_PROJ), F32), rwkv=jnp.zeros((bp, H_A, HS_A, HS_A), F32),
                gla=jnp.zeros((1, bp, H_B, DK_B, DV_B), F32), hgrn=jnp.zeros((1, bp, H_C, HD_C, HD_C), F32))
    conv_p = jnp.zeros((bp, CONV_W - 1, D_FF), F32)
    rwkv_s0 = jnp.transpose(state_rwkv.astype(F32), (0, 2, 3, 4, 1))
    for l in range(DEPTH):
        lw = {n: arr[l] for n, arr in per_layer.items()}
        for n, arr in (("p_rwkv", p_rwkv), ("p_gla", p_gla), ("p_hgrn", p_hgrn), ("w_out", w_out),
                       ("ffn_down", ffn_down)):
            lw[n] = arr[l].astype(BF16)
        st_s = dict(shift=state_rwkv_shift[l].astype(F32), rwkv=rwkv_s0,
                    gla=state_gla.astype(F32), hgrn=state_hgrn.astype(F32))

        if l == 0:
            hp = rmsnorm(xp, lw["norm_mix_pre"], tm=NORM_TILE, out_dtype=BF16)
            hs = rmsnorm(xs, lw["norm_mix_pre"], tm=NORM_TILE, out_dtype=BF16)
        proj_p, proj_s = proj_pair(hp, hs, w_in_t, l, tm=1024)
        xp, hp, out_p = _token_mix(xp, proj_p, bp, tp, True, st_p, 0, lw)
        xs, hs, out_s = _token_mix(xs, proj_s, bs, ts, False, st_s, l, lw)

        act_p, out_p["conv"], act_s, out_s["conv"] = ffn_up_pair(
            hp, hs, ffn_up, ffn_gate, l, conv_p, state_ffn_conv.astype(F32), lw["ffn_conv_w"], lw["ffn_conv_b"],
            bp, tp, bs, ts, tm=FFN_TM, tn=FFN_TN, sub=FFN_SUB)
        fp = matmul(act_p, lw["ffn_down"], tm=DOWN_TM, tn=512, name="ffn_down")
        fs = matmul(act_s, lw["ffn_down"], tm=DOWN_TM, tn=512, name="ffn_down")
        if l + 1 < DEPTH:
            xp, hp = add_rmsnorm_rmsnorm(xp, fp, lw["norm_ffn_post"], norm_mix_pre[l + 1], tm=NORM_TILE)
            xs, hs = add_rmsnorm_rmsnorm(xs, fs, lw["norm_ffn_post"], norm_mix_pre[l + 1], tm=NORM_TILE)
        else:
            xp = add_rmsnorm(xp, fp, lw["norm_ffn_post"], tm=NORM_TILE)
            xs = add_rmsnorm(xs, fs, lw["norm_ffn_post"], tm=NORM_TILE)
        for n in names:
            new_p[n].append(out_p[n])
            new_s[n].append(out_s[n])

    stack = lambda seq_: jnp.stack(seq_).astype(sdt)
    out_s = {n: stack(new_s[n]) for n in names}
    out_s["rwkv"] = jnp.transpose(out_s["rwkv"], (0, 4, 1, 2, 3))
    return (xp.reshape(bp, tp, D_MODEL), xs.reshape(bs, ts, D_MODEL),
            *(stack(new_p[n]) for n in names), *(out_s[n] for n in names))
```

```python
import functools

import jax
import jax.numpy as jnp
from jax import lax
from jax.experimental import pallas as pl
from jax.experimental.pallas import tpu as pltpu

F32 = jnp.float32
BF16 = jnp.bfloat16

D_MODEL = 4096
DEPTH = 2
HS_A = 64
W_A = 3 * D_MODEL // 8
H_A = W_A // HS_A
R_W, R_A, R_G = 128, 128, 256
A_PROJ = 3 * W_A + R_W + R_A + R_G
RWKV_LN_EPS = 64e-5
W_B = 5 * D_MODEL // 16
DV_B = 256
H_B = W_B // DV_B
DK_B = DV_B // 2
K_B = H_B * DK_B
R_GLA = 16
GLA_TAU = 16.0
W_C = 5 * D_MODEL // 16
HD_C = 128
H_C = W_C // HD_C
D_FF = 2 * D_MODEL
CONV_W = 3
RMS_EPS = 1e-6
HEAD_NORM_EPS = 1e-5

LANES = 128
SUBLANES = 8
VMEM_LIMIT_BYTES = 56 * 1024 * 1024

IN_COLS = A_PROJ + 2 * K_B + 2 * W_B + R_GLA + 4 * W_C + 3 * D_MODEL
COL_BQ = A_PROJ
COL_BK = COL_BQ + K_B
COL_BV = COL_BK + K_B
COL_BA = COL_BV + W_B
TAIL_SRC = COL_BA + R_GLA
PROJ_TN = 512
HEAD_TILES = -(-TAIL_SRC // PROJ_TN)
COL_BG = 7 * W_B
PROJ_COLS = COL_BG + IN_COLS - TAIL_SRC
TAIL_TILES = -(-(IN_COLS - TAIL_SRC) // PROJ_TN)
TAIL_TILE0 = PROJ_COLS // PROJ_TN - TAIL_TILES
TAIL_ROW0 = IN_COLS - TAIL_TILES * PROJ_TN
COL_C = COL_BG + W_B
COL_G = COL_C + 4 * W_C
assert PROJ_COLS % PROJ_TN == 0 and TAIL_TILE0 >= HEAD_TILES and TAIL_ROW0 % SUBLANES == 0
assert COL_BG % W_B == 0 and COL_C % W_C == 0 and COL_BA % LANES == 0


def _cparams(semantics):
    return pltpu.CompilerParams(dimension_semantics=semantics, vmem_limit_bytes=VMEM_LIMIT_BYTES)


def _rmsnorm_kernel(x_ref, g_ref, o_ref):
    x = x_ref[...]
    ms = jnp.mean(x * x, axis=-1, keepdims=True)
    o_ref[...] = (x * lax.rsqrt(ms + RMS_EPS) * g_ref[...]).astype(o_ref.dtype)


def rmsnorm(x, g, *, tm, out_dtype):
    m, d = x.shape
    return pl.pallas_call(
        _rmsnorm_kernel,
        grid=(m // tm,),
        in_specs=[pl.BlockSpec((tm, d), lambda i: (i, 0)), pl.BlockSpec((1, d), lambda i: (0, 0))],
        out_specs=pl.BlockSpec((tm, d), lambda i: (i, 0)),
        out_shape=jax.ShapeDtypeStruct((m, d), out_dtype),
        compiler_params=_cparams(("parallel",)),
        name="rmsnorm",
    )(x, g.reshape(1, d))


def _add_rmsnorm_kernel(x_ref, y_ref, g_ref, o_ref):
    y = y_ref[...]
    ms = jnp.mean(y * y, axis=-1, keepdims=True)
    o_ref[...] = x_ref[...] + y * lax.rsqrt(ms + RMS_EPS) * g_ref[...]


def add_rmsnorm(x, y, g, *, tm):
    m, d = x.shape
    return pl.pallas_call(
        _add_rmsnorm_kernel,
        grid=(m // tm,),
        in_specs=[pl.BlockSpec((tm, d), lambda i: (i, 0)), pl.BlockSpec((tm, d), lambda i: (i, 0)),
                  pl.BlockSpec((1, d), lambda i: (0, 0))],
        out_specs=pl.BlockSpec((tm, d), lambda i: (i, 0)),
        out_shape=jax.ShapeDtypeStruct((m, d), F32),
        compiler_params=_cparams(("parallel",)),
        name="add_rmsnorm",
    )(x, y, g.reshape(1, d))


def _add_rmsnorm_rmsnorm_kernel(x_ref, y_ref, g_ref, gn_ref, o_ref, h_ref):
    y = y_ref[...]
    ms = jnp.mean(y * y, axis=-1, keepdims=True)
    x = x_ref[...] + y * lax.rsqrt(ms + RMS_EPS) * g_ref[...]
    o_ref[...] = x
    ms = jnp.mean(x * x, axis=-1, keepdims=True)
    h_ref[...] = (x * lax.rsqrt(ms + RMS_EPS) * gn_ref[...]).astype(h_ref.dtype)


def add_rmsnorm_rmsnorm(x, y, g, g_next, *, tm):
    m, d = x.shape
    row = pl.BlockSpec((tm, d), lambda i: (i, 0))
    vec = pl.BlockSpec((1, d), lambda i: (0, 0))
    return pl.pallas_call(
        _add_rmsnorm_rmsnorm_kernel,
        grid=(m // tm,),
        in_specs=[row, row, vec, vec],
        out_specs=[row, row],
        out_shape=[jax.ShapeDtypeStruct((m, d), F32), jax.ShapeDtypeStruct((m, d), BF16)],
        compiler_params=_cparams(("parallel",)),
        name="add_rmsnorm_rmsnorm",
    )(x, y, g.reshape(1, d), g_next.reshape(1, d))


def _matmul_kernel(x_ref, w_ref, o_ref):
    o_ref[...] = jnp.dot(x_ref[...], w_ref[...], preferred_element_type=F32).astype(o_ref.dtype)


def matmul(x, w, *, tm, tn, out_dtype=F32, name="matmul"):
    m, k = x.shape
    n = w.shape[1]
    return pl.pallas_call(
        _matmul_kernel,
        grid=(m // tm, n // tn),
        in_specs=[pl.BlockSpec((tm, k), lambda i, j: (i, 0)), pl.BlockSpec((k, tn), lambda i, j: (0, j))],
        out_specs=pl.BlockSpec((tm, tn), lambda i, j: (i, j)),
        out_shape=jax.ShapeDtypeStruct((m, n), out_dtype),
        compiler_params=_cparams(("parallel", "parallel")),
        name=name,
    )(x, w)


def _proj_pair_kernel(xp_ref, xs_ref, wt_ref, op_ref, os_ref, w_bf, *, prompt_tiles):
    i = pl.program_id(1)

    @pl.when(i == 0)
    def _():
        w_bf[...] = wt_ref[...].astype(BF16)

    contract_last = (((1,), (1,)), ((), ()))

    @pl.when(i < prompt_tiles)
    def _():
        op_ref[...] = lax.dot_general(xp_ref[...], w_bf[...], contract_last, preferred_element_type=F32)

    @pl.when(i == prompt_tiles)
    def _():
        os_ref[...] = lax.dot_general(xs_ref[...], w_bf[...], contract_last, preferred_element_type=F32)


def proj_pair(xp, xs, w_in_t, layer, *, tm):
    mp, k = xp.shape
    ms = xs.shape[0]
    pt = mp // tm
    prow = lambda i: jnp.minimum(i, pt - 1)
    out_col = lambda j: jnp.where(j < HEAD_TILES, j, j + (TAIL_TILE0 - HEAD_TILES))
    w_row = lambda j: pl.multiple_of(
        jnp.where(j < HEAD_TILES, j * PROJ_TN, TAIL_ROW0 + (j - HEAD_TILES) * PROJ_TN), SUBLANES)
    return pl.pallas_call(
        functools.partial(_proj_pair_kernel, prompt_tiles=pt),
        grid=(HEAD_TILES + TAIL_TILES, pt + 1),
        in_specs=[pl.BlockSpec((tm, k), lambda j, i: (prow(i), 0)),
                  pl.BlockSpec((ms, k), lambda j, i: (0, 0), pipeline_mode=pl.Buffered(1)),
                  pl.BlockSpec((None, pl.Element(PROJ_TN), pl.Element(k)), lambda j, i: (layer, w_row(j), 0))],
        out_specs=[pl.BlockSpec((tm, PROJ_TN), lambda j, i: (prow(i), out_col(j))),
                   pl.BlockSpec((ms, PROJ_TN), lambda j, i: (0, out_col(j)))],
        out_shape=[jax.ShapeDtypeStruct((mp, PROJ_COLS), F32), jax.ShapeDtypeStruct((ms, PROJ_COLS), F32)],
        scratch_shapes=[pltpu.VMEM((PROJ_TN, k), BF16)],
        compiler_params=_cparams(("arbitrary", "arbitrary")),
        name="proj_pair",
    )(xp, xs, w_in_t)


def _merge_kernel(y_ref, g_ref, ob_ref, oc_ref, pa_ref, pb_ref, pc_ref, ga_ref, gb_ref, gc_ref, o_ref, oa_scr):
    @pl.when(pl.program_id(1) == 0)
    def _():
        oa_scr[...] = (y_ref[...] * g_ref[...]).astype(BF16)

    ma = jnp.dot(oa_scr[...], pa_ref[...], preferred_element_type=F32)
    mb = jnp.dot(ob_ref[...], pb_ref[...], preferred_element_type=F32)
    mc = jnp.dot(oc_ref[...], pc_ref[...], preferred_element_type=F32)
    merged = (jax.nn.sigmoid(ga_ref[...]) * ma + jax.nn.sigmoid(gb_ref[...]) * mb
              + jax.nn.sigmoid(gc_ref[...]) * mc)
    o_ref[...] = merged.astype(o_ref.dtype)


def merge(y_a, g_a, o_b, o_c, p_a, p_b, p_c, proj, *, tm, tn):
    m = y_a.shape[0]
    d = p_a.shape[1]
    gate0 = COL_G // tn
    gstep = d // tn

    def row(width):
        return pl.BlockSpec((tm, width), lambda i, j: (i, 0))

    def wcol(kdim):
        return pl.BlockSpec((kdim, tn), lambda i, j: (0, j))

    def gate(branch):
        return pl.BlockSpec((tm, tn), lambda i, j: (i, gate0 + branch * gstep + j))

    return pl.pallas_call(
        _merge_kernel,
        grid=(m // tm, d // tn),
        in_specs=[row(W_A), row(W_A), row(W_B), row(W_C), wcol(W_A), wcol(W_B), wcol(W_C),
                  gate(0), gate(1), gate(2)],
        out_specs=pl.BlockSpec((tm, tn), lambda i, j: (i, j)),
        out_shape=jax.ShapeDtypeStruct((m, d), BF16),
        scratch_shapes=[pltpu.VMEM((tm, W_A), BF16)],
        compiler_params=_cparams(("parallel", "arbitrary")),
        name="merge",
    )(y_a, g_a, o_b, o_c, p_a, p_b, p_c, proj, proj, proj)


def _softplus(x):
    return jnp.maximum(x, 0.0) + jnp.log(1.0 + jnp.exp(-jnp.abs(x)))


def _rwkv_gates(p, prev, mu_ref, w0_ref, w2_ref, a0_ref, a2_ref, g2_ref, r_ref, d_ref, k_ref, v_ref, a_ref, g_ref):
    ps = p + mu_ref[...] * (prev - p)
    o1 = 3 * W_A
    r_ref[...] = ps[:, :W_A]
    k_ref[...] = ps[:, W_A:2 * W_A]
    v_ref[...] = ps[:, 2 * W_A:o1]
    wc = ps[:, o1:o1 + R_W]
    ac = ps[:, o1 + R_W:o1 + R_W + R_A]
    gc = ps[:, o1 + R_W + R_A:]
    wl = w0_ref[...] + jnp.dot(jnp.tanh(wc).astype(BF16), w2_ref[...], preferred_element_type=F32)
    w = -_softplus(-wl) - 0.5
    d_ref[...] = jnp.exp(-jnp.exp(w))
    al = a0_ref[...] + jnp.dot(ac.astype(BF16), a2_ref[...], preferred_element_type=F32)
    a_ref[...] = jax.nn.sigmoid(al)
    g_ref[...] = jnp.dot(jax.nn.sigmoid(gc).astype(BF16), g2_ref[...], preferred_element_type=F32)


def _rwkv_prep_seq_kernel(p_ref, tail_ref, shift_ref, *rest, tiles_per_seq):
    i = pl.program_id(0)
    p = p_ref[...]
    first = (i % tiles_per_seq) == 0
    before = jnp.where(first, shift_ref[0], tail_ref[SUBLANES - 1:SUBLANES, :])
    row = lax.broadcasted_iota(jnp.int32, (p.shape[0], 1), 0)
    prev = jnp.where(row >= 1, pltpu.roll(p, 1, 0), before)
    _rwkv_gates(p, prev, *rest)


def _rwkv_prep_rows_kernel(p_ref, prev_ref, *rest):
    _rwkv_gates(p_ref[...], prev_ref[...], *rest)


def _rwkv_prep_call(kernel, lead_specs, lead_args, m, tm, mu, w0, w2, a0, a2, g2, name):
    full = lambda shape: pl.BlockSpec(shape, lambda i: (0, 0))
    out = jax.ShapeDtypeStruct((m, W_A), F32)
    return pl.pallas_call(
        kernel,
        grid=(m // tm,),
        in_specs=lead_specs + [full((1, A_PROJ)), full((1, W_A)), full((R_W, W_A)), full((1, W_A)),
                               full((R_A, W_A)), full((R_G, W_A))],
        out_specs=[pl.BlockSpec((tm, W_A), lambda i: (i, 0))] * 6,
        out_shape=[out] * 6,
        compiler_params=_cparams(("parallel",)),
        name=name,
    )(*lead_args, mu.reshape(1, A_PROJ), w0.reshape(1, W_A), w2.astype(BF16), a0.reshape(1, W_A),
      a2.astype(BF16), g2.astype(BF16))


def rwkv_prep_seq(proj, shift, weights, batch, seq, *, tm):
    tiles_per_seq = seq // tm
    nblk = tm // SUBLANES
    specs = [pl.BlockSpec((tm, A_PROJ), lambda i: (i, 0)),
             pl.BlockSpec((SUBLANES, A_PROJ), lambda i: (jnp.maximum(i * nblk - 1, 0), 0)),
             pl.BlockSpec((1, 1, A_PROJ), lambda i: (i // tiles_per_seq, 0, 0))]
    kern = functools.partial(_rwkv_prep_seq_kernel, tiles_per_seq=tiles_per_seq)
    return _rwkv_prep_call(kern, specs, (proj, proj, shift.reshape(batch, 1, A_PROJ)), batch * seq, tm,
                           *weights, name="rwkv_prep_seq")


def rwkv_prep_rows(proj, prev, weights, *, tm):
    specs = [pl.BlockSpec((tm, A_PROJ), lambda i: (i, 0)), pl.BlockSpec((tm, A_PROJ), lambda i: (i, 0))]
    return _rwkv_prep_call(_rwkv_prep_rows_kernel, specs, (proj, prev), proj.shape[0], tm, *weights,
                           name="rwkv_prep_rows")


def _rwkv_scan_kernel(r_ref, d_ref, k_ref, v_ref, a_ref, s0_ref, kk_ref, ka_ref, rk_ref, lng_ref, lnb_ref,
                      y_ref, st_ref, s_scr, o_scr, t_scr, b_scr, *, steps, tsub):
    tc = pl.program_id(1)

    @pl.when(tc == 0)
    def _():
        s_scr[...] = s0_ref[...]
        o_scr[...] = jnp.zeros_like(o_scr)
        b_scr[...] = jnp.zeros_like(b_scr)

    row = lax.broadcasted_iota(jnp.int32, (SUBLANES, 1), 0)

    def channel_rows(t, c0, n):
        tile, pos = t // tsub, lax.rem(t, tsub)
        return pl.ds((tile * HS_A + c0) * tsub + pos, n, stride=tsub)

    def prepare(t, slot):
        chans = channel_rows(t, 0, HS_A)
        r = r_ref[chans, :]
        k = k_ref[chans, :]
        a = a_ref[chans, :]
        kk = k * kk_ref[...]
        kk = kk * lax.rsqrt(jnp.maximum(jnp.sum(kk * kk, axis=0, keepdims=True), 1e-24))
        kp = k * (1.0 + (a - 1.0) * ka_ref[...])
        t_scr[slot, 0] = kk
        t_scr[slot, 1] = d_ref[chans, :]
        t_scr[slot, 2] = kk * a
        t_scr[slot, 3] = kp
        t_scr[slot, 4] = r
        bonus = jnp.sum(r * kp * rk_ref[...], axis=0, keepdims=True)
        b_scr[slot] = jnp.broadcast_to(bonus, b_scr.shape[1:])

    def update(t, slot):
        kk, d, kka, kp, r = (t_scr[slot, j] for j in range(5))
        for vo in range(HS_A // SUBLANES):
            v8 = v_ref[channel_rows(t, vo * SUBLANES, SUBLANES), :]
            o8 = jnp.zeros_like(v8)
            for vi in range(SUBLANES):
                s = s_scr[vo, vi]
                skk = jnp.sum(s * kk, axis=0, keepdims=True)
                s = s * d - skk * kka + v8[vi:vi + 1, :] * kp
                s_scr[vo, vi] = s
                o8 = jnp.where(row == vi, jnp.sum(s * r, axis=0, keepdims=True), o8)
            o_scr[slot, vo * SUBLANES:(vo + 1) * SUBLANES, :] = o8

    def finish(t, slot):
        chans = channel_rows(t, 0, HS_A)
        o = o_scr[slot]
        mean = jnp.mean(o, axis=0, keepdims=True)
        var = jnp.mean(jnp.square(o - mean), axis=0, keepdims=True)
        y = (o - mean) * lax.rsqrt(var + RWKV_LN_EPS) * lng_ref[...] + lnb_ref[...]
        y_ref[chans, :] = y + b_scr[slot, 0:1, :] * v_ref[chans, :]

    prepare(0, 0)

    def step(t, carry):
        slot = lax.rem(t, 2)
        other = 1 - slot
        finish(jnp.maximum(t - 1, 0), other)
        update(t, slot)
        prepare(jnp.minimum(t + 1, steps - 1), other)
        return carry

    lax.fori_loop(0, steps, step, 0)
    finish(steps - 1, (steps - 1) % 2)

    @pl.when(tc == pl.num_programs(1) - 1)
    def _():
        st_ref[...] = s_scr[...]


def rwkv_scan(r, d, k, v, a, s0, layer, kk_p, ka_p, rk_p, lng_p, lnb_p, *, steps, tsub):
    rows, lanes = r.shape
    c = HS_A
    nl = lanes // LANES
    vo = c // SUBLANES
    s0 = s0.reshape(s0.shape[0], nl, vo, SUBLANES, c, LANES)
    tok = pl.BlockSpec((steps * c, LANES), lambda l, i: (i, l))
    par = pl.BlockSpec((c, LANES), lambda l, i: (0, l))
    st_in = pl.BlockSpec((None, None, vo, SUBLANES, c, LANES), lambda l, i: (layer, l, 0, 0, 0, 0))
    st_out = pl.BlockSpec((None, vo, SUBLANES, c, LANES), lambda l, i: (l, 0, 0, 0, 0))
    y, s_t = pl.pallas_call(
        functools.partial(_rwkv_scan_kernel, steps=steps, tsub=tsub),
        grid=(nl, rows // (steps * c)),
        in_specs=[tok] * 5 + [st_in] + [par] * 5,
        out_specs=[tok, st_out],
        out_shape=[jax.ShapeDtypeStruct((rows, lanes), F32),
                   jax.ShapeDtypeStruct((nl, vo, SUBLANES, c, LANES), F32)],
        scratch_shapes=[pltpu.VMEM((vo, SUBLANES, c, LANES), F32), pltpu.VMEM((2, c, LANES), F32),
                        pltpu.VMEM((2, 5, c, LANES), F32), pltpu.VMEM((2, SUBLANES, LANES), F32)],
        compiler_params=_cparams(("parallel", "arbitrary")),
        name="rwkv_scan",
    )(r, d, k, v, a, s0, kk_p, ka_p, rk_p, lng_p, lnb_p)
    return y, s_t.reshape(nl, c, c, LANES)


LANE_SLOTS = 32
RELAYOUT_STEPS = 128


HEAD_PITCH = HS_A + SUBLANES


def _to_lane_tiles_kernel(x_ref, o_ref, xt_scr):
    nb, tt, _ = x_ref.shape
    for b in range(nb):
        xt = x_ref[b].T
        for h in range(H_A):
            xt_scr[b, h * HEAD_PITCH:h * HEAD_PITCH + HS_A, :] = xt[h * HS_A:(h + 1) * HS_A, :]
    idle = jnp.zeros((LANE_SLOTS - H_A, tt), F32)
    for c in range(HS_A):
        parts = []
        for b in range(nb):
            parts += [xt_scr[b, pl.ds(c, H_A, stride=HEAD_PITCH), :], idle]
        tile = jnp.concatenate(parts, axis=0).T
        o_ref[:, c, :, :] = tile.reshape(tt // SUBLANES, SUBLANES, LANES)


def _from_lane_tiles_kernel(y_ref, o_ref, xt_scr):
    nb, tt, _ = o_ref.shape
    for c in range(HS_A):
        heads = y_ref[:, c, :, :].reshape(tt, LANES).T
        for b in range(nb):
            xt_scr[b, pl.ds(c, H_A, stride=HEAD_PITCH), :] = heads[b * LANE_SLOTS:b * LANE_SLOTS + H_A, :]
    for b in range(nb):
        xt = jnp.concatenate([xt_scr[b, h * HEAD_PITCH:h * HEAD_PITCH + HS_A, :] for h in range(H_A)], axis=0)
        o_ref[b] = xt.T


def to_lane_tiles(x, batch, seq):
    tt = RELAYOUT_STEPS
    out = pl.pallas_call(
        _to_lane_tiles_kernel,
        grid=(seq // tt,),
        in_specs=[pl.BlockSpec((batch, tt, W_A), lambda i: (0, i, 0))],
        out_specs=pl.BlockSpec((tt // SUBLANES, HS_A, SUBLANES, LANES), lambda i: (i, 0, 0, 0)),
        out_shape=jax.ShapeDtypeStruct((seq // SUBLANES, HS_A, SUBLANES, LANES), F32),
        scratch_shapes=[pltpu.VMEM((batch, H_A * HEAD_PITCH, tt), F32)],
        compiler_params=_cparams(("parallel",)),
        name="to_lane_tiles",
    )(x.reshape(batch, seq, W_A))
    return out.reshape(seq * HS_A, LANES)


def from_lane_tiles(y, batch, seq):
    tt = RELAYOUT_STEPS
    out = pl.pallas_call(
        _from_lane_tiles_kernel,
        grid=(seq // tt,),
        in_specs=[pl.BlockSpec((tt // SUBLANES, HS_A, SUBLANES, LANES), lambda i: (i, 0, 0, 0))],
        out_specs=pl.BlockSpec((batch, tt, W_A), lambda i: (0, i, 0)),
        out_shape=jax.ShapeDtypeStruct((batch, seq, W_A), F32),
        scratch_shapes=[pltpu.VMEM((batch, H_A * HEAD_PITCH, tt), F32)],
        compiler_params=_cparams(("parallel",)),
        name="from_lane_tiles",
    )(y.reshape(seq // SUBLANES, HS_A, SUBLANES, LANES))
    return out.reshape(batch * seq, W_A)


def rwkv_recurrence_long(r, d, k, v, a, s0, params, batch, seq, *, steps):
    idle = LANE_SLOTS - H_A

    def param_tile(p):
        return jnp.tile(jnp.pad(p.reshape(H_A, HS_A).T, ((0, 0), (0, idle))), (1, batch))

    s0l = jnp.pad(jnp.transpose(s0, (2, 3, 0, 1)), ((0, 0), (0, 0), (0, 0), (0, idle)))
    s0l = s0l.reshape(1, 1, HS_A, HS_A, LANES)
    y, s_t = rwkv_scan(*(to_lane_tiles(z, batch, seq) for z in (r, d, k, v, a)), s0l, 0,
                       *(param_tile(p) for p in params), steps=steps, tsub=SUBLANES)
    s_t = s_t.reshape(HS_A, HS_A, batch, LANE_SLOTS)[..., :H_A]
    return from_lane_tiles(y, batch, seq), jnp.transpose(s_t, (2, 3, 0, 1))


def rwkv_recurrence_short(r, d, k, v, a, s0, layer, params, batch, seq):
    def lanes_of(z):
        z = jnp.transpose(z.reshape(batch, seq, H_A, HS_A), (3, 1, 2, 0))
        return z.reshape(HS_A * seq, H_A * batch)

    def param_tile(p):
        return jnp.repeat(p.reshape(H_A, HS_A).T, batch, axis=1)

    y, s_t = rwkv_scan(*(lanes_of(z) for z in (r, d, k, v, a)), s0, layer, *(param_tile(p) for p in params),
                       steps=seq, tsub=seq)
    y = jnp.transpose(y.reshape(HS_A, seq, H_A, batch), (3, 1, 2, 0)).reshape(batch * seq, W_A)
    return y, s_t


def _block_prefix_sum(x):
    row = lax.broadcasted_iota(jnp.int32, (SUBLANES, 1), 0)
    for shift in (1, 2, 4):
        x = x + jnp.where(row >= shift, pltpu.roll(x, shift, 0), 0.0)
    return x


def _chunk_scores(q, k, blocks):
    c = q.shape[0]
    nb = c // SUBLANES
    row = lax.broadcasted_iota(jnp.int32, (SUBLANES, 1), 0)
    lane = lax.broadcasted_iota(jnp.int32, (SUBLANES, c), 1)
    rows = []
    for ib in range(nb):
        lo = ib * SUBLANES
        qi = q[lo:lo + SUBLANES, :]
        ki = k[lo:lo + SUBLANES, :]
        bi = blocks[ib]
        att = jnp.zeros((SUBLANES, c), F32)
        for j in range(SUBLANES):
            decay = jnp.exp(jnp.minimum(bi - bi[j:j + 1, :], 0.0))
            col = jnp.sum(qi * ki[j:j + 1, :] * decay, axis=-1, keepdims=True)
            att = jnp.where(lane == lo + j, col, att)
        att = jnp.where(lane <= lo + row, att, 0.0)
        if ib > 0:
            ref = blocks[ib - 1][SUBLANES - 1:SUBLANES, :]
            qs = (qi * jnp.exp(bi - ref)).astype(BF16)
            ks = [k[jb * SUBLANES:(jb + 1) * SUBLANES, :] * jnp.exp(ref - blocks[jb]) for jb in range(ib)]
            ks.append(jnp.zeros((c - lo, q.shape[1]), F32))
            ks = jnp.concatenate(ks, axis=0).astype(BF16)
            att = att + lax.dot_general(qs, ks, (((1,), (1,)), ((), ())), preferred_element_type=F32)
        rows.append(att)
    return jnp.concatenate(rows, axis=0)


def _gated_chunk(q, k, v, log_a, s_prev):
    nb = q.shape[0] // SUBLANES
    blocks = []
    carry = None
    for ib in range(nb):
        bi = _block_prefix_sum(log_a[ib * SUBLANES:(ib + 1) * SUBLANES, :])
        if carry is not None:
            bi = bi + carry
        carry = bi[SUBLANES - 1:SUBLANES, :]
        blocks.append(bi)
    b = jnp.concatenate(blocks, axis=0)
    b_last = carry
    att = _chunk_scores(q, k, blocks)
    v_bf = v.astype(BF16)
    o = jnp.dot(att.astype(BF16), v_bf, preferred_element_type=F32)
    qe = (q * jnp.exp(b)).astype(BF16)
    o = o + jnp.dot(qe, s_prev.astype(BF16), preferred_element_type=F32)
    kd = (k * jnp.exp(b_last - b)).astype(BF16)
    e_col = jnp.transpose(jnp.broadcast_to(jnp.exp(b_last), (SUBLANES, q.shape[1])))[:, 0:1]
    s_new = s_prev * e_col + lax.dot_general(kd, v_bf, (((0,), (0,)), ((), ())), preferred_element_type=F32)
    return o, s_new


def _head_rms(o, g):
    return o * lax.rsqrt(jnp.mean(o * o, axis=-1, keepdims=True) + HEAD_NORM_EPS) * g


def _mask_padding(log_decay, valid, chunk):
    if valid >= chunk:
        return log_decay
    row = lax.broadcasted_iota(jnp.int32, (log_decay.shape[0], 1), 0)
    return jnp.where(lax.rem(row, chunk) < valid, log_decay, 0.0)


def _gla_kernel(q_ref, k_ref, v_ref, rg_ref, ac_ref, a2_ref, ab_ref, ng_ref, s0_ref, o_ref, st_ref,
                *, valid, chunk):
    @pl.when(pl.program_id(1) == 0)
    def _():
        st_ref[...] = s0_ref[...]

    gl = jnp.dot(ac_ref[...].astype(BF16), a2_ref[...], preferred_element_type=F32) + ab_ref[...]
    log_a = _mask_padding(-_softplus(-gl) / GLA_TAU, valid, chunk)
    q = q_ref[...] * (DK_B ** -0.5)
    k = k_ref[...]
    v = v_ref[...]
    rg = rg_ref[...]
    for s in range(st_ref.shape[0]):
        rs = slice(s * chunk, (s + 1) * chunk)
        for h in range(H_B):
            ks = slice(h * DK_B, (h + 1) * DK_B)
            vs = slice(h * DV_B, (h + 1) * DV_B)
            o, s_new = _gated_chunk(q[rs, ks], k[rs, ks], v[rs, vs], log_a[rs, ks], st_ref[s, h])
            st_ref[s, h] = s_new
            o_ref[rs, vs] = (_head_rms(o, ng_ref[...]) * jax.nn.silu(rg[rs, vs])).astype(o_ref.dtype)


def _hgrn_kernel(q_ref, z_ref, i_ref, g_ref, lb_ref, ng_ref, s0_ref, o_ref, st_ref, *, valid, chunk):
    @pl.when(pl.program_id(1) == 0)
    def _():
        st_ref[...] = s0_ref[...]

    q = jax.nn.silu(q_ref[...])
    lb = lb_ref[...]
    f = lb + (1.0 - lb) * jax.nn.sigmoid(z_ref[...])
    log_f = _mask_padding(jnp.log(jnp.maximum(f, 1e-30)), valid, chunk)
    k = 1.0 - f
    v = i_ref[...]
    g = g_ref[...]
    for s in range(st_ref.shape[0]):
        rs = slice(s * chunk, (s + 1) * chunk)
        for h in range(H_C):
            hs = slice(h * HD_C, (h + 1) * HD_C)
            o, s_new = _gated_chunk(q[rs, hs], k[rs, hs], v[rs, hs], log_f[rs, hs], st_ref[s, h])
            st_ref[s, h] = s_new
            o_ref[rs, hs] = (_head_rms(o, ng_ref[...]) * jax.nn.sigmoid(g[rs, hs])).astype(o_ref.dtype)


def _seq_spec(rows, width, col_block, nchunks):
    return pl.BlockSpec((rows, width), lambda b, c: (b * nchunks + c, col_block))


def _const_spec(shape):
    return pl.BlockSpec(shape, lambda b, c: (0,) * len(shape))


def _state_in_spec(layer, nseq, tail):
    return pl.BlockSpec((None, nseq) + tail, lambda b, c: (layer, b) + (0,) * len(tail))


def gla_branch(src, cols, a2, a_bias, norm_g, s0, layer, batch, seq, *, chunk, valid=None, nseq=1):
    nch = seq // chunk
    assert nseq == 1 or nch == 1
    valid = chunk if valid is None else valid
    cq, ck, cv, cg, ca = cols
    rows = chunk * nseq
    a2p = jnp.zeros((LANES, K_B), F32).at[:R_GLA].set(a2).astype(BF16)
    state = pl.BlockSpec((nseq, H_B, DK_B, DV_B), lambda b, c: (b, 0, 0, 0))
    return pl.pallas_call(
        functools.partial(_gla_kernel, valid=valid, chunk=chunk),
        grid=(batch // nseq, nch),
        in_specs=[_seq_spec(rows, K_B, cq // K_B, nch), _seq_spec(rows, K_B, ck // K_B, nch),
                  _seq_spec(rows, W_B, cv // W_B, nch), _seq_spec(rows, W_B, cg // W_B, nch),
                  _seq_spec(rows, LANES, ca // LANES, nch),
                  _const_spec((LANES, K_B)), _const_spec((1, K_B)), _const_spec((1, DV_B)),
                  _state_in_spec(layer, nseq, (H_B, DK_B, DV_B))],
        out_specs=[pl.BlockSpec((rows, W_B), lambda b, c: (b * nch + c, 0)), state],
        out_shape=[jax.ShapeDtypeStruct((batch * seq, W_B), BF16),
                   jax.ShapeDtypeStruct((batch, H_B, DK_B, DV_B), F32)],
        compiler_params=_cparams(("parallel", "arbitrary")),
        name="gla",
    )(src, src, src, src, src, a2p, a_bias.reshape(1, K_B), norm_g.reshape(1, DV_B), s0)


def hgrn_branch(src, col0, lb, norm_g, s0, layer, batch, seq, *, chunk, valid=None, nseq=1):
    nch = seq // chunk
    assert nseq == 1 or nch == 1
    valid = chunk if valid is None else valid
    c0 = col0 // W_C
    rows = chunk * nseq
    state = pl.BlockSpec((nseq, H_C, HD_C, HD_C), lambda b, c: (b, 0, 0, 0))
    return pl.pallas_call(
        functools.partial(_hgrn_kernel, valid=valid, chunk=chunk),
        grid=(batch // nseq, nch),
        in_specs=[_seq_spec(rows, W_C, c0 + j, nch) for j in range(4)]
        + [_const_spec((1, W_C)), _const_spec((1, HD_C)), _state_in_spec(layer, nseq, (H_C, HD_C, HD_C))],
        out_specs=[pl.BlockSpec((rows, W_C), lambda b, c: (b * nch + c, 0)), state],
        out_shape=[jax.ShapeDtypeStruct((batch * seq, W_C), BF16),
                   jax.ShapeDtypeStruct((batch, H_C, HD_C, HD_C), F32)],
        compiler_params=_cparams(("parallel", "arbitrary")),
        name="hgrn",
    )(src, src, src, src, lb.reshape(1, W_C), norm_g.reshape(1, HD_C), s0)


def _conv_gelu_gate(u, u1, u2, gate, cw_ref, cb_ref):
    c = cb_ref[...] + u2 * cw_ref[0:1, :]
    c = c + u1 * cw_ref[1:2, :]
    c = c + u * cw_ref[2:3, :]
    return (jax.nn.gelu(c, approximate=True) * gate).astype(BF16)


def _select_rows(sel, x):
    return jnp.dot(sel, x, precision=lax.Precision.HIGHEST, preferred_element_type=F32)


def _ffn_up_pair_kernel(xp_ref, xs_ref, wu_ref, wg_ref, stp_ref, sts_ref, cw_ref, cb_ref,
                        actp_ref, nsp_ref, acts_ref, nss_ref, wu_bf, wg_bf, carry,
                        *, prompt_tiles, tiles_per_seq, sub, short_seq):
    i = pl.program_id(1)

    @pl.when(i == 0)
    def _():
        wu_bf[...] = wu_ref[...].astype(BF16)
        wg_bf[...] = wg_ref[...].astype(BF16)

    @pl.when(i < prompt_tiles)
    def _():
        first = (i % tiles_per_seq) == 0
        pm2 = jnp.where(first, stp_ref[0, 0:1, :], carry[0:1, :])
        pm1 = jnp.where(first, stp_ref[0, 1:2, :], carry[1:2, :])
        row = lax.broadcasted_iota(jnp.int32, (sub, 1), 0)
        for s in range(xp_ref.shape[0] // sub):
            rows = slice(s * sub, (s + 1) * sub)
            x = xp_ref[rows, :]
            u = jnp.dot(x, wu_bf[...], preferred_element_type=F32)
            gate = jnp.dot(x, wg_bf[...], preferred_element_type=F32)
            u1 = jnp.where(row >= 1, pltpu.roll(u, 1, 0), pm1)
            u2 = jnp.where(row >= 2, pltpu.roll(u, 2, 0), jnp.where(row == 1, pm1, pm2))
            actp_ref[rows, :] = _conv_gelu_gate(u, u1, u2, gate, cw_ref, cb_ref)
            pm2 = u[sub - 2:sub - 1, :]
            pm1 = u[sub - 1:sub, :]
        carry[0:1, :] = pm2
        carry[1:2, :] = pm1
        nsp_ref[0, 0:1, :] = pm2
        nsp_ref[0, 1:2, :] = pm1

    @pl.when(i == prompt_tiles)
    def _():
        x = xs_ref[...]
        u = jnp.dot(x, wu_bf[...], preferred_element_type=F32)
        gate = jnp.dot(x, wg_bf[...], preferred_element_type=F32)
        m = u.shape[0]
        nseq = m // short_seq
        r = lax.broadcasted_iota(jnp.int32, (m, nseq), 0)
        start = lax.broadcasted_iota(jnp.int32, (m, nseq), 1) * short_seq
        at0 = (r == start).astype(F32)
        at1 = (r == start + 1).astype(F32)
        st0 = sts_ref[:, 0, :]
        st1 = sts_ref[:, 1, :]
        tap1 = _select_rows(at0, st1)
        tap2 = _select_rows(at0, st0) + _select_rows(at1, st1)
        pos = lax.rem(lax.broadcasted_iota(jnp.int32, (m, 1), 0), short_seq)
        u1 = jnp.where(pos >= 1, pltpu.roll(u, 1, 0), tap1)
        u2 = jnp.where(pos >= 2, pltpu.roll(u, 2, 0), tap2)
        acts_ref[...] = _conv_gelu_gate(u, u1, u2, gate, cw_ref, cb_ref)
        rr = lax.broadcasted_iota(jnp.int32, (nseq, m), 1)
        end = lax.broadcasted_iota(jnp.int32, (nseq, m), 0) * short_seq + short_seq
        nss_ref[:, 0, :] = _select_rows((rr == end - 2).astype(F32), u)
        nss_ref[:, 1, :] = _select_rows((rr == end - 1).astype(F32), u)


def ffn_up_pair(xp, xs, w_up, w_gate, layer, conv_p, conv_s, conv_w, conv_b, bp, tp, bs, ts, *, tm, tn, sub):
    mp, k = xp.shape
    ms = xs.shape[0]
    pt = mp // tm
    tiles_per_seq = tp // tm
    prow = lambda i: jnp.minimum(i, pt - 1)
    wspec = pl.BlockSpec((None, k, tn), lambda j, i: (layer, 0, j))
    pstate = pl.BlockSpec((1, CONV_W - 1, tn), lambda j, i: (prow(i) // tiles_per_seq, 0, j))
    kern = functools.partial(_ffn_up_pair_kernel, prompt_tiles=pt, tiles_per_seq=tiles_per_seq, sub=sub,
                             short_seq=ts)
    return pl.pallas_call(
        kern,
        grid=(D_FF // tn, pt + 1),
        in_specs=[pl.BlockSpec((tm, k), lambda j, i: (prow(i), 0)),
                  pl.BlockSpec((ms, k), lambda j, i: (0, 0), pipeline_mode=pl.Buffered(1)),
                  wspec, wspec, pstate,
                  pl.BlockSpec((None, bs, CONV_W - 1, tn), lambda j, i: (layer, 0, 0, j)),
                  pl.BlockSpec((CONV_W, tn), lambda j, i: (0, j)), pl.BlockSpec((1, tn), lambda j, i: (0, j))],
        out_specs=[pl.BlockSpec((tm, tn), lambda j, i: (prow(i), j)), pstate,
                   pl.BlockSpec((ms, tn), lambda j, i: (0, j)),
                   pl.BlockSpec((bs, CONV_W - 1, tn), lambda j, i: (0, 0, j))],
        out_shape=[jax.ShapeDtypeStruct((mp, D_FF), BF16), jax.ShapeDtypeStruct((bp, CONV_W - 1, D_FF), F32),
                   jax.ShapeDtypeStruct((ms, D_FF), BF16), jax.ShapeDtypeStruct((bs, CONV_W - 1, D_FF), F32)],
        scratch_shapes=[pltpu.VMEM((k, tn), BF16), pltpu.VMEM((k, tn), BF16), pltpu.VMEM((SUBLANES, tn), F32)],
        compiler_params=_cparams(("arbitrary", "arbitrary")),
        name="ffn_up_pair",
    )(xp, xs, w_up, w_gate, conv_p, conv_s, conv_w, conv_b.reshape(1, D_FF))


NORM_TILE = 256
PREP_TILE = 128
MERGE_TM = 1024
MERGE_TN = 256
FFN_TM = 1024
FFN_SUB = 512
FFN_TN = 256
DOWN_TM = 512
SCAN_STEPS = 32
SEQ_CHUNK = 64
DEC_PAD = SUBLANES
DEC_SEQS = 1


def _shifted_rows(x, first, seq):
    w = x.shape[-1]
    xs = x.reshape(-1, seq, w)
    return jnp.concatenate([first[:, None, :], xs[:, :-1]], axis=1).reshape(-1, w)


def _last_rows(x, batch, seq, back):
    return jnp.take(x, jnp.arange(batch) * seq + (seq - 1 - back), axis=0)


def _token_mix(x, proj, batch, seq, long_seq, st, layer, lw):
    m = batch * seq
    prep_w = (lw["rwkv_mu"], lw["rwkv_w0"], lw["rwkv_w2"], lw["rwkv_a0"], lw["rwkv_a2"], lw["rwkv_g2"])
    if long_seq:
        r, d, k, v, a, g = rwkv_prep_seq(proj, st["shift"], prep_w, batch, seq, tm=PREP_TILE)
    else:
        prev = _shifted_rows(proj[:, :A_PROJ], st["shift"], seq)
        r, d, k, v, a, g = rwkv_prep_rows(proj, prev, prep_w, tm=PREP_TILE)
    params = (lw["rwkv_k_k"], lw["rwkv_k_a"], lw["rwkv_r_k"], lw["rwkv_ln_g"], lw["rwkv_ln_b"])
    if long_seq:
        y_a, s_rwkv = rwkv_recurrence_long(r, d, k, v, a, st["rwkv"], params, batch, seq, steps=SCAN_STEPS)
    else:
        y_a, s_rwkv = rwkv_recurrence_short(r, d, k, v, a, st["rwkv"], layer, params, batch, seq)

    if long_seq:
        o_b, s_gla = gla_branch(proj, (COL_BQ, COL_BK, COL_BV, COL_BG, COL_BA), lw["gla_a2"], lw["gla_a_bias"],
                                lw["gla_norm_g"], st["gla"], layer, batch, seq, chunk=SEQ_CHUNK)
        o_c, s_hgrn = hgrn_branch(proj, COL_C, lw["lb"], lw["hgrn_norm_g"], st["hgrn"], layer, batch, seq,
                                  chunk=SEQ_CHUNK)
    else:
        dec = jnp.concatenate([proj[:, COL_BQ:COL_BA], proj[:, COL_BG:COL_G], proj[:, COL_BA:COL_BA + LANES]],
                              axis=1)
        dec = jnp.pad(dec.reshape(batch, seq, -1), ((0, 0), (0, DEC_PAD - seq), (0, 0)))
        dec = dec.reshape(batch * DEC_PAD, -1)
        gate_col = COL_BA - COL_BQ
        dec_cols = (0, COL_BK - COL_BQ, COL_BV - COL_BQ, gate_col, gate_col + COL_G - COL_BG)
        unpad = lambda o: o.reshape(batch, DEC_PAD, -1)[:, :seq].reshape(m, -1)
        o_b, s_gla = gla_branch(dec, dec_cols, lw["gla_a2"], lw["gla_a_bias"], lw["gla_norm_g"], st["gla"],
                                layer, batch, DEC_PAD, chunk=DEC_PAD, valid=seq, nseq=DEC_SEQS)
        o_c, s_hgrn = hgrn_branch(dec, gate_col + W_B, lw["lb"], lw["hgrn_norm_g"], st["hgrn"], layer,
                                  batch, DEC_PAD, chunk=DEC_PAD, valid=seq, nseq=DEC_SEQS)
        o_b, o_c = unpad(o_b), unpad(o_c)

    tm = 1024 if m % 1024 == 0 else m
    mtm = MERGE_TM if m % MERGE_TM == 0 else m
    merged = merge(y_a, g, o_b, o_c, lw["p_rwkv"], lw["p_gla"], lw["p_hgrn"], proj, tm=mtm, tn=MERGE_TN)
    mix = matmul(merged, lw["w_out"], tm=tm, tn=1024, name="w_out")
    x, h = add_rmsnorm_rmsnorm(x, mix, lw["norm_mix_post"], lw["norm_ffn_pre"], tm=NORM_TILE)
    s_shift = _last_rows(proj, batch, seq, 0)[:, :A_PROJ]
    return x, h, dict(shift=s_shift, rwkv=s_rwkv, gla=s_gla, hgrn=s_hgrn)


def kernel(x_prompt, x_sample, state_rwkv_shift, state_rwkv, state_gla, state_hgrn, state_ffn_conv, norm_mix_pre, norm_mix_post, norm_ffn_pre, norm_ffn_post, w_in, rwkv_mu, rwkv_w0, rwkv_w2, rwkv_a0, rwkv_a2, rwkv_g2, rwkv_k_k, rwkv_k_a, rwkv_r_k, rwkv_ln_g, rwkv_ln_b, gla_a2, gla_a_bias, gla_norm_g, hgrn_lb_logits, hgrn_norm_g, p_rwkv, p_gla, p_hgrn, w_out, ffn_up, ffn_gate, ffn_conv_w, ffn_conv_b, ffn_down):
    bp, tp, _ = x_prompt.shape
    bs, ts, _ = x_sample.shape
    sdt = state_rwkv.dtype

    probs = jax.nn.softmax(hgrn_lb_logits.astype(F32), axis=0)
    lb_all = jnp.cumsum(probs, axis=0) - probs[0]

    xp = x_prompt.reshape(bp * tp, D_MODEL)
    xs = x_sample.reshape(bs * ts, D_MODEL)
    names = ("shift", "rwkv", "gla", "hgrn", "conv")
    new_p = {n: [] for n in names}
    new_s = {n: [] for n in names}
    per_layer = dict(norm_mix_pre=norm_mix_pre, norm_mix_post=norm_mix_post, norm_ffn_pre=norm_ffn_pre,
                     norm_ffn_post=norm_ffn_post, rwkv_mu=rwkv_mu, rwkv_w0=rwkv_w0, rwkv_w2=rwkv_w2,
                     rwkv_a0=rwkv_a0, rwkv_a2=rwkv_a2, rwkv_g2=rwkv_g2, rwkv_k_k=rwkv_k_k, rwkv_k_a=rwkv_k_a,
                     rwkv_r_k=rwkv_r_k, rwkv_ln_g=rwkv_ln_g, rwkv_ln_b=rwkv_ln_b, gla_a2=gla_a2,
                     gla_a_bias=gla_a_bias, gla_norm_g=gla_norm_g, hgrn_norm_g=hgrn_norm_g,
                     ffn_conv_w=ffn_conv_w, ffn_conv_b=ffn_conv_b, lb=lb_all)
    w_in_t = jnp.swapaxes(w_in.astype(F32), 1, 2)
    ffn_up = ffn_up.astype(F32)
    ffn_gate = ffn_gate.astype(F32)
    st_p = dict(shift=jnp.zeros((bp, A_PROJ), F32), rwkv=jnp.zeros((bp, H_A, HS_A, HS_A), F32),
                gla=jnp.zeros((1, bp, H_B, DK_B, DV_B), F32), hgrn=jnp.zeros((1, bp, H_C, HD_C, HD_C), F32))
    conv_p = jnp.zeros((bp, CONV_W - 1, D_FF), F32)
    rwkv_s0 = jnp.transpose(state_rwkv.astype(F32), (0, 2, 3, 4, 1))
    for l in range(DEPTH):
        lw = {n: arr[l] for n, arr in per_layer.items()}
        for n, arr in (("p_rwkv", p_rwkv), ("p_gla", p_gla), ("p_hgrn", p_hgrn), ("w_out", w_out),
                       ("ffn_down", ffn_down)):
            lw[n] = arr[l].astype(BF16)
        st_s = dict(shift=state_rwkv_shift[l].astype(F32), rwkv=rwkv_s0,
                    gla=state_gla.astype(F32), hgrn=state_hgrn.astype(F32))

        if l == 0:
            hp = rmsnorm(xp, lw["norm_mix_pre"], tm=NORM_TILE, out_dtype=BF16)
            hs = rmsnorm(xs, lw["norm_mix_pre"], tm=NORM_TILE, out_dtype=BF16)
        proj_p, proj_s = proj_pair(hp, hs, w_in_t, l, tm=1024)
        xp, hp, out_p = _token_mix(xp, proj_p, bp, tp, True, st_p, 0, lw)
        xs, hs, out_s = _token_mix(xs, proj_s, bs, ts, False, st_s, l, lw)

        act_p, out_p["conv"], act_s, out_s["conv"] = ffn_up_pair(
            hp, hs, ffn_up, ffn_gate, l, conv_p, state_ffn_conv.astype(F32), lw["ffn_conv_w"], lw["ffn_conv_b"],
            bp, tp, bs, ts, tm=FFN_TM, tn=FFN_TN, sub=FFN_SUB)
        fp = matmul(act_p, lw["ffn_down"], tm=DOWN_TM, tn=512, name="ffn_down")
        fs = matmul(act_s, lw["ffn_down"], tm=DOWN_TM, tn=512, name="ffn_down")
        if l + 1 < DEPTH:
            xp, hp = add_rmsnorm_rmsnorm(xp, fp, lw["norm_ffn_post"], norm_mix_pre[l + 1], tm=NORM_TILE)
            xs, hs = add_rmsnorm_rmsnorm(xs, fs, lw["norm_ffn_post"], norm_mix_pre[l + 1], tm=NORM_TILE)
        else:
            xp = add_rmsnorm(xp, fp, lw["norm_ffn_post"], tm=NORM_TILE)
            xs = add_rmsnorm(xs, fs, lw["norm_ffn_post"], tm=NORM_TILE)
        for n in names:
            new_p[n].append(out_p[n])
            new_s[n].append(out_s[n])

    stack = lambda seq_: jnp.stack(seq_).astype(sdt)
    out_s = {n: stack(new_s[n]) for n in names}
    out_s["rwkv"] = jnp.transpose(out_s["rwkv"], (0, 4, 1, 2, 3))
    return (xp.reshape(bp, tp, D_MODEL), xs.reshape(bs, ts, D_MODEL),
            *(stack(new_p[n]) for n in names), *(out_s[n] for n in names))
```

```python
import functools

import jax
import jax.numpy as jnp
from jax import lax
from jax.experimental import pallas as pl
from jax.experimental.pallas import tpu as pltpu

F32 = jnp.float32
BF16 = jnp.bfloat16

D_MODEL = 4096
DEPTH = 2
HS_A = 64
W_A = 3 * D_MODEL // 8
H_A = W_A // HS_A
R_W, R_A, R_G = 128, 128, 256
A_PROJ = 3 * W_A + R_W + R_A + R_G
RWKV_LN_EPS = 64e-5
W_B = 5 * D_MODEL // 16
DV_B = 256
H_B = W_B // DV_B
DK_B = DV_B // 2
K_B = H_B * DK_B
R_GLA = 16
GLA_TAU = 16.0
W_C = 5 * D_MODEL // 16
HD_C = 128
H_C = W_C // HD_C
D_FF = 2 * D_MODEL
CONV_W = 3
RMS_EPS = 1e-6
HEAD_NORM_EPS = 1e-5

LANES = 128
SUBLANES = 8
VMEM_LIMIT_BYTES = 56 * 1024 * 1024

IN_COLS = A_PROJ + 2 * K_B + 2 * W_B + R_GLA + 4 * W_C + 3 * D_MODEL
COL_BQ = A_PROJ
COL_BK = COL_BQ + K_B
COL_BV = COL_BK + K_B
COL_BA = COL_BV + W_B
TAIL_SRC = COL_BA + R_GLA
PROJ_TN = 512
HEAD_TILES = -(-TAIL_SRC // PROJ_TN)
COL_BG = 7 * W_B
PROJ_COLS = COL_BG + IN_COLS - TAIL_SRC
TAIL_TILES = -(-(IN_COLS - TAIL_SRC) // PROJ_TN)
TAIL_TILE0 = PROJ_COLS // PROJ_TN - TAIL_TILES
TAIL_ROW0 = IN_COLS - TAIL_TILES * PROJ_TN
COL_C = COL_BG + W_B
COL_G = COL_C + 4 * W_C
assert PROJ_COLS % PROJ_TN == 0 and TAIL_TILE0 >= HEAD_TILES and TAIL_ROW0 % SUBLANES == 0
assert COL_BG % W_B == 0 and COL_C % W_C == 0 and COL_BA % LANES == 0


def _cparams(semantics):
    return pltpu.CompilerParams(dimension_semantics=semantics, vmem_limit_bytes=VMEM_LIMIT_BYTES)


def _rmsnorm_kernel(x_ref, g_ref, o_ref):
    x = x_ref[...]
    ms = jnp.mean(x * x, axis=-1, keepdims=True)
    o_ref[...] = (x * lax.rsqrt(ms + RMS_EPS) * g_ref[...]).astype(o_ref.dtype)


def rmsnorm(x, g, *, tm, out_dtype):
    m, d = x.shape
    return pl.pallas_call(
        _rmsnorm_kernel,
        grid=(m // tm,),
        in_specs=[pl.BlockSpec((tm, d), lambda i: (i, 0)), pl.BlockSpec((1, d), lambda i: (0, 0))],
        out_specs=pl.BlockSpec((tm, d), lambda i: (i, 0)),
        out_shape=jax.ShapeDtypeStruct((m, d), out_dtype),
        compiler_params=_cparams(("parallel",)),
        name="rmsnorm",
    )(x, g.reshape(1, d))


def _add_rmsnorm_kernel(x_ref, y_ref, g_ref, o_ref):
    y = y_ref[...]
    ms = jnp.mean(y * y, axis=-1, keepdims=True)
    o_ref[...] = x_ref[...] + y * lax.rsqrt(ms + RMS_EPS) * g_ref[...]


def add_rmsnorm(x, y, g, *, tm):
    m, d = x.shape
    return pl.pallas_call(
        _add_rmsnorm_kernel,
        grid=(m // tm,),
        in_specs=[pl.BlockSpec((tm, d), lambda i: (i, 0)), pl.BlockSpec((tm, d), lambda i: (i, 0)),
                  pl.BlockSpec((1, d), lambda i: (0, 0))],
        out_specs=pl.BlockSpec((tm, d), lambda i: (i, 0)),
        out_shape=jax.ShapeDtypeStruct((m, d), F32),
        compiler_params=_cparams(("parallel",)),
        name="add_rmsnorm",
    )(x, y, g.reshape(1, d))


def _add_rmsnorm_rmsnorm_kernel(x_ref, y_ref, g_ref, gn_ref, o_ref, h_ref):
    y = y_ref[...]
    ms = jnp.mean(y * y, axis=-1, keepdims=True)
    x = x_ref[...] + y * lax.rsqrt(ms + RMS_EPS) * g_ref[...]
    o_ref[...] = x
    ms = jnp.mean(x * x, axis=-1, keepdims=True)
    h_ref[...] = (x * lax.rsqrt(ms + RMS_EPS) * gn_ref[...]).astype(h_ref.dtype)


def add_rmsnorm_rmsnorm(x, y, g, g_next, *, tm):
    m, d = x.shape
    row = pl.BlockSpec((tm, d), lambda i: (i, 0))
    vec = pl.BlockSpec((1, d), lambda i: (0, 0))
    return pl.pallas_call(
        _add_rmsnorm_rmsnorm_kernel,
        grid=(m // tm,),
        in_specs=[row, row, vec, vec],
        out_specs=[row, row],
        out_shape=[jax.ShapeDtypeStruct((m, d), F32), jax.ShapeDtypeStruct((m, d), BF16)],
        compiler_params=_cparams(("parallel",)),
        name="add_rmsnorm_rmsnorm",
    )(x, y, g.reshape(1, d), g_next.reshape(1, d))


def _matmul_kernel(x_ref, w_ref, o_ref):
    o_ref[...] = jnp.dot(x_ref[...], w_ref[...], preferred_element_type=F32).astype(o_ref.dtype)


def matmul(x, w, *, tm, tn, out_dtype=F32, name="matmul"):
    m, k = x.shape
    n = w.shape[1]
    return pl.pallas_call(
        _matmul_kernel,
        grid=(m // tm, n // tn),
        in_specs=[pl.BlockSpec((tm, k), lambda i, j: (i, 0)), pl.BlockSpec((k, tn), lambda i, j: (0, j))],
        out_specs=pl.BlockSpec((tm, tn), lambda i, j: (i, j)),
        out_shape=jax.ShapeDtypeStruct((m, n), out_dtype),
        compiler_params=_cparams(("parallel", "parallel")),
        name=name,
    )(x, w)


def _proj_pair_kernel(xp_ref, xs_ref, wt_ref, op_ref, os_ref, w_bf, *, prompt_tiles):
    i = pl.program_id(1)

    @pl.when(i == 0)
    def _():
        w_bf[...] = wt_ref[...].astype(BF16)

    contract_last = (((1,), (1,)), ((), ()))

    @pl.when(i < prompt_tiles)
    def _():
        op_ref[...] = lax.dot_general(xp_ref[...], w_bf[...], contract_last, preferred_element_type=F32)

    @pl.when(i == prompt_tiles)
    def _():
        os_ref[...] = lax.dot_general(xs_ref[...], w_bf[...], contract_last, preferred_element_type=F32)


def proj_pair(xp, xs, w_in_t, layer, *, tm):
    mp, k = xp.shape
    ms = xs.shape[0]
    pt = mp // tm
    prow = lambda i: jnp.minimum(i, pt - 1)
    out_col = lambda j: jnp.where(j < HEAD_TILES, j, j + (TAIL_TILE0 - HEAD_TILES))
    w_row = lambda j: pl.multiple_of(
        jnp.where(j < HEAD_TILES, j * PROJ_TN, TAIL_ROW0 + (j - HEAD_TILES) * PROJ_TN), SUBLANES)
    return pl.pallas_call(
        functools.partial(_proj_pair_kernel, prompt_tiles=pt),
        grid=(HEAD_TILES + TAIL_TILES, pt + 1),
        in_specs=[pl.BlockSpec((tm, k), lambda j, i: (prow(i), 0)),
                  pl.BlockSpec((ms, k), lambda j, i: (0, 0), pipeline_mode=pl.Buffered(1)),
                  pl.BlockSpec((None, pl.Element(PROJ_TN), pl.Element(k)), lambda j, i: (layer, w_row(j), 0))],
        out_specs=[pl.BlockSpec((tm, PROJ_TN), lambda j, i: (prow(i), out_col(j))),
                   pl.BlockSpec((ms, PROJ_TN), lambda j, i: (0, out_col(j)))],
        out_shape=[jax.ShapeDtypeStruct((mp, PROJ_COLS), F32), jax.ShapeDtypeStruct((ms, PROJ_COLS), F32)],
        scratch_shapes=[pltpu.VMEM((PROJ_TN, k), BF16)],
        compiler_params=_cparams(("arbitrary", "arbitrary")),
        name="proj_pair",
    )(xp, xs, w_in_t)


def _merge_kernel(y_ref, g_ref, ob_ref, oc_ref, pa_ref, pb_ref, pc_ref, ga_ref, gb_ref, gc_ref, o_ref, oa_scr):
    @pl.when(pl.program_id(1) == 0)
    def _():
        oa_scr[...] = (y_ref[...] * g_ref[...]).astype(BF16)

    ma = jnp.dot(oa_scr[...], pa_ref[...], preferred_element_type=F32)
    mb = jnp.dot(ob_ref[...], pb_ref[...], preferred_element_type=F32)
    mc = jnp.dot(oc_ref[...], pc_ref[...], preferred_element_type=F32)
    merged = (jax.nn.sigmoid(ga_ref[...]) * ma + jax.nn.sigmoid(gb_ref[...]) * mb
              + jax.nn.sigmoid(gc_ref[...]) * mc)
    o_ref[...] = merged.astype(o_ref.dtype)


def merge(y_a, g_a, o_b, o_c, p_a, p_b, p_c, proj, *, tm, tn):
    m = y_a.shape[0]
    d = p_a.shape[1]
    gate0 = COL_G // tn
    gstep = d // tn

    def row(width):
        return pl.BlockSpec((tm, width), lambda i, j: (i, 0))

    def wcol(kdim):
        return pl.BlockSpec((kdim, tn), lambda i, j: (0, j))

    def gate(branch):
        return pl.BlockSpec((tm, tn), lambda i, j: (i, gate0 + branch * gstep + j))

    return pl.pallas_call(
        _merge_kernel,
        grid=(m // tm, d // tn),
        in_specs=[row(W_A), row(W_A), row(W_B), row(W_C), wcol(W_A), wcol(W_B), wcol(W_C),
                  gate(0), gate(1), gate(2)],
        out_specs=pl.BlockSpec((tm, tn), lambda i, j: (i, j)),
        out_shape=jax.ShapeDtypeStruct((m, d), BF16),
        scratch_shapes=[pltpu.VMEM((tm, W_A), BF16)],
        compiler_params=_cparams(("parallel", "arbitrary")),
        name="merge",
    )(y_a, g_a, o_b, o_c, p_a, p_b, p_c, proj, proj, proj)


def _softplus(x):
    return jnp.maximum(x, 0.0) + jnp.log(1.0 + jnp.exp(-jnp.abs(x)))


def _rwkv_gates(p, prev, mu_ref, w0_ref, w2_ref, a0_ref, a2_ref, g2_ref, r_ref, d_ref, k_ref, v_ref, a_ref, g_ref):
    ps = p + mu_ref[...] * (prev - p)
    o1 = 3 * W_A
    r_ref[...] = ps[:, :W_A]
    k_ref[...] = ps[:, W_A:2 * W_A]
    v_ref[...] = ps[:, 2 * W_A:o1]
    wc = ps[:, o1:o1 + R_W]
    ac = ps[:, o1 + R_W:o1 + R_W + R_A]
    gc = ps[:, o1 + R_W + R_A:]
    wl = w0_ref[...] + jnp.dot(jnp.tanh(wc).astype(BF16), w2_ref[...], preferred_element_type=F32)
    w = -_softplus(-wl) - 0.5
    d_ref[...] = jnp.exp(-jnp.exp(w))
    al = a0_ref[...] + jnp.dot(ac.astype(BF16), a2_ref[...], preferred_element_type=F32)
    a_ref[...] = jax.nn.sigmoid(al)
    g_ref[...] = jnp.dot(jax.nn.sigmoid(gc).astype(BF16), g2_ref[...], preferred_element_type=F32)


def _rwkv_prep_seq_kernel(p_ref, tail_ref, shift_ref, *rest, tiles_per_seq):
    i = pl.program_id(0)
    p = p_ref[...]
    first = (i % tiles_per_seq) == 0
    before = jnp.where(first, shift_ref[0], tail_ref[SUBLANES - 1:SUBLANES, :])
    row = lax.broadcasted_iota(jnp.int32, (p.shape[0], 1), 0)
    prev = jnp.where(row >= 1, pltpu.roll(p, 1, 0), before)
    _rwkv_gates(p, prev, *rest)


def _rwkv_prep_rows_kernel(p_ref, prev_ref, *rest):
    _rwkv_gates(p_ref[...], prev_ref[...], *rest)


def _rwkv_prep_call(kernel, lead_specs, lead_args, m, tm, mu, w0, w2, a0, a2, g2, name):
    full = lambda shape: pl.BlockSpec(shape, lambda i: (0, 0))
    out = jax.ShapeDtypeStruct((m, W_A), F32)
    return pl.pallas_call(
        kernel,
        grid=(m // tm,),
        in_specs=lead_specs + [full((1, A_PROJ)), full((1, W_A)), full((R_W, W_A)), full((1, W_A)),
                               full((R_A, W_A)), full((R_G, W_A))],
        out_specs=[pl.BlockSpec((tm, W_A), lambda i: (i, 0))] * 6,
        out_shape=[out] * 6,
        compiler_params=_cparams(("parallel",)),
        name=name,
    )(*lead_args, mu.reshape(1, A_PROJ), w0.reshape(1, W_A), w2.astype(BF16), a0.reshape(1, W_A),
      a2.astype(BF16), g2.astype(BF16))


def rwkv_prep_seq(proj, shift, weights, batch, seq, *, tm):
    tiles_per_seq = seq // tm
    nblk = tm // SUBLANES
    specs = [pl.BlockSpec((tm, A_PROJ), lambda i: (i, 0)),
             pl.BlockSpec((SUBLANES, A_PROJ), lambda i: (jnp.maximum(i * nblk - 1, 0), 0)),
             pl.BlockSpec((1, 1, A_PROJ), lambda i: (i // tiles_per_seq, 0, 0))]
    kern = functools.partial(_rwkv_prep_seq_kernel, tiles_per_seq=tiles_per_seq)
    return _rwkv_prep_call(kern, specs, (proj, proj, shift.reshape(batch, 1, A_PROJ)), batch * seq, tm,
                           *weights, name="rwkv_prep_seq")


def rwkv_prep_rows(proj, prev, weights, *, tm):
    specs = [pl.BlockSpec((tm, A_PROJ), lambda i: (i, 0)), pl.BlockSpec((tm, A_PROJ), lambda i: (i, 0))]
    return _rwkv_prep_call(_rwkv_prep_rows_kernel, specs, (proj, prev), proj.shape[0], tm, *weights,
                           name="rwkv_prep_rows")


def _rwkv_scan_kernel(r_ref, d_ref, k_ref, v_ref, a_ref, s0_ref, kk_ref, ka_ref, rk_ref, lng_ref, lnb_ref,
                      y_ref, st_ref, s_scr, o_scr, t_scr, b_scr, *, steps, tsub):
    tc = pl.program_id(1)

    @pl.when(tc == 0)
    def _():
        s_scr[...] = s0_ref[...]
        o_scr[...] = jnp.zeros_like(o_scr)
        b_scr[...] = jnp.zeros_like(b_scr)

    row = lax.broadcasted_iota(jnp.int32, (SUBLANES, 1), 0)

    def channel_rows(t, c0, n):
        tile, pos = t // tsub, lax.rem(t, tsub)
        return pl.ds((tile * HS_A + c0) * tsub + pos, n, stride=tsub)

    def prepare(t, slot):
        chans = channel_rows(t, 0, HS_A)
        r = r_ref[chans, :]
        k = k_ref[chans, :]
        a = a_ref[chans, :]
        kk = k * kk_ref[...]
        kk = kk * lax.rsqrt(jnp.maximum(jnp.sum(kk * kk, axis=0, keepdims=True), 1e-24))
        kp = k * (1.0 + (a - 1.0) * ka_ref[...])
        t_scr[slot, 0] = kk
        t_scr[slot, 1] = d_ref[chans, :]
        t_scr[slot, 2] = kk * a
        t_scr[slot, 3] = kp
        t_scr[slot, 4] = r
        bonus = jnp.sum(r * kp * rk_ref[...], axis=0, keepdims=True)
        b_scr[slot] = jnp.broadcast_to(bonus, b_scr.shape[1:])

    def update(t, slot):
        kk, d, kka, kp, r = (t_scr[slot, j] for j in range(5))
        for vo in range(HS_A // SUBLANES):
            v8 = v_ref[channel_rows(t, vo * SUBLANES, SUBLANES), :]
            o8 = jnp.zeros_like(v8)
            for vi in range(SUBLANES):
                s = s_scr[vo, vi]
                skk = jnp.sum(s * kk, axis=0, keepdims=True)
                s = s * d - skk * kka + v8[vi:vi + 1, :] * kp
                s_scr[vo, vi] = s
                o8 = jnp.where(row == vi, jnp.sum(s * r, axis=0, keepdims=True), o8)
            o_scr[slot, vo * SUBLANES:(vo + 1) * SUBLANES, :] = o8

    def finish(t, slot):
        chans = channel_rows(t, 0, HS_A)
        o = o_scr[slot]
        mean = jnp.mean(o, axis=0, keepdims=True)
        var = jnp.mean(jnp.square(o - mean), axis=0, keepdims=True)
        y = (o - mean) * lax.rsqrt(var + RWKV_LN_EPS) * lng_ref[...] + lnb_ref[...]
        y_ref[chans, :] = y + b_scr[slot, 0:1, :] * v_ref[chans, :]

    prepare(0, 0)

    def step(t, carry):
        slot = lax.rem(t, 2)
        other = 1 - slot
        finish(jnp.maximum(t - 1, 0), other)
        update(t, slot)
        prepare(jnp.minimum(t + 1, steps - 1), other)
        return carry

    lax.fori_loop(0, steps, step, 0)
    finish(steps - 1, (steps - 1) % 2)

    @pl.when(tc == pl.num_programs(1) - 1)
    def _():
        st_ref[...] = s_scr[...]


def rwkv_scan(r, d, k, v, a, s0, layer, kk_p, ka_p, rk_p, lng_p, lnb_p, *, steps, tsub):
    rows, lanes = r.shape
    c = HS_A
    nl = lanes // LANES
    vo = c // SUBLANES
    s0 = s0.reshape(s0.shape[0], nl, vo, SUBLANES, c, LANES)
    tok = pl.BlockSpec((steps * c, LANES), lambda l, i: (i, l))
    par = pl.BlockSpec((c, LANES), lambda l, i: (0, l))
    st_in = pl.BlockSpec((None, None, vo, SUBLANES, c, LANES), lambda l, i: (layer, l, 0, 0, 0, 0))
    st_out = pl.BlockSpec((None, vo, SUBLANES, c, LANES), lambda l, i: (l, 0, 0, 0, 0))
    y, s_t = pl.pallas_call(
        functools.partial(_rwkv_scan_kernel, steps=steps, tsub=tsub),
        grid=(nl, rows // (steps * c)),
        in_specs=[tok] * 5 + [st_in] + [par] * 5,
        out_specs=[tok, st_out],
        out_shape=[jax.ShapeDtypeStruct((rows, lanes), F32),
                   jax.ShapeDtypeStruct((nl, vo, SUBLANES, c, LANES), F32)],
        scratch_shapes=[pltpu.VMEM((vo, SUBLANES, c, LANES), F32), pltpu.VMEM((2, c, LANES), F32),
                        pltpu.VMEM((2, 5, c, LANES), F32), pltpu.VMEM((2, SUBLANES, LANES), F32)],
        compiler_params=_cparams(("parallel", "arbitrary")),
        name="rwkv_scan",
    )(r, d, k, v, a, s0, kk_p, ka_p, rk_p, lng_p, lnb_p)
    return y, s_t.reshape(nl, c, c, LANES)


LANE_SLOTS = 32
RELAYOUT_STEPS = 128


HEAD_PITCH = HS_A + SUBLANES


def _to_lane_tiles_kernel(x_ref, o_ref, xt_scr):
    nb, tt, _ = x_ref.shape
    for b in range(nb):
        xt = x_ref[b].T
        for h in range(H_A):
            xt_scr[b, h * HEAD_PITCH:h * HEAD_PITCH + HS_A, :] = xt[h * HS_A:(h + 1) * HS_A, :]
    idle = jnp.zeros((LANE_SLOTS - H_A, tt), F32)
    for c in range(HS_A):
        parts = []
        for b in range(nb):
            parts += [xt_scr[b, pl.ds(c, H_A, stride=HEAD_PITCH), :], idle]
        tile = jnp.concatenate(parts, axis=0).T
        o_ref[:, c, :, :] = tile.reshape(tt // SUBLANES, SUBLANES, LANES)


def _from_lane_tiles_kernel(y_ref, o_ref, xt_scr):
    nb, tt, _ = o_ref.shape
    for c in range(HS_A):
        heads = y_ref[:, c, :, :].reshape(tt, LANES).T
        for b in range(nb):
            xt_scr[b, pl.ds(c, H_A, stride=HEAD_PITCH), :] = heads[b * LANE_SLOTS:b * LANE_SLOTS + H_A, :]
    for b in range(nb):
        xt = jnp.concatenate([xt_scr[b, h * HEAD_PITCH:h * HEAD_PITCH + HS_A, :] for h in range(H_A)], axis=0)
        o_ref[b] = xt.T


def to_lane_tiles(x, batch, seq):
    tt = RELAYOUT_STEPS
    out = pl.pallas_call(
        _to_lane_tiles_kernel,
        grid=(seq // tt,),
        in_specs=[pl.BlockSpec((batch, tt, W_A), lambda i: (0, i, 0))],
        out_specs=pl.BlockSpec((tt // SUBLANES, HS_A, SUBLANES, LANES), lambda i: (i, 0, 0, 0)),
        out_shape=jax.ShapeDtypeStruct((seq // SUBLANES, HS_A, SUBLANES, LANES), F32),
        scratch_shapes=[pltpu.VMEM((batch, H_A * HEAD_PITCH, tt), F32)],
        compiler_params=_cparams(("parallel",)),
        name="to_lane_tiles",
    )(x.reshape(batch, seq, W_A))
    return out.reshape(seq * HS_A, LANES)


def from_lane_tiles(y, batch, seq):
    tt = RELAYOUT_STEPS
    out = pl.pallas_call(
        _from_lane_tiles_kernel,
        grid=(seq // tt,),
        in_specs=[pl.BlockSpec((tt // SUBLANES, HS_A, SUBLANES, LANES), lambda i: (i, 0, 0, 0))],
        out_specs=pl.BlockSpec((batch, tt, W_A), lambda i: (0, i, 0)),
        out_shape=jax.ShapeDtypeStruct((batch, seq, W_A), F32),
        scratch_shapes=[pltpu.VMEM((batch, H_A * HEAD_PITCH, tt), F32)],
        compiler_params=_cparams(("parallel",)),
        name="from_lane_tiles",
    )(y.reshape(seq // SUBLANES, HS_A, SUBLANES, LANES))
    return out.reshape(batch * seq, W_A)


def rwkv_recurrence_long(r, d, k, v, a, s0, params, batch, seq, *, steps):
    idle = LANE_SLOTS - H_A

    def param_tile(p):
        return jnp.tile(jnp.pad(p.reshape(H_A, HS_A).T, ((0, 0), (0, idle))), (1, batch))

    s0l = jnp.pad(jnp.transpose(s0, (2, 3, 0, 1)), ((0, 0), (0, 0), (0, 0), (0, idle)))
    s0l = s0l.reshape(1, 1, HS_A, HS_A, LANES)
    y, s_t = rwkv_scan(*(to_lane_tiles(z, batch, seq) for z in (r, d, k, v, a)), s0l, 0,
                       *(param_tile(p) for p in params), steps=steps, tsub=SUBLANES)
    s_t = s_t.reshape(HS_A, HS_A, batch, LANE_SLOTS)[..., :H_A]
    return from_lane_tiles(y, batch, seq), jnp.transpose(s_t, (2, 3, 0, 1))


def rwkv_recurrence_short(r, d, k, v, a, s0, layer, params, batch, seq):
    def lanes_of(z):
        z = jnp.transpose(z.reshape(batch, seq, H_A, HS_A), (3, 1, 2, 0))
        return z.reshape(HS_A * seq, H_A * batch)

    def param_tile(p):
        return jnp.repeat(p.reshape(H_A, HS_A).T, batch, axis=1)

    y, s_t = rwkv_scan(*(lanes_of(z) for z in (r, d, k, v, a)), s0, layer, *(param_tile(p) for p in params),
                       steps=seq, tsub=seq)
    y = jnp.transpose(y.reshape(HS_A, seq, H_A, batch), (3, 1, 2, 0)).reshape(batch * seq, W_A)
    return y, s_t


def _block_prefix_sum(x):
    row = lax.broadcasted_iota(jnp.int32, (SUBLANES, 1), 0)
    for shift in (1, 2, 4):
        x = x + jnp.where(row >= shift, pltpu.roll(x, shift, 0), 0.0)
    return x


def _chunk_scores(q, k, blocks):
    c = q.shape[0]
    nb = c // SUBLANES
    row = lax.broadcasted_iota(jnp.int32, (SUBLANES, 1), 0)
    lane = lax.broadcasted_iota(jnp.int32, (SUBLANES, c), 1)
    rows = []
    for ib in range(nb):
        lo = ib * SUBLANES
        qi = q[lo:lo + SUBLANES, :]
        ki = k[lo:lo + SUBLANES, :]
        bi = blocks[ib]
        att = jnp.zeros((SUBLANES, c), F32)
        for j in range(SUBLANES):
            decay = jnp.exp(jnp.minimum(bi - bi[j:j + 1, :], 0.0))
            col = jnp.sum(qi * ki[j:j + 1, :] * decay, axis=-1, keepdims=True)
            att = jnp.where(lane == lo + j, col, att)
        att = jnp.where(lane <= lo + row, att, 0.0)
        if ib > 0:
            ref = blocks[ib - 1][SUBLANES - 1:SUBLANES, :]
            qs = (qi * jnp.exp(bi - ref)).astype(BF16)
            ks = [k[jb * SUBLANES:(jb + 1) * SUBLANES, :] * jnp.exp(ref - blocks[jb]) for jb in range(ib)]
            ks.append(jnp.zeros((c - lo, q.shape[1]), F32))
            ks = jnp.concatenate(ks, axis=0).astype(BF16)
            att = att + lax.dot_general(qs, ks, (((1,), (1,)), ((), ())), preferred_element_type=F32)
        rows.append(att)
    return jnp.concatenate(rows, axis=0)


def _gated_chunk(q, k, v, log_a, s_prev):
    nb = q.shape[0] // SUBLANES
    blocks = []
    carry = None
    for ib in range(nb):
        bi = _block_prefix_sum(log_a[ib * SUBLANES:(ib + 1) * SUBLANES, :])
        if carry is not None:
            bi = bi + carry
        carry = bi[SUBLANES - 1:SUBLANES, :]
        blocks.append(bi)
    b = jnp.concatenate(blocks, axis=0)
    b_last = carry
    att = _chunk_scores(q, k, blocks)
    v_bf = v.astype(BF16)
    o = jnp.dot(att.astype(BF16), v_bf, preferred_element_type=F32)
    qe = (q * jnp.exp(b)).astype(BF16)
    o = o + jnp.dot(qe, s_prev.astype(BF16), preferred_element_type=F32)
    kd = (k * jnp.exp(b_last - b)).astype(BF16)
    e_col = jnp.transpose(jnp.broadcast_to(jnp.exp(b_last), (SUBLANES, q.shape[1])))[:, 0:1]
    s_new = s_prev * e_col + lax.dot_general(kd, v_bf, (((0,), (0,)), ((), ())), preferred_element_type=F32)
    return o, s_new


def _head_rms(o, g):
    return o * lax.rsqrt(jnp.mean(o * o, axis=-1, keepdims=True) + HEAD_NORM_EPS) * g


def _mask_padding(log_decay, valid, chunk):
    if valid >= chunk:
        return log_decay
    row = lax.broadcasted_iota(jnp.int32, (log_decay.shape[0], 1), 0)
    return jnp.where(lax.rem(row, chunk) < valid, log_decay, 0.0)


def _gla_kernel(q_ref, k_ref, v_ref, rg_ref, ac_ref, a2_ref, ab_ref, ng_ref, s0_ref, *rest, valid, chunk):
    o_ref, st_ref = rest[-2:]

    @pl.when(pl.program_id(1) == 0)
    def _():
        st_ref[...] = s0_ref[...]

    gl = jnp.dot(ac_ref[...].astype(BF16), a2_ref[...], preferred_element_type=F32) + ab_ref[...]
    log_a = _mask_padding(-_softplus(-gl) / GLA_TAU, valid, chunk)
    q = q_ref[...] * (DK_B ** -0.5)
    k = k_ref[...]
    v = v_ref[...]
    rg = rg_ref[...]
    for s in range(st_ref.shape[0]):
        rs = slice(s * chunk, (s + 1) * chunk)
        for h in range(H_B):
            ks = slice(h * DK_B, (h + 1) * DK_B)
            vs = slice(h * DV_B, (h + 1) * DV_B)
            o, s_new = _gated_chunk(q[rs, ks], k[rs, ks], v[rs, vs], log_a[rs, ks], st_ref[s, h])
            st_ref[s, h] = s_new
            o_ref[rs, vs] = (_head_rms(o, ng_ref[...]) * jax.nn.silu(rg[rs, vs])).astype(o_ref.dtype)


def _hgrn_kernel(q_ref, z_ref, i_ref, g_ref, lb_ref, ng_ref, s0_ref, *rest, valid, chunk):
    o_ref, st_ref = rest[-2:]

    @pl.when(pl.program_id(1) == 0)
    def _():
        st_ref[...] = s0_ref[...]

    q = jax.nn.silu(q_ref[...])
    lb = lb_ref[...]
    f = lb + (1.0 - lb) * jax.nn.sigmoid(z_ref[...])
    log_f = _mask_padding(jnp.log(jnp.maximum(f, 1e-30)), valid, chunk)
    k = 1.0 - f
    v = i_ref[...]
    g = g_ref[...]
    for s in range(st_ref.shape[0]):
        rs = slice(s * chunk, (s + 1) * chunk)
        for h in range(H_C):
            hs = slice(h * HD_C, (h + 1) * HD_C)
            o, s_new = _gated_chunk(q[rs, hs], k[rs, hs], v[rs, hs], log_f[rs, hs], st_ref[s, h])
            st_ref[s, h] = s_new
            o_ref[rs, hs] = (_head_rms(o, ng_ref[...]) * jax.nn.sigmoid(g[rs, hs])).astype(o_ref.dtype)


def _seq_spec(rows, width, col_block, nchunks):
    return pl.BlockSpec((rows, width), lambda b, c: (b * nchunks + c, col_block))


def _const_spec(shape):
    return pl.BlockSpec(shape, lambda b, c: (0,) * len(shape))


def _state_in_spec(layer, nseq, tail):
    return pl.BlockSpec((None, nseq) + tail, lambda b, c: (layer, b) + (0,) * len(tail))


def _mixer_call(kernel, in_specs, args, width, state_tail, nseq, batch, seq, nch, layer, out_states, name):
    rows = seq // nch * nseq
    in_specs = list(in_specs)
    args = list(args)
    aliases = {}
    if out_states is not None:
        aliases[len(args)] = 1
        in_specs.append(pl.BlockSpec(memory_space=pl.ANY))
        args.append(out_states)
    state_out = pl.BlockSpec((None, nseq) + state_tail, lambda b, c: (layer, b) + (0,) * len(state_tail))
    return pl.pallas_call(
        kernel,
        grid=(batch // nseq, nch),
        in_specs=in_specs,
        out_specs=[pl.BlockSpec((rows, width), lambda b, c: (b * nch + c, 0)), state_out],
        out_shape=[jax.ShapeDtypeStruct((batch * seq, width), BF16),
                   jax.ShapeDtypeStruct((DEPTH, batch) + state_tail, F32)],
        input_output_aliases=aliases,
        compiler_params=_cparams(("parallel", "arbitrary")),
        name=name,
    )(*args)


def gla_branch(src, cols, a2, a_bias, norm_g, s0, s0_layer, layer, out_states, batch, seq, *, chunk,
               valid=None, nseq=1):
    nch = seq // chunk
    assert nseq == 1 or nch == 1
    valid = chunk if valid is None else valid
    cq, ck, cv, cg, ca = cols
    rows = chunk * nseq
    a2p = jnp.zeros((LANES, K_B), F32).at[:R_GLA].set(a2).astype(BF16)
    tail = (H_B, DK_B, DV_B)
    in_specs = [_seq_spec(rows, K_B, cq // K_B, nch), _seq_spec(rows, K_B, ck // K_B, nch),
                _seq_spec(rows, W_B, cv // W_B, nch), _seq_spec(rows, W_B, cg // W_B, nch),
                _seq_spec(rows, LANES, ca // LANES, nch),
                _const_spec((LANES, K_B)), _const_spec((1, K_B)), _const_spec((1, DV_B)),
                _state_in_spec(s0_layer, nseq, tail)]
    args = (src, src, src, src, src, a2p, a_bias.reshape(1, K_B), norm_g.reshape(1, DV_B), s0)
    return _mixer_call(functools.partial(_gla_kernel, valid=valid, chunk=chunk), in_specs, args, W_B, tail,
                       nseq, batch, seq, nch, layer, out_states, "gla")


def hgrn_branch(src, col0, lb, norm_g, s0, s0_layer, layer, out_states, batch, seq, *, chunk, valid=None,
                nseq=1):
    nch = seq // chunk
    assert nseq == 1 or nch == 1
    valid = chunk if valid is None else valid
    c0 = col0 // W_C
    rows = chunk * nseq
    tail = (H_C, HD_C, HD_C)
    in_specs = [_seq_spec(rows, W_C, c0 + j, nch) for j in range(4)] + [
        _const_spec((1, W_C)), _const_spec((1, HD_C)), _state_in_spec(s0_layer, nseq, tail)]
    args = (src, src, src, src, lb.reshape(1, W_C), norm_g.reshape(1, HD_C), s0)
    return _mixer_call(functools.partial(_hgrn_kernel, valid=valid, chunk=chunk), in_specs, args, W_C, tail,
                       nseq, batch, seq, nch, layer, out_states, "hgrn")


def _conv_gelu_gate(u, u1, u2, gate, cw_ref, cb_ref):
    c = cb_ref[...] + u2 * cw_ref[0:1, :]
    c = c + u1 * cw_ref[1:2, :]
    c = c + u * cw_ref[2:3, :]
    return (jax.nn.gelu(c, approximate=True) * gate).astype(BF16)


def _select_rows(sel, x):
    return jnp.dot(sel, x, precision=lax.Precision.HIGHEST, preferred_element_type=F32)


def _ffn_up_pair_kernel(xp_ref, xs_ref, wu_ref, wg_ref, stp_ref, sts_ref, cw_ref, cb_ref,
                        actp_ref, nsp_ref, acts_ref, nss_ref, wu_bf, wg_bf, carry,
                        *, prompt_tiles, tiles_per_seq, sub, short_seq):
    i = pl.program_id(1)

    @pl.when(i == 0)
    def _():
        wu_bf[...] = wu_ref[...].astype(BF16)
        wg_bf[...] = wg_ref[...].astype(BF16)

    @pl.when(i < prompt_tiles)
    def _():
        first = (i % tiles_per_seq) == 0
        pm2 = jnp.where(first, stp_ref[0, 0:1, :], carry[0:1, :])
        pm1 = jnp.where(first, stp_ref[0, 1:2, :], carry[1:2, :])
        row = lax.broadcasted_iota(jnp.int32, (sub, 1), 0)
        for s in range(xp_ref.shape[0] // sub):
            rows = slice(s * sub, (s + 1) * sub)
            x = xp_ref[rows, :]
            u = jnp.dot(x, wu_bf[...], preferred_element_type=F32)
            gate = jnp.dot(x, wg_bf[...], preferred_element_type=F32)
            u1 = jnp.where(row >= 1, pltpu.roll(u, 1, 0), pm1)
            u2 = jnp.where(row >= 2, pltpu.roll(u, 2, 0), jnp.where(row == 1, pm1, pm2))
            actp_ref[rows, :] = _conv_gelu_gate(u, u1, u2, gate, cw_ref, cb_ref)
            pm2 = u[sub - 2:sub - 1, :]
            pm1 = u[sub - 1:sub, :]
        carry[0:1, :] = pm2
        carry[1:2, :] = pm1
        nsp_ref[0, 0:1, :] = pm2
        nsp_ref[0, 1:2, :] = pm1

    @pl.when(i == prompt_tiles)
    def _():
        x = xs_ref[...]
        u = jnp.dot(x, wu_bf[...], preferred_element_type=F32)
        gate = jnp.dot(x, wg_bf[...], preferred_element_type=F32)
        m = u.shape[0]
        nseq = m // short_seq
        r = lax.broadcasted_iota(jnp.int32, (m, nseq), 0)
        start = lax.broadcasted_iota(jnp.int32, (m, nseq), 1) * short_seq
        at0 = (r == start).astype(F32)
        at1 = (r == start + 1).astype(F32)
        st0 = sts_ref[:, 0, :]
        st1 = sts_ref[:, 1, :]
        tap1 = _select_rows(at0, st1)
        tap2 = _select_rows(at0, st0) + _select_rows(at1, st1)
        pos = lax.rem(lax.broadcasted_iota(jnp.int32, (m, 1), 0), short_seq)
        u1 = jnp.where(pos >= 1, pltpu.roll(u, 1, 0), tap1)
        u2 = jnp.where(pos >= 2, pltpu.roll(u, 2, 0), tap2)
        acts_ref[...] = _conv_gelu_gate(u, u1, u2, gate, cw_ref, cb_ref)
        rr = lax.broadcasted_iota(jnp.int32, (nseq, m), 1)
        end = lax.broadcasted_iota(jnp.int32, (nseq, m), 0) * short_seq + short_seq
        nss_ref[:, 0, :] = _select_rows((rr == end - 2).astype(F32), u)
        nss_ref[:, 1, :] = _select_rows((rr == end - 1).astype(F32), u)


def ffn_up_pair(xp, xs, w_up, w_gate, layer, conv_p, conv_s, conv_w, conv_b, bp, tp, bs, ts, *, tm, tn, sub):
    mp, k = xp.shape
    ms = xs.shape[0]
    pt = mp // tm
    tiles_per_seq = tp // tm
    prow = lambda i: jnp.minimum(i, pt - 1)
    wspec = pl.BlockSpec((None, k, tn), lambda j, i: (layer, 0, j))
    pstate = pl.BlockSpec((1, CONV_W - 1, tn), lambda j, i: (prow(i) // tiles_per_seq, 0, j))
    kern = functools.partial(_ffn_up_pair_kernel, prompt_tiles=pt, tiles_per_seq=tiles_per_seq, sub=sub,
                             short_seq=ts)
    return pl.pallas_call(
        kern,
        grid=(D_FF // tn, pt + 1),
        in_specs=[pl.BlockSpec((tm, k), lambda j, i: (prow(i), 0)),
                  pl.BlockSpec((ms, k), lambda j, i: (0, 0), pipeline_mode=pl.Buffered(1)),
                  wspec, wspec, pstate,
                  pl.BlockSpec((None, bs, CONV_W - 1, tn), lambda j, i: (layer, 0, 0, j)),
                  pl.BlockSpec((CONV_W, tn), lambda j, i: (0, j)), pl.BlockSpec((1, tn), lambda j, i: (0, j))],
        out_specs=[pl.BlockSpec((tm, tn), lambda j, i: (prow(i), j)), pstate,
                   pl.BlockSpec((ms, tn), lambda j, i: (0, j)),
                   pl.BlockSpec((bs, CONV_W - 1, tn), lambda j, i: (0, 0, j))],
        out_shape=[jax.ShapeDtypeStruct((mp, D_FF), BF16), jax.ShapeDtypeStruct((bp, CONV_W - 1, D_FF), F32),
                   jax.ShapeDtypeStruct((ms, D_FF), BF16), jax.ShapeDtypeStruct((bs, CONV_W - 1, D_FF), F32)],
        scratch_shapes=[pltpu.VMEM((k, tn), BF16), pltpu.VMEM((k, tn), BF16), pltpu.VMEM((SUBLANES, tn), F32)],
        compiler_params=_cparams(("arbitrary", "arbitrary")),
        name="ffn_up_pair",
    )(xp, xs, w_up, w_gate, conv_p, conv_s, conv_w, conv_b.reshape(1, D_FF))


NORM_TILE = 256
PREP_TILE = 128
MERGE_TM = 1024
MERGE_TN = 256
FFN_TM = 1024
FFN_SUB = 256
FFN_TN = 256
DOWN_TM = 512
SCAN_STEPS = 32
SEQ_CHUNK = 64
DEC_PAD = SUBLANES
DEC_SEQS = 1


def _shifted_rows(x, first, seq):
    w = x.shape[-1]
    xs = x.reshape(-1, seq, w)
    return jnp.concatenate([first[:, None, :], xs[:, :-1]], axis=1).reshape(-1, w)


def _last_rows(x, batch, seq, back):
    return jnp.take(x, jnp.arange(batch) * seq + (seq - 1 - back), axis=0)


def _token_mix(x, proj, batch, seq, long_seq, st, s0_layer, layer, done, lw):
    m = batch * seq
    prep_w = (lw["rwkv_mu"], lw["rwkv_w0"], lw["rwkv_w2"], lw["rwkv_a0"], lw["rwkv_a2"], lw["rwkv_g2"])
    if long_seq:
        r, d, k, v, a, g = rwkv_prep_seq(proj, st["shift"], prep_w, batch, seq, tm=PREP_TILE)
    else:
        prev = _shifted_rows(proj[:, :A_PROJ], st["shift"], seq)
        r, d, k, v, a, g = rwkv_prep_rows(proj, prev, prep_w, tm=PREP_TILE)
    params = (lw["rwkv_k_k"], lw["rwkv_k_a"], lw["rwkv_r_k"], lw["rwkv_ln_g"], lw["rwkv_ln_b"])
    if long_seq:
        y_a, s_rwkv = rwkv_recurrence_long(r, d, k, v, a, st["rwkv"], params, batch, seq, steps=SCAN_STEPS)
    else:
        y_a, s_rwkv = rwkv_recurrence_short(r, d, k, v, a, st["rwkv"], s0_layer, params, batch, seq)

    if long_seq:
        o_b, s_gla = gla_branch(proj, (COL_BQ, COL_BK, COL_BV, COL_BG, COL_BA), lw["gla_a2"], lw["gla_a_bias"],
                                lw["gla_norm_g"], st["gla"], s0_layer, layer, done["gla"], batch, seq,
                                chunk=SEQ_CHUNK)
        o_c, s_hgrn = hgrn_branch(proj, COL_C, lw["lb"], lw["hgrn_norm_g"], st["hgrn"], s0_layer, layer,
                                  done["hgrn"], batch, seq, chunk=SEQ_CHUNK)
    else:
        dec = jnp.concatenate([proj[:, COL_BQ:COL_BA], proj[:, COL_BG:COL_G], proj[:, COL_BA:COL_BA + LANES]],
                              axis=1)
        dec = jnp.pad(dec.reshape(batch, seq, -1), ((0, 0), (0, DEC_PAD - seq), (0, 0)))
        dec = dec.reshape(batch * DEC_PAD, -1)
        gate_col = COL_BA - COL_BQ
        dec_cols = (0, COL_BK - COL_BQ, COL_BV - COL_BQ, gate_col, gate_col + COL_G - COL_BG)
        unpad = lambda o: o.reshape(batch, DEC_PAD, -1)[:, :seq].reshape(m, -1)
        o_b, s_gla = gla_branch(dec, dec_cols, lw["gla_a2"], lw["gla_a_bias"], lw["gla_norm_g"], st["gla"],
                                s0_layer, layer, done["gla"], batch, DEC_PAD, chunk=DEC_PAD, valid=seq,
                                nseq=DEC_SEQS)
        o_c, s_hgrn = hgrn_branch(dec, gate_col + W_B, lw["lb"], lw["hgrn_norm_g"], st["hgrn"], s0_layer, layer,
                                  done["hgrn"], batch, DEC_PAD, chunk=DEC_PAD, valid=seq, nseq=DEC_SEQS)
        o_b, o_c = unpad(o_b), unpad(o_c)

    tm = 1024 if m % 1024 == 0 else m
    mtm = MERGE_TM if m % MERGE_TM == 0 else m
    merged = merge(y_a, g, o_b, o_c, lw["p_rwkv"], lw["p_gla"], lw["p_hgrn"], proj, tm=mtm, tn=MERGE_TN)
    mix = matmul(merged, lw["w_out"], tm=tm, tn=1024, name="w_out")
    x, h = add_rmsnorm_rmsnorm(x, mix, lw["norm_mix_post"], lw["norm_ffn_pre"], tm=NORM_TILE)
    s_shift = _last_rows(proj, batch, seq, 0)[:, :A_PROJ]
    return x, h, dict(shift=s_shift, rwkv=s_rwkv, gla=s_gla, hgrn=s_hgrn)


def kernel(x_prompt, x_sample, state_rwkv_shift, state_rwkv, state_gla, state_hgrn, state_ffn_conv, norm_mix_pre, norm_mix_post, norm_ffn_pre, norm_ffn_post, w_in, rwkv_mu, rwkv_w0, rwkv_w2, rwkv_a0, rwkv_a2, rwkv_g2, rwkv_k_k, rwkv_k_a, rwkv_r_k, rwkv_ln_g, rwkv_ln_b, gla_a2, gla_a_bias, gla_norm_g, hgrn_lb_logits, hgrn_norm_g, p_rwkv, p_gla, p_hgrn, w_out, ffn_up, ffn_gate, ffn_conv_w, ffn_conv_b, ffn_down):
    bp, tp, _ = x_prompt.shape
    bs, ts, _ = x_sample.shape
    sdt = state_rwkv.dtype

    probs = jax.nn.softmax(hgrn_lb_logits.astype(F32), axis=0)
    lb_all = jnp.cumsum(probs, axis=0) - probs[0]

    xp = x_prompt.reshape(bp * tp, D_MODEL)
    xs = x_sample.reshape(bs * ts, D_MODEL)
    names = ("shift", "rwkv", "gla", "hgrn", "conv")
    new_p = {n: [] for n in names}
    new_s = {n: [] for n in names}
    per_layer = dict(norm_mix_pre=norm_mix_pre, norm_mix_post=norm_mix_post, norm_ffn_pre=norm_ffn_pre,
                     norm_ffn_post=norm_ffn_post, rwkv_mu=rwkv_mu, rwkv_w0=rwkv_w0, rwkv_w2=rwkv_w2,
                     rwkv_a0=rwkv_a0, rwkv_a2=rwkv_a2, rwkv_g2=rwkv_g2, rwkv_k_k=rwkv_k_k, rwkv_k_a=rwkv_k_a,
                     rwkv_r_k=rwkv_r_k, rwkv_ln_g=rwkv_ln_g, rwkv_ln_b=rwkv_ln_b, gla_a2=gla_a2,
                     gla_a_bias=gla_a_bias, gla_norm_g=gla_norm_g, hgrn_norm_g=hgrn_norm_g,
                     ffn_conv_w=ffn_conv_w, ffn_conv_b=ffn_conv_b, lb=lb_all)
    w_in_t = jnp.swapaxes(w_in.astype(F32), 1, 2)
    ffn_up = ffn_up.astype(F32)
    ffn_gate = ffn_gate.astype(F32)
    st_p = dict(shift=jnp.zeros((bp, A_PROJ), F32), rwkv=jnp.zeros((bp, H_A, HS_A, HS_A), F32),
                gla=jnp.zeros((1, bp, H_B, DK_B, DV_B), F32), hgrn=jnp.zeros((1, bp, H_C, HD_C, HD_C), F32))
    conv_p = jnp.zeros((bp, CONV_W - 1, D_FF), F32)
    rwkv_s0 = jnp.transpose(state_rwkv.astype(F32), (0, 2, 3, 4, 1))
    done_p = dict(gla=None, hgrn=None)
    done_s = dict(gla=None, hgrn=None)
    for l in range(DEPTH):
        lw = {n: arr[l] for n, arr in per_layer.items()}
        for n, arr in (("p_rwkv", p_rwkv), ("p_gla", p_gla), ("p_hgrn", p_hgrn), ("w_out", w_out),
                       ("ffn_down", ffn_down)):
            lw[n] = arr[l].astype(BF16)
        st_s = dict(shift=state_rwkv_shift[l].astype(F32), rwkv=rwkv_s0,
                    gla=state_gla.astype(F32), hgrn=state_hgrn.astype(F32))

        if l == 0:
            hp = rmsnorm(xp, lw["norm_mix_pre"], tm=NORM_TILE, out_dtype=BF16)
            hs = rmsnorm(xs, lw["norm_mix_pre"], tm=NORM_TILE, out_dtype=BF16)
        proj_p, proj_s = proj_pair(hp, hs, w_in_t, l, tm=1024)
        xp, hp, out_p = _token_mix(xp, proj_p, bp, tp, True, st_p, 0, l, done_p, lw)
        xs, hs, out_s = _token_mix(xs, proj_s, bs, ts, False, st_s, l, l, done_s, lw)
        done_p = dict(gla=out_p["gla"], hgrn=out_p["hgrn"])
        done_s = dict(gla=out_s["gla"], hgrn=out_s["hgrn"])

        act_p, out_p["conv"], act_s, out_s["conv"] = ffn_up_pair(
            hp, hs, ffn_up, ffn_gate, l, conv_p, state_ffn_conv.astype(F32), lw["ffn_conv_w"], lw["ffn_conv_b"],
            bp, tp, bs, ts, tm=FFN_TM, tn=FFN_TN, sub=FFN_SUB)
        fp = matmul(act_p, lw["ffn_down"], tm=DOWN_TM, tn=512, name="ffn_down")
        fs = matmul(act_s, lw["ffn_down"], tm=DOWN_TM, tn=512, name="ffn_down")
        if l + 1 < DEPTH:
            xp, hp = add_rmsnorm_rmsnorm(xp, fp, lw["norm_ffn_post"], norm_mix_pre[l + 1], tm=NORM_TILE)
            xs, hs = add_rmsnorm_rmsnorm(xs, fs, lw["norm_ffn_post"], norm_mix_pre[l + 1], tm=NORM_TILE)
        else:
            xp = add_rmsnorm(xp, fp, lw["norm_ffn_post"], tm=NORM_TILE)
            xs = add_rmsnorm(xs, fs, lw["norm_ffn_post"], tm=NORM_TILE)
        for n in names:
            new_p[n].append(out_p[n])
            new_s[n].append(out_s[n])

    def collect(new, done):
        out = {n: jnp.stack(new[n]) for n in ("shift", "rwkv", "conv")}
        out.update(done)
        return {n: out[n].astype(sdt) for n in names}

    out_p = collect(new_p, done_p)
    out_s = collect(new_s, done_s)
    out_s["rwkv"] = jnp.transpose(out_s["rwkv"], (0, 4, 1, 2, 3))
    return (xp.reshape(bp, tp, D_MODEL), xs.reshape(bs, ts, D_MODEL),
            *(out_p[n] for n in names), *(out_s[n] for n in names))
```

```python
import functools

import jax
import jax.numpy as jnp
from jax import lax
from jax.experimental import pallas as pl
from jax.experimental.pallas import tpu as pltpu

F32 = jnp.float32
BF16 = jnp.bfloat16

D_MODEL = 4096
DEPTH = 2
HS_A = 64
W_A = 3 * D_MODEL // 8
H_A = W_A // HS_A
R_W, R_A, R_G = 128, 128, 256
A_PROJ = 3 * W_A + R_W + R_A + R_G
RWKV_LN_EPS = 64e-5
W_B = 5 * D_MODEL // 16
DV_B = 256
H_B = W_B // DV_B
DK_B = DV_B // 2
K_B = H_B * DK_B
R_GLA = 16
GLA_TAU = 16.0
W_C = 5 * D_MODEL // 16
HD_C = 128
H_C = W_C // HD_C
D_FF = 2 * D_MODEL
CONV_W = 3
RMS_EPS = 1e-6
HEAD_NORM_EPS = 1e-5

LANES = 128
SUBLANES = 8
VMEM_LIMIT_BYTES = 56 * 1024 * 1024

IN_COLS = A_PROJ + 2 * K_B + 2 * W_B + R_GLA + 4 * W_C + 3 * D_MODEL
COL_BQ = A_PROJ
COL_BK = COL_BQ + K_B
COL_BV = COL_BK + K_B
COL_BA = COL_BV + W_B
TAIL_SRC = COL_BA + R_GLA
PROJ_TN = 512
HEAD_TILES = -(-TAIL_SRC // PROJ_TN)
COL_BG = 7 * W_B
PROJ_COLS = COL_BG + IN_COLS - TAIL_SRC
TAIL_TILES = -(-(IN_COLS - TAIL_SRC) // PROJ_TN)
TAIL_TILE0 = PROJ_COLS // PROJ_TN - TAIL_TILES
TAIL_ROW0 = IN_COLS - TAIL_TILES * PROJ_TN
COL_C = COL_BG + W_B
COL_G = COL_C + 4 * W_C
assert PROJ_COLS % PROJ_TN == 0 and TAIL_TILE0 >= HEAD_TILES and TAIL_ROW0 % SUBLANES == 0
assert COL_BG % W_B == 0 and COL_C % W_C == 0 and COL_BA % LANES == 0


def _cparams(semantics):
    return pltpu.CompilerParams(dimension_semantics=semantics, vmem_limit_bytes=VMEM_LIMIT_BYTES)


def _rmsnorm_kernel(x_ref, g_ref, o_ref):
    x = x_ref[...]
    ms = jnp.mean(x * x, axis=-1, keepdims=True)
    o_ref[...] = (x * lax.rsqrt(ms + RMS_EPS) * g_ref[...]).astype(o_ref.dtype)


def rmsnorm(x, g, *, tm, out_dtype):
    m, d = x.shape
    return pl.pallas_call(
        _rmsnorm_kernel,
        grid=(m // tm,),
        in_specs=[pl.BlockSpec((tm, d), lambda i: (i, 0)), pl.BlockSpec((1, d), lambda i: (0, 0))],
        out_specs=pl.BlockSpec((tm, d), lambda i: (i, 0)),
        out_shape=jax.ShapeDtypeStruct((m, d), out_dtype),
        compiler_params=_cparams(("parallel",)),
        name="rmsnorm",
    )(x, g.reshape(1, d))


def _add_rmsnorm_kernel(x_ref, y_ref, g_ref, o_ref):
    y = y_ref[...]
    ms = jnp.mean(y * y, axis=-1, keepdims=True)
    o_ref[...] = x_ref[...] + y * lax.rsqrt(ms + RMS_EPS) * g_ref[...]


def add_rmsnorm(x, y, g, *, tm):
    m, d = x.shape
    return pl.pallas_call(
        _add_rmsnorm_kernel,
        grid=(m // tm,),
        in_specs=[pl.BlockSpec((tm, d), lambda i: (i, 0)), pl.BlockSpec((tm, d), lambda i: (i, 0)),
                  pl.BlockSpec((1, d), lambda i: (0, 0))],
        out_specs=pl.BlockSpec((tm, d), lambda i: (i, 0)),
        out_shape=jax.ShapeDtypeStruct((m, d), F32),
        compiler_params=_cparams(("parallel",)),
        name="add_rmsnorm",
    )(x, y, g.reshape(1, d))


def _add_rmsnorm_rmsnorm_kernel(x_ref, y_ref, g_ref, gn_ref, o_ref, h_ref):
    y = y_ref[...]
    ms = jnp.mean(y * y, axis=-1, keepdims=True)
    x = x_ref[...] + y * lax.rsqrt(ms + RMS_EPS) * g_ref[...]
    o_ref[...] = x
    ms = jnp.mean(x * x, axis=-1, keepdims=True)
    h_ref[...] = (x * lax.rsqrt(ms + RMS_EPS) * gn_ref[...]).astype(h_ref.dtype)


def add_rmsnorm_rmsnorm(x, y, g, g_next, *, tm):
    m, d = x.shape
    row = pl.BlockSpec((tm, d), lambda i: (i, 0))
    vec = pl.BlockSpec((1, d), lambda i: (0, 0))
    return pl.pallas_call(
        _add_rmsnorm_rmsnorm_kernel,
        grid=(m // tm,),
        in_specs=[row, row, vec, vec],
        out_specs=[row, row],
        out_shape=[jax.ShapeDtypeStruct((m, d), F32), jax.ShapeDtypeStruct((m, d), BF16)],
        compiler_params=_cparams(("parallel",)),
        name="add_rmsnorm_rmsnorm",
    )(x, y, g.reshape(1, d), g_next.reshape(1, d))


def _matmul_kernel(x_ref, w_ref, o_ref):
    o_ref[...] = jnp.dot(x_ref[...], w_ref[...], preferred_element_type=F32).astype(o_ref.dtype)


def matmul(x, w, *, tm, tn, out_dtype=F32, name="matmul"):
    m, k = x.shape
    n = w.shape[1]
    return pl.pallas_call(
        _matmul_kernel,
        grid=(m // tm, n // tn),
        in_specs=[pl.BlockSpec((tm, k), lambda i, j: (i, 0)), pl.BlockSpec((k, tn), lambda i, j: (0, j))],
        out_specs=pl.BlockSpec((tm, tn), lambda i, j: (i, j)),
        out_shape=jax.ShapeDtypeStruct((m, n), out_dtype),
        compiler_params=_cparams(("parallel", "parallel")),
        name=name,
    )(x, w)


def _proj_pair_kernel(xp_ref, xs_ref, wt_ref, op_ref, os_ref, w_bf, *, prompt_tiles):
    i = pl.program_id(1)

    @pl.when(i == 0)
    def _():
        w_bf[...] = wt_ref[...].astype(BF16)

    contract_last = (((1,), (1,)), ((), ()))

    @pl.when(i < prompt_tiles)
    def _():
        op_ref[...] = lax.dot_general(xp_ref[...], w_bf[...], contract_last, preferred_element_type=F32)

    @pl.when(i == prompt_tiles)
    def _():
        os_ref[...] = lax.dot_general(xs_ref[...], w_bf[...], contract_last, preferred_element_type=F32)


def proj_pair(xp, xs, w_in_t, layer, *, tm):
    mp, k = xp.shape
    ms = xs.shape[0]
    pt = mp // tm
    prow = lambda i: jnp.minimum(i, pt - 1)
    out_col = lambda j: jnp.where(j < HEAD_TILES, j, j + (TAIL_TILE0 - HEAD_TILES))
    w_row = lambda j: pl.multiple_of(
        jnp.where(j < HEAD_TILES, j * PROJ_TN, TAIL_ROW0 + (j - HEAD_TILES) * PROJ_TN), SUBLANES)
    return pl.pallas_call(
        functools.partial(_proj_pair_kernel, prompt_tiles=pt),
        grid=(HEAD_TILES + TAIL_TILES, pt + 1),
        in_specs=[pl.BlockSpec((tm, k), lambda j, i: (prow(i), 0)),
                  pl.BlockSpec((ms, k), lambda j, i: (0, 0), pipeline_mode=pl.Buffered(1)),
                  pl.BlockSpec((None, pl.Element(PROJ_TN), pl.Element(k)), lambda j, i: (layer, w_row(j), 0))],
        out_specs=[pl.BlockSpec((tm, PROJ_TN), lambda j, i: (prow(i), out_col(j))),
                   pl.BlockSpec((ms, PROJ_TN), lambda j, i: (0, out_col(j)))],
        out_shape=[jax.ShapeDtypeStruct((mp, PROJ_COLS), F32), jax.ShapeDtypeStruct((ms, PROJ_COLS), F32)],
        scratch_shapes=[pltpu.VMEM((PROJ_TN, k), BF16)],
        compiler_params=_cparams(("arbitrary", "arbitrary")),
        name="proj_pair",
    )(xp, xs, w_in_t)


def _merge_kernel(y_ref, g_ref, ob_ref, oc_ref, pa_ref, pb_ref, pc_ref, ga_ref, gb_ref, gc_ref, o_ref, oa_scr):
    @pl.when(pl.program_id(1) == 0)
    def _():
        oa_scr[...] = (y_ref[...] * g_ref[...]).astype(BF16)

    ma = jnp.dot(oa_scr[...], pa_ref[...], preferred_element_type=F32)
    mb = jnp.dot(ob_ref[...], pb_ref[...], preferred_element_type=F32)
    mc = jnp.dot(oc_ref[...], pc_ref[...], preferred_element_type=F32)
    merged = (jax.nn.sigmoid(ga_ref[...]) * ma + jax.nn.sigmoid(gb_ref[...]) * mb
              + jax.nn.sigmoid(gc_ref[...]) * mc)
    o_ref[...] = merged.astype(o_ref.dtype)


def merge(y_a, g_a, o_b, o_c, p_a, p_b, p_c, proj, *, tm, tn):
    m = y_a.shape[0]
    d = p_a.shape[1]
    gate0 = COL_G // tn
    gstep = d // tn

    def row(width):
        return pl.BlockSpec((tm, width), lambda i, j: (i, 0))

    def wcol(kdim):
        return pl.BlockSpec((kdim, tn), lambda i, j: (0, j))

    def gate(branch):
        return pl.BlockSpec((tm, tn), lambda i, j: (i, gate0 + branch * gstep + j))

    return pl.pallas_call(
        _merge_kernel,
        grid=(m // tm, d // tn),
        in_specs=[row(W_A), row(W_A), row(W_B), row(W_C), wcol(W_A), wcol(W_B), wcol(W_C),
                  gate(0), gate(1), gate(2)],
        out_specs=pl.BlockSpec((tm, tn), lambda i, j: (i, j)),
        out_shape=jax.ShapeDtypeStruct((m, d), BF16),
        scratch_shapes=[pltpu.VMEM((tm, W_A), BF16)],
        compiler_params=_cparams(("parallel", "arbitrary")),
        name="merge",
    )(y_a, g_a, o_b, o_c, p_a, p_b, p_c, proj, proj, proj)


def _softplus(x):
    return jnp.maximum(x, 0.0) + jnp.log(1.0 + jnp.exp(-jnp.abs(x)))


def _rwkv_gates(p, prev, mu_ref, w0_ref, w2_ref, a0_ref, a2_ref, g2_ref, r_ref, d_ref, k_ref, v_ref, a_ref, g_ref):
    ps = p + mu_ref[...] * (prev - p)
    o1 = 3 * W_A
    r_ref[...] = ps[:, :W_A]
    k_ref[...] = ps[:, W_A:2 * W_A]
    v_ref[...] = ps[:, 2 * W_A:o1]
    wc = ps[:, o1:o1 + R_W]
    ac = ps[:, o1 + R_W:o1 + R_W + R_A]
    gc = ps[:, o1 + R_W + R_A:]
    wl = w0_ref[...] + jnp.dot(jnp.tanh(wc).astype(BF16), w2_ref[...], preferred_element_type=F32)
    w = -_softplus(-wl) - 0.5
    d_ref[...] = jnp.exp(-jnp.exp(w))
    al = a0_ref[...] + jnp.dot(ac.astype(BF16), a2_ref[...], preferred_element_type=F32)
    a_ref[...] = jax.nn.sigmoid(al)
    g_ref[...] = jnp.dot(jax.nn.sigmoid(gc).astype(BF16), g2_ref[...], preferred_element_type=F32)


def _rwkv_prep_seq_kernel(p_ref, tail_ref, shift_ref, *rest, tiles_per_seq):
    i = pl.program_id(0)
    p = p_ref[...]
    first = (i % tiles_per_seq) == 0
    before = jnp.where(first, shift_ref[0], tail_ref[SUBLANES - 1:SUBLANES, :])
    row = lax.broadcasted_iota(jnp.int32, (p.shape[0], 1), 0)
    prev = jnp.where(row >= 1, pltpu.roll(p, 1, 0), before)
    _rwkv_gates(p, prev, *rest)


def _rwkv_prep_rows_kernel(p_ref, prev_ref, *rest):
    _rwkv_gates(p_ref[...], prev_ref[...], *rest)


def _rwkv_prep_call(kernel, lead_specs, lead_args, m, tm, mu, w0, w2, a0, a2, g2, name):
    full = lambda shape: pl.BlockSpec(shape, lambda i: (0, 0))
    out = jax.ShapeDtypeStruct((m, W_A), F32)
    return pl.pallas_call(
        kernel,
        grid=(m // tm,),
        in_specs=lead_specs + [full((1, A_PROJ)), full((1, W_A)), full((R_W, W_A)), full((1, W_A)),
                               full((R_A, W_A)), full((R_G, W_A))],
        out_specs=[pl.BlockSpec((tm, W_A), lambda i: (i, 0))] * 6,
        out_shape=[out] * 6,
        compiler_params=_cparams(("parallel",)),
        name=name,
    )(*lead_args, mu.reshape(1, A_PROJ), w0.reshape(1, W_A), w2.astype(BF16), a0.reshape(1, W_A),
      a2.astype(BF16), g2.astype(BF16))


def rwkv_prep_seq(proj, shift, weights, batch, seq, *, tm):
    tiles_per_seq = seq // tm
    nblk = tm // SUBLANES
    specs = [pl.BlockSpec((tm, A_PROJ), lambda i: (i, 0)),
             pl.BlockSpec((SUBLANES, A_PROJ), lambda i: (jnp.maximum(i * nblk - 1, 0), 0)),
             pl.BlockSpec((1, 1, A_PROJ), lambda i: (i // tiles_per_seq, 0, 0))]
    kern = functools.partial(_rwkv_prep_seq_kernel, tiles_per_seq=tiles_per_seq)
    return _rwkv_prep_call(kern, specs, (proj, proj, shift.reshape(batch, 1, A_PROJ)), batch * seq, tm,
                           *weights, name="rwkv_prep_seq")


def rwkv_prep_rows(proj, prev, weights, *, tm):
    specs = [pl.BlockSpec((tm, A_PROJ), lambda i: (i, 0)), pl.BlockSpec((tm, A_PROJ), lambda i: (i, 0))]
    return _rwkv_prep_call(_rwkv_prep_rows_kernel, specs, (proj, prev), proj.shape[0], tm, *weights,
                           name="rwkv_prep_rows")


def _rwkv_scan_kernel(r_ref, d_ref, k_ref, v_ref, a_ref, s0_ref, kk_ref, ka_ref, rk_ref, lng_ref, lnb_ref,
                      y_ref, st_ref, s_scr, o_scr, t_scr, b_scr, *, steps, tsub):
    tc = pl.program_id(1)

    @pl.when(tc == 0)
    def _():
        s_scr[...] = s0_ref[...]
        o_scr[...] = jnp.zeros_like(o_scr)
        b_scr[...] = jnp.zeros_like(b_scr)

    row = lax.broadcasted_iota(jnp.int32, (SUBLANES, 1), 0)

    def channel_rows(t, c0, n):
        tile, pos = t // tsub, lax.rem(t, tsub)
        return pl.ds((tile * HS_A + c0) * tsub + pos, n, stride=tsub)

    def prepare(t, slot):
        chans = channel_rows(t, 0, HS_A)
        r = r_ref[chans, :]
        k = k_ref[chans, :]
        a = a_ref[chans, :]
        kk = k * kk_ref[...]
        kk = kk * lax.rsqrt(jnp.maximum(jnp.sum(kk * kk, axis=0, keepdims=True), 1e-24))
        kp = k * (1.0 + (a - 1.0) * ka_ref[...])
        t_scr[slot, 0] = kk
        t_scr[slot, 1] = d_ref[chans, :]
        t_scr[slot, 2] = kk * a
        t_scr[slot, 3] = kp
        t_scr[slot, 4] = r
        bonus = jnp.sum(r * kp * rk_ref[...], axis=0, keepdims=True)
        b_scr[slot] = jnp.broadcast_to(bonus, b_scr.shape[1:])

    def update(t, slot):
        kk, d, kka, kp, r = (t_scr[slot, j] for j in range(5))
        for vo in range(HS_A // SUBLANES):
            v8 = v_ref[channel_rows(t, vo * SUBLANES, SUBLANES), :]
            o8 = jnp.zeros_like(v8)
            for vi in range(SUBLANES):
                s = s_scr[vo, vi]
                skk = jnp.sum(s * kk, axis=0, keepdims=True)
                s = s * d - skk * kka + v8[vi:vi + 1, :] * kp
                s_scr[vo, vi] = s
                o8 = jnp.where(row == vi, jnp.sum(s * r, axis=0, keepdims=True), o8)
            o_scr[slot, vo * SUBLANES:(vo + 1) * SUBLANES, :] = o8

    def finish(t, slot):
        chans = channel_rows(t, 0, HS_A)
        o = o_scr[slot]
        mean = jnp.mean(o, axis=0, keepdims=True)
        var = jnp.mean(jnp.square(o - mean), axis=0, keepdims=True)
        y = (o - mean) * lax.rsqrt(var + RWKV_LN_EPS) * lng_ref[...] + lnb_ref[...]
        y_ref[chans, :] = y + b_scr[slot, 0:1, :] * v_ref[chans, :]

    prepare(0, 0)

    def step(t, carry):
        slot = lax.rem(t, 2)
        other = 1 - slot
        finish(jnp.maximum(t - 1, 0), other)
        update(t, slot)
        prepare(jnp.minimum(t + 1, steps - 1), other)
        return carry

    lax.fori_loop(0, steps, step, 0)
    finish(steps - 1, (steps - 1) % 2)

    @pl.when(tc == pl.num_programs(1) - 1)
    def _():
        st_ref[...] = s_scr[...]


def rwkv_scan(r, d, k, v, a, s0, layer, kk_p, ka_p, rk_p, lng_p, lnb_p, *, steps, tsub):
    rows, lanes = r.shape
    c = HS_A
    nl = lanes // LANES
    vo = c // SUBLANES
    s0 = s0.reshape(s0.shape[0], nl, vo, SUBLANES, c, LANES)
    tok = pl.BlockSpec((steps * c, LANES), lambda l, i: (i, l))
    par = pl.BlockSpec((c, LANES), lambda l, i: (0, l))
    st_in = pl.BlockSpec((None, None, vo, SUBLANES, c, LANES), lambda l, i: (layer, l, 0, 0, 0, 0))
    st_out = pl.BlockSpec((None, vo, SUBLANES, c, LANES), lambda l, i: (l, 0, 0, 0, 0))
    y, s_t = pl.pallas_call(
        functools.partial(_rwkv_scan_kernel, steps=steps, tsub=tsub),
        grid=(nl, rows // (steps * c)),
        in_specs=[tok] * 5 + [st_in] + [par] * 5,
        out_specs=[tok, st_out],
        out_shape=[jax.ShapeDtypeStruct((rows, lanes), F32),
                   jax.ShapeDtypeStruct((nl, vo, SUBLANES, c, LANES), F32)],
        scratch_shapes=[pltpu.VMEM((vo, SUBLANES, c, LANES), F32), pltpu.VMEM((2, c, LANES), F32),
                        pltpu.VMEM((2, 5, c, LANES), F32), pltpu.VMEM((2, SUBLANES, LANES), F32)],
        compiler_params=_cparams(("parallel", "arbitrary")),
        name="rwkv_scan",
    )(r, d, k, v, a, s0, kk_p, ka_p, rk_p, lng_p, lnb_p)
    return y, s_t.reshape(nl, c, c, LANES)


LANE_SLOTS = 32
RELAYOUT_STEPS = 128


HEAD_PITCH = HS_A + SUBLANES


def _to_lane_tiles_kernel(x_ref, o_ref, xt_scr):
    nb, tt, _ = x_ref.shape
    for b in range(nb):
        xt = x_ref[b].T
        for h in range(H_A):
            xt_scr[b, h * HEAD_PITCH:h * HEAD_PITCH + HS_A, :] = xt[h * HS_A:(h + 1) * HS_A, :]
    idle = jnp.zeros((LANE_SLOTS - H_A, tt), F32)
    for c in range(HS_A):
        parts = []
        for b in range(nb):
            parts += [xt_scr[b, pl.ds(c, H_A, stride=HEAD_PITCH), :], idle]
        tile = jnp.concatenate(parts, axis=0).T
        o_ref[:, c, :, :] = tile.reshape(tt // SUBLANES, SUBLANES, LANES)


def _from_lane_tiles_kernel(y_ref, o_ref, xt_scr):
    nb, tt, _ = o_ref.shape
    for c in range(HS_A):
        heads = y_ref[:, c, :, :].reshape(tt, LANES).T
        for b in range(nb):
            xt_scr[b, pl.ds(c, H_A, stride=HEAD_PITCH), :] = heads[b * LANE_SLOTS:b * LANE_SLOTS + H_A, :]
    for b in range(nb):
        xt = jnp.concatenate([xt_scr[b, h * HEAD_PITCH:h * HEAD_PITCH + HS_A, :] for h in range(H_A)], axis=0)
        o_ref[b] = xt.T


def to_lane_tiles(x, batch, seq):
    tt = RELAYOUT_STEPS
    out = pl.pallas_call(
        _to_lane_tiles_kernel,
        grid=(seq // tt,),
        in_specs=[pl.BlockSpec((batch, tt, W_A), lambda i: (0, i, 0))],
        out_specs=pl.BlockSpec((tt // SUBLANES, HS_A, SUBLANES, LANES), lambda i: (i, 0, 0, 0)),
        out_shape=jax.ShapeDtypeStruct((seq // SUBLANES, HS_A, SUBLANES, LANES), F32),
        scratch_shapes=[pltpu.VMEM((batch, H_A * HEAD_PITCH, tt), F32)],
        compiler_params=_cparams(("parallel",)),
        name="to_lane_tiles",
    )(x.reshape(batch, seq, W_A))
    return out.reshape(seq * HS_A, LANES)


def from_lane_tiles(y, batch, seq):
    tt = RELAYOUT_STEPS
    out = pl.pallas_call(
        _from_lane_tiles_kernel,
        grid=(seq // tt,),
        in_specs=[pl.BlockSpec((tt // SUBLANES, HS_A, SUBLANES, LANES), lambda i: (i, 0, 0, 0))],
        out_specs=pl.BlockSpec((batch, tt, W_A), lambda i: (0, i, 0)),
        out_shape=jax.ShapeDtypeStruct((batch, seq, W_A), F32),
        scratch_shapes=[pltpu.VMEM((batch, H_A * HEAD_PITCH, tt), F32)],
        compiler_params=_cparams(("parallel",)),
        name="from_lane_tiles",
    )(y.reshape(seq // SUBLANES, HS_A, SUBLANES, LANES))
    return out.reshape(batch * seq, W_A)


def rwkv_recurrence_long(r, d, k, v, a, s0, params, batch, seq, *, steps):
    idle = LANE_SLOTS - H_A

    def param_tile(p):
        return jnp.tile(jnp.pad(p.reshape(H_A, HS_A).T, ((0, 0), (0, idle))), (1, batch))

    s0l = jnp.pad(jnp.transpose(s0, (2, 3, 0, 1)), ((0, 0), (0, 0), (0, 0), (0, idle)))
    s0l = s0l.reshape(1, 1, HS_A, HS_A, LANES)
    y, s_t = rwkv_scan(*(to_lane_tiles(z, batch, seq) for z in (r, d, k, v, a)), s0l, 0,
                       *(param_tile(p) for p in params), steps=steps, tsub=SUBLANES)
    s_t = s_t.reshape(HS_A, HS_A, batch, LANE_SLOTS)[..., :H_A]
    return from_lane_tiles(y, batch, seq), jnp.transpose(s_t, (2, 3, 0, 1))


def rwkv_recurrence_short(r, d, k, v, a, s0, layer, params, batch, seq):
    def lanes_of(z):
        z = jnp.transpose(z.reshape(batch, seq, H_A, HS_A), (3, 1, 2, 0))
        return z.reshape(HS_A * seq, H_A * batch)

    def param_tile(p):
        return jnp.repeat(p.reshape(H_A, HS_A).T, batch, axis=1)

    y, s_t = rwkv_scan(*(lanes_of(z) for z in (r, d, k, v, a)), s0, layer, *(param_tile(p) for p in params),
                       steps=seq, tsub=seq)
    y = jnp.transpose(y.reshape(HS_A, seq, H_A, batch), (3, 1, 2, 0)).reshape(batch * seq, W_A)
    return y, s_t


def _block_prefix_sum(x):
    row = lax.broadcasted_iota(jnp.int32, (SUBLANES, 1), 0)
    for shift in (1, 2, 4):
        x = x + jnp.where(row >= shift, pltpu.roll(x, shift, 0), 0.0)
    return x


def _chunk_scores(q, k, blocks):
    c = q.shape[0]
    nb = c // SUBLANES
    row = lax.broadcasted_iota(jnp.int32, (SUBLANES, 1), 0)
    lane = lax.broadcasted_iota(jnp.int32, (SUBLANES, c), 1)
    rows = []
    for ib in range(nb):
        lo = ib * SUBLANES
        qi = q[lo:lo + SUBLANES, :]
        ki = k[lo:lo + SUBLANES, :]
        bi = blocks[ib]
        att = jnp.zeros((SUBLANES, c), F32)
        for j in range(SUBLANES):
            decay = jnp.exp(jnp.minimum(bi - bi[j:j + 1, :], 0.0))
            col = jnp.sum(qi * ki[j:j + 1, :] * decay, axis=-1, keepdims=True)
            att = jnp.where(lane == lo + j, col, att)
        att = jnp.where(lane <= lo + row, att, 0.0)
        if ib > 0:
            ref = blocks[ib - 1][SUBLANES - 1:SUBLANES, :]
            qs = (qi * jnp.exp(bi - ref)).astype(BF16)
            ks = [k[jb * SUBLANES:(jb + 1) * SUBLANES, :] * jnp.exp(ref - blocks[jb]) for jb in range(ib)]
            ks.append(jnp.zeros((c - lo, q.shape[1]), F32))
            ks = jnp.concatenate(ks, axis=0).astype(BF16)
            att = att + lax.dot_general(qs, ks, (((1,), (1,)), ((), ())), preferred_element_type=F32)
        rows.append(att)
    return jnp.concatenate(rows, axis=0)


def _gated_chunk(q, k, v, log_a, s_prev):
    nb = q.shape[0] // SUBLANES
    blocks = []
    carry = None
    for ib in range(nb):
        bi = _block_prefix_sum(log_a[ib * SUBLANES:(ib + 1) * SUBLANES, :])
        if carry is not None:
            bi = bi + carry
        carry = bi[SUBLANES - 1:SUBLANES, :]
        blocks.append(bi)
    b = jnp.concatenate(blocks, axis=0)
    b_last = carry
    att = _chunk_scores(q, k, blocks)
    v_bf = v.astype(BF16)
    o = jnp.dot(att.astype(BF16), v_bf, preferred_element_type=F32)
    qe = (q * jnp.exp(b)).astype(BF16)
    o = o + jnp.dot(qe, s_prev.astype(BF16), preferred_element_type=F32)
    kd = (k * jnp.exp(b_last - b)).astype(BF16)
    e_col = jnp.transpose(jnp.broadcast_to(jnp.exp(b_last), (SUBLANES, q.shape[1])))[:, 0:1]
    s_new = s_prev * e_col + lax.dot_general(kd, v_bf, (((0,), (0,)), ((), ())), preferred_element_type=F32)
    return o, s_new


def _head_rms(o, g):
    return o * lax.rsqrt(jnp.mean(o * o, axis=-1, keepdims=True) + HEAD_NORM_EPS) * g


def _mask_padding(log_decay, valid, chunk):
    if valid >= chunk:
        return log_decay
    row = lax.broadcasted_iota(jnp.int32, (log_decay.shape[0], 1), 0)
    return jnp.where(lax.rem(row, chunk) < valid, log_decay, 0.0)


def _gla_kernel(q_ref, k_ref, v_ref, rg_ref, ac_ref, a2_ref, ab_ref, ng_ref, s0_ref, *rest, valid, chunk):
    o_ref, st_ref = rest[-2:]

    @pl.when(pl.program_id(1) == 0)
    def _():
        st_ref[...] = s0_ref[...]

    gl = jnp.dot(ac_ref[...].astype(BF16), a2_ref[...], preferred_element_type=F32) + ab_ref[...]
    log_a = _mask_padding(-_softplus(-gl) / GLA_TAU, valid, chunk)
    q = q_ref[...] * (DK_B ** -0.5)
    k = k_ref[...]
    v = v_ref[...]
    rg = rg_ref[...]
    for s in range(st_ref.shape[0]):
        rs = slice(s * chunk, (s + 1) * chunk)
        for h in range(H_B):
            ks = slice(h * DK_B, (h + 1) * DK_B)
            vs = slice(h * DV_B, (h + 1) * DV_B)
            o, s_new = _gated_chunk(q[rs, ks], k[rs, ks], v[rs, vs], log_a[rs, ks], st_ref[s, h])
            st_ref[s, h] = s_new
            o_ref[rs, vs] = (_head_rms(o, ng_ref[...]) * jax.nn.silu(rg[rs, vs])).astype(o_ref.dtype)


def _hgrn_kernel(q_ref, z_ref, i_ref, g_ref, lb_ref, ng_ref, s0_ref, *rest, valid, chunk):
    o_ref, st_ref = rest[-2:]

    @pl.when(pl.program_id(1) == 0)
    def _():
        st_ref[...] = s0_ref[...]

    q = jax.nn.silu(q_ref[...])
    lb = lb_ref[...]
    f = lb + (1.0 - lb) * jax.nn.sigmoid(z_ref[...])
    log_f = _mask_padding(jnp.log(jnp.maximum(f, 1e-30)), valid, chunk)
    k = 1.0 - f
    v = i_ref[...]
    g = g_ref[...]
    for s in range(st_ref.shape[0]):
        rs = slice(s * chunk, (s + 1) * chunk)
        for h in range(H_C):
            hs = slice(h * HD_C, (h + 1) * HD_C)
            o, s_new = _gated_chunk(q[rs, hs], k[rs, hs], v[rs, hs], log_f[rs, hs], st_ref[s, h])
            st_ref[s, h] = s_new
            o_ref[rs, hs] = (_head_rms(o, ng_ref[...]) * jax.nn.sigmoid(g[rs, hs])).astype(o_ref.dtype)


def _seq_spec(rows, width, col_block, nchunks):
    return pl.BlockSpec((rows, width), lambda b, c: (b * nchunks + c, col_block))


def _const_spec(shape):
    return pl.BlockSpec(shape, lambda b, c: (0,) * len(shape))


def _state_in_spec(layer, nseq, tail):
    return pl.BlockSpec((None, nseq) + tail, lambda b, c: (layer, b) + (0,) * len(tail))


def _mixer_call(kernel, in_specs, args, width, state_tail, nseq, batch, seq, nch, layer, out_states, name):
    rows = seq // nch * nseq
    in_specs = list(in_specs)
    args = list(args)
    aliases = {}
    if out_states is not None:
        aliases[len(args)] = 1
        in_specs.append(pl.BlockSpec(memory_space=pl.ANY))
        args.append(out_states)
    state_out = pl.BlockSpec((None, nseq) + state_tail, lambda b, c: (layer, b) + (0,) * len(state_tail))
    return pl.pallas_call(
        kernel,
        grid=(batch // nseq, nch),
        in_specs=in_specs,
        out_specs=[pl.BlockSpec((rows, width), lambda b, c: (b * nch + c, 0)), state_out],
        out_shape=[jax.ShapeDtypeStruct((batch * seq, width), BF16),
                   jax.ShapeDtypeStruct((DEPTH, batch) + state_tail, F32)],
        input_output_aliases=aliases,
        compiler_params=_cparams(("parallel", "arbitrary")),
        name=name,
    )(*args)


def gla_branch(src, cols, a2, a_bias, norm_g, s0, s0_layer, layer, out_states, batch, seq, *, chunk,
               valid=None, nseq=1):
    nch = seq // chunk
    assert nseq == 1 or nch == 1
    valid = chunk if valid is None else valid
    cq, ck, cv, cg, ca = cols
    rows = chunk * nseq
    a2p = jnp.zeros((LANES, K_B), F32).at[:R_GLA].set(a2).astype(BF16)
    tail = (H_B, DK_B, DV_B)
    in_specs = [_seq_spec(rows, K_B, cq // K_B, nch), _seq_spec(rows, K_B, ck // K_B, nch),
                _seq_spec(rows, W_B, cv // W_B, nch), _seq_spec(rows, W_B, cg // W_B, nch),
                _seq_spec(rows, LANES, ca // LANES, nch),
                _const_spec((LANES, K_B)), _const_spec((1, K_B)), _const_spec((1, DV_B)),
                _state_in_spec(s0_layer, nseq, tail)]
    args = (src, src, src, src, src, a2p, a_bias.reshape(1, K_B), norm_g.reshape(1, DV_B), s0)
    return _mixer_call(functools.partial(_gla_kernel, valid=valid, chunk=chunk), in_specs, args, W_B, tail,
                       nseq, batch, seq, nch, layer, out_states, "gla")


def hgrn_branch(src, col0, lb, norm_g, s0, s0_layer, layer, out_states, batch, seq, *, chunk, valid=None,
                nseq=1):
    nch = seq // chunk
    assert nseq == 1 or nch == 1
    valid = chunk if valid is None else valid
    c0 = col0 // W_C
    rows = chunk * nseq
    tail = (H_C, HD_C, HD_C)
    in_specs = [_seq_spec(rows, W_C, c0 + j, nch) for j in range(4)] + [
        _const_spec((1, W_C)), _const_spec((1, HD_C)), _state_in_spec(s0_layer, nseq, tail)]
    args = (src, src, src, src, lb.reshape(1, W_C), norm_g.reshape(1, HD_C), s0)
    return _mixer_call(functools.partial(_hgrn_kernel, valid=valid, chunk=chunk), in_specs, args, W_C, tail,
                       nseq, batch, seq, nch, layer, out_states, "hgrn")


def _conv_gelu_gate(u, u1, u2, gate, cw_ref, cb_ref):
    c = cb_ref[...] + u2 * cw_ref[0:1, :]
    c = c + u1 * cw_ref[1:2, :]
    c = c + u * cw_ref[2:3, :]
    return (jax.nn.gelu(c, approximate=True) * gate).astype(BF16)


def _select_rows(sel, x):
    return jnp.dot(sel, x, precision=lax.Precision.HIGHEST, preferred_element_type=F32)


def _ffn_up_pair_kernel(xp_ref, xs_ref, wu_ref, wg_ref, stp_ref, sts_ref, cw_ref, cb_ref,
                        actp_ref, nsp_ref, acts_ref, nss_ref, wu_bf, wg_bf, carry,
                        *, prompt_tiles, tiles_per_seq, sub, short_seq):
    i = pl.program_id(1)

    @pl.when(i == 0)
    def _():
        wu_bf[...] = wu_ref[...].astype(BF16)
        wg_bf[...] = wg_ref[...].astype(BF16)

    @pl.when(i < prompt_tiles)
    def _():
        first = (i % tiles_per_seq) == 0
        pm2 = jnp.where(first, stp_ref[0, 0:1, :], carry[0:1, :])
        pm1 = jnp.where(first, stp_ref[0, 1:2, :], carry[1:2, :])
        row = lax.broadcasted_iota(jnp.int32, (sub, 1), 0)
        for s in range(xp_ref.shape[0] // sub):
            rows = slice(s * sub, (s + 1) * sub)
            x = xp_ref[rows, :]
            u = jnp.dot(x, wu_bf[...], preferred_element_type=F32)
            gate = jnp.dot(x, wg_bf[...], preferred_element_type=F32)
            u1 = jnp.where(row >= 1, pltpu.roll(u, 1, 0), pm1)
            u2 = jnp.where(row >= 2, pltpu.roll(u, 2, 0), jnp.where(row == 1, pm1, pm2))
            actp_ref[rows, :] = _conv_gelu_gate(u, u1, u2, gate, cw_ref, cb_ref)
            pm2 = u[sub - 2:sub - 1, :]
            pm1 = u[sub - 1:sub, :]
        carry[0:1, :] = pm2
        carry[1:2, :] = pm1
        nsp_ref[0, 0:1, :] = pm2
        nsp_ref[0, 1:2, :] = pm1

    @pl.when(i == prompt_tiles)
    def _():
        x = xs_ref[...]
        u = jnp.dot(x, wu_bf[...], preferred_element_type=F32)
        gate = jnp.dot(x, wg_bf[...], preferred_element_type=F32)
        m = u.shape[0]
        nseq = m // short_seq
        r = lax.broadcasted_iota(jnp.int32, (m, nseq), 0)
        start = lax.broadcasted_iota(jnp.int32, (m, nseq), 1) * short_seq
        at0 = (r == start).astype(F32)
        at1 = (r == start + 1).astype(F32)
        st0 = sts_ref[:, 0, :]
        st1 = sts_ref[:, 1, :]
        tap1 = _select_rows(at0, st1)
        tap2 = _select_rows(at0, st0) + _select_rows(at1, st1)
        pos = lax.rem(lax.broadcasted_iota(jnp.int32, (m, 1), 0), short_seq)
        u1 = jnp.where(pos >= 1, pltpu.roll(u, 1, 0), tap1)
        u2 = jnp.where(pos >= 2, pltpu.roll(u, 2, 0), tap2)
        acts_ref[...] = _conv_gelu_gate(u, u1, u2, gate, cw_ref, cb_ref)
        rr = lax.broadcasted_iota(jnp.int32, (nseq, m), 1)
        end = lax.broadcasted_iota(jnp.int32, (nseq, m), 0) * short_seq + short_seq
        nss_ref[:, 0, :] = _select_rows((rr == end - 2).astype(F32), u)
        nss_ref[:, 1, :] = _select_rows((rr == end - 1).astype(F32), u)


def ffn_up_pair(xp, xs, w_up, w_gate, layer, conv_p, conv_s, conv_w, conv_b, bp, tp, bs, ts, *, tm, tn, sub):
    mp, k = xp.shape
    ms = xs.shape[0]
    pt = mp // tm
    tiles_per_seq = tp // tm
    prow = lambda i: jnp.minimum(i, pt - 1)
    wspec = pl.BlockSpec((None, k, tn), lambda j, i: (layer, 0, j))
    pstate = pl.BlockSpec((1, CONV_W - 1, tn), lambda j, i: (prow(i) // tiles_per_seq, 0, j))
    kern = functools.partial(_ffn_up_pair_kernel, prompt_tiles=pt, tiles_per_seq=tiles_per_seq, sub=sub,
                             short_seq=ts)
    return pl.pallas_call(
        kern,
        grid=(D_FF // tn, pt + 1),
        in_specs=[pl.BlockSpec((tm, k), lambda j, i: (prow(i), 0)),
                  pl.BlockSpec((ms, k), lambda j, i: (0, 0), pipeline_mode=pl.Buffered(1)),
                  wspec, wspec, pstate,
                  pl.BlockSpec((None, bs, CONV_W - 1, tn), lambda j, i: (layer, 0, 0, j)),
                  pl.BlockSpec((CONV_W, tn), lambda j, i: (0, j)), pl.BlockSpec((1, tn), lambda j, i: (0, j))],
        out_specs=[pl.BlockSpec((tm, tn), lambda j, i: (prow(i), j)), pstate,
                   pl.BlockSpec((ms, tn), lambda j, i: (0, j)),
                   pl.BlockSpec((bs, CONV_W - 1, tn), lambda j, i: (0, 0, j))],
        out_shape=[jax.ShapeDtypeStruct((mp, D_FF), BF16), jax.ShapeDtypeStruct((bp, CONV_W - 1, D_FF), F32),
                   jax.ShapeDtypeStruct((ms, D_FF), BF16), jax.ShapeDtypeStruct((bs, CONV_W - 1, D_FF), F32)],
        scratch_shapes=[pltpu.VMEM((k, tn), BF16), pltpu.VMEM((k, tn), BF16), pltpu.VMEM((SUBLANES, tn), F32)],
        compiler_params=_cparams(("arbitrary", "arbitrary")),
        name="ffn_up_pair",
    )(xp, xs, w_up, w_gate, conv_p, conv_s, conv_w, conv_b.reshape(1, D_FF))


NORM_TILE = 256
PREP_TILE = 128
MERGE_TM = 1024
MERGE_TN = 256
FFN_TM = 1024
FFN_SUB = 512
FFN_TN = 256
DOWN_TM = 512
SCAN_STEPS = 32
SEQ_CHUNK = 64
DEC_PAD = SUBLANES
DEC_SEQS = 1


def _shifted_rows(x, first, seq):
    w = x.shape[-1]
    xs = x.reshape(-1, seq, w)
    return jnp.concatenate([first[:, None, :], xs[:, :-1]], axis=1).reshape(-1, w)


def _last_rows(x, batch, seq, back):
    return jnp.take(x, jnp.arange(batch) * seq + (seq - 1 - back), axis=0)


def _token_mix(x, proj, batch, seq, long_seq, st, s0_layer, layer, done, lw):
    m = batch * seq
    prep_w = (lw["rwkv_mu"], lw["rwkv_w0"], lw["rwkv_w2"], lw["rwkv_a0"], lw["rwkv_a2"], lw["rwkv_g2"])
    if long_seq:
        r, d, k, v, a, g = rwkv_prep_seq(proj, st["shift"], prep_w, batch, seq, tm=PREP_TILE)
    else:
        prev = _shifted_rows(proj[:, :A_PROJ], st["shift"], seq)
        r, d, k, v, a, g = rwkv_prep_rows(proj, prev, prep_w, tm=PREP_TILE)
    params = (lw["rwkv_k_k"], lw["rwkv_k_a"], lw["rwkv_r_k"], lw["rwkv_ln_g"], lw["rwkv_ln_b"])
    if long_seq:
        y_a, s_rwkv = rwkv_recurrence_long(r, d, k, v, a, st["rwkv"], params, batch, seq, steps=SCAN_STEPS)
    else:
        y_a, s_rwkv = rwkv_recurrence_short(r, d, k, v, a, st["rwkv"], s0_layer, params, batch, seq)

    if long_seq:
        o_b, s_gla = gla_branch(proj, (COL_BQ, COL_BK, COL_BV, COL_BG, COL_BA), lw["gla_a2"], lw["gla_a_bias"],
                                lw["gla_norm_g"], st["gla"], s0_layer, layer, done["gla"], batch, seq,
                                chunk=SEQ_CHUNK)
        o_c, s_hgrn = hgrn_branch(proj, COL_C, lw["lb"], lw["hgrn_norm_g"], st["hgrn"], s0_layer, layer,
                                  done["hgrn"], batch, seq, chunk=SEQ_CHUNK)
    else:
        dec = jnp.concatenate([proj[:, COL_BQ:COL_BA], proj[:, COL_BG:COL_G], proj[:, COL_BA:COL_BA + LANES]],
                              axis=1)
        dec = jnp.pad(dec.reshape(batch, seq, -1), ((0, 0), (0, DEC_PAD - seq), (0, 0)))
        dec = dec.reshape(batch * DEC_PAD, -1)
        gate_col = COL_BA - COL_BQ
        dec_cols = (0, COL_BK - COL_BQ, COL_BV - COL_BQ, gate_col, gate_col + COL_G - COL_BG)
        unpad = lambda o: o.reshape(batch, DEC_PAD, -1)[:, :seq].reshape(m, -1)
        o_b, s_gla = gla_branch(dec, dec_cols, lw["gla_a2"], lw["gla_a_bias"], lw["gla_norm_g"], st["gla"],
                                s0_layer, layer, done["gla"], batch, DEC_PAD, chunk=DEC_PAD, valid=seq,
                                nseq=DEC_SEQS)
        o_c, s_hgrn = hgrn_branch(dec, gate_col + W_B, lw["lb"], lw["hgrn_norm_g"], st["hgrn"], s0_layer, layer,
                                  done["hgrn"], batch, DEC_PAD, chunk=DEC_PAD, valid=seq, nseq=DEC_SEQS)
        o_b, o_c = unpad(o_b), unpad(o_c)

    tm = 1024 if m % 1024 == 0 else m
    mtm = MERGE_TM if m % MERGE_TM == 0 else m
    merged = merge(y_a, g, o_b, o_c, lw["p_rwkv"], lw["p_gla"], lw["p_hgrn"], proj, tm=mtm, tn=MERGE_TN)
    mix = matmul(merged, lw["w_out"], tm=tm, tn=1024, name="w_out")
    x, h = add_rmsnorm_rmsnorm(x, mix, lw["norm_mix_post"], lw["norm_ffn_pre"], tm=NORM_TILE)
    s_shift = _last_rows(proj, batch, seq, 0)[:, :A_PROJ]
    return x, h, dict(shift=s_shift, rwkv=s_rwkv, gla=s_gla, hgrn=s_hgrn)


def kernel(x_prompt, x_sample, state_rwkv_shift, state_rwkv, state_gla, state_hgrn, state_ffn_conv, norm_mix_pre, norm_mix_post, norm_ffn_pre, norm_ffn_post, w_in, rwkv_mu, rwkv_w0, rwkv_w2, rwkv_a0, rwkv_a2, rwkv_g2, rwkv_k_k, rwkv_k_a, rwkv_r_k, rwkv_ln_g, rwkv_ln_b, gla_a2, gla_a_bias, gla_norm_g, hgrn_lb_logits, hgrn_norm_g, p_rwkv, p_gla, p_hgrn, w_out, ffn_up, ffn_gate, ffn_conv_w, ffn_conv_b, ffn_down):
    bp, tp, _ = x_prompt.shape
    bs, ts, _ = x_sample.shape
    sdt = state_rwkv.dtype

    probs = jax.nn.softmax(hgrn_lb_logits.astype(F32), axis=0)
    lb_all = jnp.cumsum(probs, axis=0) - probs[0]

    xp = x_prompt.reshape(bp * tp, D_MODEL)
    xs = x_sample.reshape(bs * ts, D_MODEL)
    names = ("shift", "rwkv", "gla", "hgrn", "conv")
    new_p = {n: [] for n in names}
    new_s = {n: [] for n in names}
    per_layer = dict(norm_mix_pre=norm_mix_pre, norm_mix_post=norm_mix_post, norm_ffn_pre=norm_ffn_pre,
                     norm_ffn_post=norm_ffn_post, rwkv_mu=rwkv_mu, rwkv_w0=rwkv_w0, rwkv_w2=rwkv_w2,
                     rwkv_a0=rwkv_a0, rwkv_a2=rwkv_a2, rwkv_g2=rwkv_g2, rwkv_k_k=rwkv_k_k, rwkv_k_a=rwkv_k_a,
                     rwkv_r_k=rwkv_r_k, rwkv_ln_g=rwkv_ln_g, rwkv_ln_b=rwkv_ln_b, gla_a2=gla_a2,
                     gla_a_bias=gla_a_bias, gla_norm_g=gla_norm_g, hgrn_norm_g=hgrn_norm_g,
                     ffn_conv_w=ffn_conv_w, ffn_conv_b=ffn_conv_b, lb=lb_all)
    w_in_t = jnp.swapaxes(w_in.astype(F32), 1, 2)
    ffn_up = ffn_up.astype(F32)
    ffn_gate = ffn_gate.astype(F32)
    st_p = dict(shift=jnp.zeros((bp, A_PROJ), F32), rwkv=jnp.zeros((bp, H_A, HS_A, HS_A), F32),
                gla=jnp.zeros((1, bp, H_B, DK_B, DV_B), F32), hgrn=jnp.zeros((1, bp, H_C, HD_C, HD_C), F32))
    conv_p = jnp.zeros((bp, CONV_W - 1, D_FF), F32)
    rwkv_s0 = jnp.transpose(state_rwkv.astype(F32), (0, 2, 3, 4, 1))
    done_p = dict(gla=None, hgrn=None)
    done_s = dict(gla=None, hgrn=None)
    for l in range(DEPTH):
        lw = {n: arr[l] for n, arr in per_layer.items()}
        for n, arr in (("p_rwkv", p_rwkv), ("p_gla", p_gla), ("p_hgrn", p_hgrn), ("w_out", w_out),
                       ("ffn_down", ffn_down)):
            lw[n] = arr[l].astype(BF16)
        st_s = dict(shift=state_rwkv_shift[l].astype(F32), rwkv=rwkv_s0,
                    gla=state_gla.astype(F32), hgrn=state_hgrn.astype(F32))

        if l == 0:
            hp = rmsnorm(xp, lw["norm_mix_pre"], tm=NORM_TILE, out_dtype=BF16)
            hs = rmsnorm(xs, lw["norm_mix_pre"], tm=NORM_TILE, out_dtype=BF16)
        proj_p, proj_s = proj_pair(hp, hs, w_in_t, l, tm=1024)
        xp, hp, out_p = _token_mix(xp, proj_p, bp, tp, True, st_p, 0, l, done_p, lw)
        xs, hs, out_s = _token_mix(xs, proj_s, bs, ts, False, st_s, l, l, done_s, lw)
        done_p = dict(gla=out_p["gla"], hgrn=out_p["hgrn"])
        done_s = dict(gla=out_s["gla"], hgrn=out_s["hgrn"])

        act_p, out_p["conv"], act_s, out_s["conv"] = ffn_up_pair(
            hp, hs, ffn_up, ffn_gate, l, conv_p, state_ffn_conv.astype(F32), lw["ffn_conv_w"], lw["ffn_conv_b"],
            bp, tp, bs, ts, tm=FFN_TM, tn=FFN_TN, sub=FFN_SUB)
        fp = matmul(act_p, lw["ffn_down"], tm=DOWN_TM, tn=512, name="ffn_down")
        fs = matmul(act_s, lw["ffn_down"], tm=DOWN_TM, tn=512, name="ffn_down")
        if l + 1 < DEPTH:
            xp, hp = add_rmsnorm_rmsnorm(xp, fp, lw["norm_ffn_post"], norm_mix_pre[l + 1], tm=NORM_TILE)
            xs, hs = add_rmsnorm_rmsnorm(xs, fs, lw["norm_ffn_post"], norm_mix_pre[l + 1], tm=NORM_TILE)
        else:
            xp = add_rmsnorm(xp, fp, lw["norm_ffn_post"], tm=NORM_TILE)
            xs = add_rmsnorm(xs, fs, lw["norm_ffn_post"], tm=NORM_TILE)
        for n in names:
            new_p[n].append(out_p[n])
            new_s[n].append(out_s[n])

    def collect(new, done):
        out = {n: jnp.stack(new[n]) for n in ("shift", "rwkv", "conv")}
        out.update(done)
        return {n: out[n].astype(sdt) for n in names}

    out_p = collect(new_p, done_p)
    out_s = collect(new_s, done_s)
    out_s["rwkv"] = jnp.transpose(out_s["rwkv"], (0, 4, 1, 2, 3))
    return (xp.reshape(bp, tp, D_MODEL), xs.reshape(bs, ts, D_MODEL),
            *(out_p[n] for n in names), *(out_s[n] for n in names))
```
